```python
import math
import jax, jax.numpy as jnp
from jax import lax
import numpy as np

D_MODEL = 4096
BATCH = 1
SEQ = 16384
DEPTH = 4

CTX_LEN = 256
GRID_W = 64
NORM_EPS = 1e-6

A_HEADS = 24
A_HEAD_DIM = 64
A_WIDTH = A_HEADS * A_HEAD_DIM
A_DECAY_RANK = 64
A_ICLR_RANK = 64
A_GATE_RANK = 224
A_GN_EPS = 64e-5
A_COLS = 3 * A_WIDTH + 2 * A_DECAY_RANK + 2 * A_ICLR_RANK + A_GATE_RANK

POOL_WINDOWS = (2, 4, 8, 16)
POOL_GROUPS = 4
POOL_GROUP_W = 384
B_WIDTH = POOL_GROUPS * POOL_GROUP_W

C_Q_HEADS = 16
C_KV_HEADS = 4
C_HEAD_DIM = 128
C_WIDTH = C_Q_HEADS * C_HEAD_DIM
C_KV_WIDTH = C_KV_HEADS * C_HEAD_DIM
ATTN_RADIUS = 128
ATTN_BLOCK = 128
ROPE_BASE = 10000.0
NEG_INF = -1e30

N_BRANCH = 3
GATE_RANK = 256
MOD_RANK = 256
D_FF = (8 * D_MODEL + 3 * 256 - 1) // (3 * 256) * 256

OFF_B = A_COLS
OFF_Q = OFF_B + B_WIDTH
OFF_K = OFF_Q + C_WIDTH
OFF_V = OFF_K + C_KV_WIDTH
OFF_G = OFF_V + C_KV_WIDTH
IN_COLS = OFF_G + GATE_RANK

kernel_name = "hybrid_rwkv7_pool_swa_prefix_dit"

F32 = jnp.float32


def _rms_norm(x, g):
    xf = x.astype(F32)
    y = xf * lax.rsqrt(jnp.mean(xf * xf, axis=-1, keepdims=True) + NORM_EPS)
    return (y * g.astype(F32)).astype(x.dtype)


def _split_cols(p):
    return (p[..., :OFF_B], p[..., OFF_B:OFF_Q], p[..., OFF_Q:OFF_K],
            p[..., OFF_K:OFF_V], p[..., OFF_V:OFF_G], p[..., OFF_G:])


def _centred_token_shift(u, mu):
    zero = jnp.zeros_like(u[:, :1])
    prev = jnp.concatenate([zero, u[:, :-1]], axis=1)
    nxt = jnp.concatenate([u[:, 1:], zero], axis=1)
    return u + mu * (0.5 * (prev + nxt) - u)


def _rwkv_inputs(pa, mu, w0, w_up, a0, a_up, k_k, k_a):
    pa = _centred_token_shift(pa, mu).astype(F32)
    B, T, _ = pa.shape
    C, H, N = A_WIDTH, A_HEADS, A_HEAD_DIM
    r = pa[..., :C]
    k = pa[..., C:2 * C]
    v = pa[..., 2 * C:3 * C]
    o = 3 * C
    wd = pa[..., o:o + 2 * A_DECAY_RANK].reshape(B, T, 2, A_DECAY_RANK)
    o += 2 * A_DECAY_RANK
    ad = pa[..., o:o + 2 * A_ICLR_RANK].reshape(B, T, 2, A_ICLR_RANK)
    o += 2 * A_ICLR_RANK
    gd = pa[..., o:o + A_GATE_RANK]
    w_log = -jax.nn.softplus(-(w0 + jnp.einsum('btdr,drc->btdc', jnp.tanh(wd), w_up))) - 0.5
    decay = jnp.exp(-jnp.exp(w_log))
    a = jax.nn.sigmoid(a0 + jnp.einsum('btdr,drc->btdc', ad, a_up))
    kk = (k * k_k).reshape(B, T, H, N)
    kk = kk / jnp.maximum(jnp.sqrt(jnp.sum(kk * kk, axis=-1, keepdims=True)), 1e-12)
    k_dir = k[:, :, None] * (1.0 + (a - 1.0) * k_a)
    kka = kk.reshape(B, T, 1, C) * a
    heads2 = lambda t: t.reshape(B, T, 2, H, N)
    return {"r": r.reshape(B, T, H, N), "v": v.reshape(B, T, H, N), "kk": kk,
            "k": heads2(k_dir), "kka": heads2(kka), "w": heads2(decay), "gd": gd}


def _delta_scan(S0, f, d, reverse, with_out):
    seqs = [f["w"][:, :, d], f["k"][:, :, d], f["v"], f["kk"], f["kka"][:, :, d]]
    if with_out:
        seqs.append(f["r"])
    xs = tuple(jnp.moveaxis(t, 1, 0) for t in seqs)

    def step(S, inp):
        w_t, k_t, v_t, kk_t, kka_t = inp[:5]
        S = (S * w_t[:, :, None, :]
             - jnp.einsum('bhvk,bhk->bhv', S, kk_t)[..., None] * kka_t[:, :, None, :]
             + v_t[..., None] * k_t[:, :, None, :])
        y = jnp.einsum('bhvk,bhk->bhv', S, inp[5]) if with_out else None
        return S, y

    S_T, ys = lax.scan(step, S0, xs, reverse=reverse)
    return S_T, (jnp.moveaxis(ys, 0, 1) if with_out else None)


def _rwkv_readout(ro, f, g_up, r_k, ln_g, ln_b):
    B, T, H, N = ro.shape
    mu = jnp.mean(ro, axis=-1, keepdims=True)
    var = jnp.mean(jnp.square(ro - mu), axis=-1, keepdims=True)
    yn = ((ro - mu) * lax.rsqrt(var + A_GN_EPS)).reshape(B, T, A_WIDTH) * ln_g + ln_b
    bonus = jnp.sum(f["r"][:, :, None] * f["k"] * r_k.reshape(H, N).astype(F32), axis=(2, 4))
    bonus = (bonus[..., None] * f["v"]).reshape(B, T, A_WIDTH)
    g = jax.nn.sigmoid(f["gd"]) @ g_up
    return (yn + bonus) * g


def _rwkv_mixer(pa, pa_c, mu, w0, w_up, a0, a_up, g_up, k_k, k_a, r_k, ln_g, ln_b, ctx_out):
    f = _rwkv_inputs(pa, mu, w0, w_up, a0, a_up, k_k, k_a)
    fc = _rwkv_inputs(pa_c, mu, w0, w_up, a0, a_up, k_k, k_a)
    S0 = jnp.zeros((pa.shape[0], A_HEADS, A_HEAD_DIM, A_HEAD_DIM), F32)
    ys, ys_c = [], []
    for d, rev in enumerate((False, True)):
        S_c, y_c = _delta_scan(S0, fc, d, rev, ctx_out)
        _, y = _delta_scan(S_c, f, d, rev, True)
        ys.append(y)
        ys_c.append(y_c)
    y_lat = _rwkv_readout(ys[0] + ys[1], f, g_up, r_k, ln_g, ln_b).astype(pa.dtype)
    y_ctx = (_rwkv_readout(ys_c[0] + ys_c[1], fc, g_up, r_k, ln_g, ln_b).astype(pa.dtype)
             if ctx_out else None)
    return y_lat, y_ctx


def _pool_mixer(u, pool_w, pool_scale):
    B, T, _ = u.shape
    ug = u.reshape(B, T, POOL_GROUPS, POOL_GROUP_W).astype(F32)
    cs = jnp.concatenate([jnp.zeros_like(ug[:, :1]), jnp.cumsum(ug, axis=1)], axis=1)
    t = jnp.arange(T)
    outs = []
    for gi, win in enumerate(POOL_WINDOWS):
        lo = jnp.clip(t - win // 2, 0, T - 1)
        hi = jnp.clip(t + win // 2 - 1, 0, T - 1)
        s = cs[:, hi + 1, gi] - cs[:, lo, gi]
        cnt = (hi - lo + 1).astype(F32)
        outs.append(s / cnt[None, :, None] - ug[:, :, gi])
    pooled = jnp.stack(outs, axis=2)
    y = jnp.einsum('btgc,gcd->btgd', pooled, pool_w).reshape(B, T, B_WIDTH) * pool_scale
    return y.astype(u.dtype)


def _rope_tables(T):
    rows = T // GRID_W
    row = jnp.repeat(jnp.arange(rows), GRID_W).astype(F32)
    col = jnp.tile(jnp.arange(GRID_W), rows).astype(F32)
    half = C_HEAD_DIM // 2
    inv = ROPE_BASE ** (-jnp.arange(0, half, 2, dtype=F32) / half)
    ang_r = row[:, None] * inv[None]
    ang_c = col[:, None] * inv[None]
    return jnp.cos(ang_r), jnp.sin(ang_r), jnp.cos(ang_c), jnp.sin(ang_c)


def _apply_rope(x, tabs):
    cr, sr, cc, sc = (t[None, :, None, :] for t in tabs)
    xf = x.astype(F32)
    half = x.shape[-1] // 2
    qtr = half // 2

    def rot(u, cs, sn):
        u1, u2 = u[..., :qtr], u[..., qtr:]
        return jnp.concatenate([u1 * cs - u2 * sn, u2 * cs + u1 * sn], axis=-1)

    return jnp.concatenate([rot(xf[..., :half], cr, sr), rot(xf[..., half:], cc, sc)], axis=-1).astype(x.dtype)


def _local_ctx_attention(q, k, v, kc, vc, sink):
    B, T, Hq, dh = q.shape
    L = kc.shape[1]
    G = Hq // C_KV_HEADS
    nb = T // ATTN_BLOCK
    scale = dh ** -0.5
    qb = q.reshape(B, nb, ATTN_BLOCK, C_KV_HEADS, G, dh)

    def windows(t):
        tp = jnp.pad(t, ((0, 0), (ATTN_BLOCK, ATTN_BLOCK), (0, 0), (0, 0)))
        tp = tp.reshape(B, nb + 2, ATTN_BLOCK, C_KV_HEADS, dh)
        return jnp.concatenate([tp[:, :-2], tp[:, 1:-1], tp[:, 2:]], axis=2)

    kw, vw = windows(k), windows(v)
    s_loc = jnp.einsum('bnqhgd,bnkhd->bnhgqk', qb, kw, preferred_element_type=F32) * scale
    blk = jnp.arange(nb)[:, None]
    qpos = blk * ATTN_BLOCK + jnp.arange(ATTN_BLOCK)[None]
    kpos = (blk - 1) * ATTN_BLOCK + jnp.arange(3 * ATTN_BLOCK)[None]
    valid = ((jnp.abs(qpos[:, :, None] - kpos[:, None, :]) <= ATTN_RADIUS)
             & (kpos[:, None, :] >= 0) & (kpos[:, None, :] < T))
    s_loc = jnp.where(valid[None, :, None, None], s_loc, NEG_INF)
    s_ctx = jnp.einsum('bnqhgd,bchd->bnhgqc', qb, kc, preferred_element_type=F32) * scale
    s_sink = jnp.broadcast_to(sink.reshape(1, 1, C_KV_HEADS, G, 1, 1).astype(F32),
                              (B, nb, C_KV_HEADS, G, ATTN_BLOCK, 1))
    p = jax.nn.softmax(jnp.concatenate([s_loc, s_ctx, s_sink], axis=-1), axis=-1)
    nloc = 3 * ATTN_BLOCK
    p_loc = p[..., :nloc].astype(v.dtype)
    p_ctx = p[..., nloc:nloc + L].astype(v.dtype)
    o = (jnp.einsum('bnhgqk,bnkhd->bnqhgd', p_loc, vw)
         + jnp.einsum('bnhgqc,bchd->bnqhgd', p_ctx, vc))
    return o.reshape(B, T, Hq * dh)


def _ctx_attention(qc, kc, vc, sink):
    B, L, Hq, dh = qc.shape
    G = Hq // C_KV_HEADS
    q5 = qc.reshape(B, L, C_KV_HEADS, G, dh)
    s = jnp.einsum('bqhgd,bkhd->bhgqk', q5, kc, preferred_element_type=F32) * dh ** -0.5
    s_sink = jnp.broadcast_to(sink.reshape(1, C_KV_HEADS, G, 1, 1).astype(F32), (B, C_KV_HEADS, G, L, 1))
    p = jax.nn.softmax(jnp.concatenate([s, s_sink], axis=-1), axis=-1)[..., :-1]
    o = jnp.einsum('bhgqk,bkhd->bqhgd', p.astype(vc.dtype), vc)
    return o.reshape(B, L, Hq * dh)


def _merge(gd, ys, gate_up, w_brs, w_out):
    D = w_out.shape[0]
    acc = None
    for i, (y, wb) in enumerate(zip(ys, w_brs)):
        term = jax.nn.sigmoid(gd @ gate_up[:, i * D:(i + 1) * D]) * (y @ wb)
        acc = term if acc is None else acc + term
    return acc @ w_out


def _swiglu(h, w1, w3, w2):
    return (jax.nn.silu(h @ w1) * (h @ w3)) @ w2


def _layer(x, ctx, mod, mod_c, norm_g, w_in, shift_mu, w0, w_up, a0, a_up, g_up, k_k, k_a,
           r_k, ln_g, ln_b, pool_w, pool_scale, sink, gate_up, w_br_a, w_br_b, w_br_c,
           w_out, w1, w3, w2, rope, ctx_out):
    B, T, _ = x.shape
    L = ctx.shape[1]
    sh1, sc1, gt1, sh2, sc2, gt2 = jnp.split(mod[:, None, :], 6, axis=-1)
    csh1, csc1, cgt1, csh2, csc2, cgt2 = jnp.split(mod_c[:, None, :], 6, axis=-1)

    h = _rms_norm(x, norm_g[0]) * (1.0 + sc1) + sh1
    hc = _rms_norm(ctx, norm_g[0]) * (1.0 + csc1) + csh1
    pa, pb, pq, pk, pv, pg = _split_cols(h @ w_in)
    ca, cb, cq, ck, cv, cg = _split_cols(hc @ w_in)

    y_a, yc_a = _rwkv_mixer(pa, ca, shift_mu, w0, w_up, a0, a_up, g_up, k_k, k_a,
                            r_k, ln_g, ln_b, ctx_out)
    y_b = _pool_mixer(pb, pool_w, pool_scale)
    q = _apply_rope(pq.reshape(B, T, C_Q_HEADS, C_HEAD_DIM), rope)
    k = _apply_rope(pk.reshape(B, T, C_KV_HEADS, C_HEAD_DIM), rope)
    v = pv.reshape(B, T, C_KV_HEADS, C_HEAD_DIM)
    kc = ck.reshape(B, L, C_KV_HEADS, C_HEAD_DIM)
    vc = cv.reshape(B, L, C_KV_HEADS, C_HEAD_DIM)
    y_c = _local_ctx_attention(q, k, v, kc, vc, sink)

    w_brs = (w_br_a, w_br_b, w_br_c)
    mix = _merge(pg, (y_a, y_b, y_c), gate_up, w_brs, w_out)
    x = x + gt1 * _rms_norm(mix, norm_g[1])
    h2 = _rms_norm(x, norm_g[2]) * (1.0 + sc2) + sh2
    x = x + gt2 * _rms_norm(_swiglu(h2, w1, w3, w2), norm_g[3])

    if ctx_out:
        yc_b = _pool_mixer(cb, pool_w, pool_scale)
        yc_c = _ctx_attention(cq.reshape(B, L, C_Q_HEADS, C_HEAD_DIM), kc, vc, sink)
        mixc = _merge(cg, (yc_a, yc_b, yc_c), gate_up, w_brs, w_out)
        ctx = ctx + cgt1 * _rms_norm(mixc, norm_g[1])
        h2c = _rms_norm(ctx, norm_g[2]) * (1.0 + csc2) + csh2
        ctx = ctx + cgt2 * _rms_norm(_swiglu(h2c, w1, w3, w2), norm_g[3])
    return x, ctx


def setup_inputs(seed: int = 0) -> dict:
    key = jax.random.key(seed)
    ks = iter(jax.random.split(key, 40))
    L, D = DEPTH, D_MODEL

    def nrm(shape, scale):
        return jax.random.normal(next(ks), shape, F32) * scale

    def unif(shape, lo, hi):
        return jax.random.uniform(next(ks), shape, F32, lo, hi)

    return {
        "x": nrm((BATCH, SEQ, D), 1.0),
        "c": nrm((BATCH, D), 1.0),
        "ctx": nrm((BATCH, CTX_LEN, D), 1.0),
        "c_ctx": nrm((D,), 1.0),
        "mod_down": nrm((L, D, MOD_RANK), D ** -0.5),
        "mod_up": nrm((L, MOD_RANK, 6 * D), 0.5 * MOD_RANK ** -0.5),
        "mod_b": nrm((L, 6 * D), 0.01),
        "norm_g": 1.0 + nrm((L, 4, D), 0.05),
        "w_in": nrm((L, D, IN_COLS), D ** -0.5),
        "shift_mu": unif((L, A_COLS), 0.0, 1.0),
        "rwkv_w0": unif((L, 2, A_WIDTH), -4.0, 0.0),
        "rwkv_w_up": nrm((L, 2, A_DECAY_RANK, A_WIDTH), A_DECAY_RANK ** -0.5),
        "rwkv_a0": nrm((L, 2, A_WIDTH), 0.5),
        "rwkv_a_up": nrm((L, 2, A_ICLR_RANK, A_WIDTH), A_ICLR_RANK ** -0.5),
        "rwkv_g_up": nrm((L, A_GATE_RANK, A_WIDTH), A_GATE_RANK ** -0.5),
        "rwkv_k_k": 0.85 + nrm((L, A_WIDTH), 0.05),
        "rwkv_k_a": 1.0 + nrm((L, A_WIDTH), 0.05),
        "rwkv_r_k": nrm((L, A_WIDTH), 0.1),
        "rwkv_ln_g": 1.0 + nrm((L, A_WIDTH), 0.05),
        "rwkv_ln_b": nrm((L, A_WIDTH), 0.01),
        "pool_w": nrm((L, POOL_GROUPS, POOL_GROUP_W, POOL_GROUP_W), POOL_GROUP_W ** -0.5),
        "pool_scale": 1.0 + nrm((L, B_WIDTH), 0.1),
        "attn_sink": nrm((L, C_Q_HEADS), 0.5),
        "gate_up": nrm((L, GATE_RANK, N_BRANCH * D), GATE_RANK ** -0.5),
        "w_branch_a": nrm((L, A_WIDTH, D), A_WIDTH ** -0.5),
        "w_branch_b": nrm((L, B_WIDTH, D), B_WIDTH ** -0.5),
        "w_branch_c": nrm((L, C_WIDTH, D), C_WIDTH ** -0.5),
        "w_out": nrm((L, D, D), D ** -0.5),
        "ffn_w1": nrm((L, D, D_FF), D ** -0.5),
        "ffn_w3": nrm((L, D, D_FF), D ** -0.5),
        "ffn_w2": nrm((L, D_FF, D), D_FF ** -0.5),
    }


def reference(x, c, ctx, c_ctx, mod_down, mod_up, mod_b, norm_g, w_in, shift_mu,
              rwkv_w0, rwkv_w_up, rwkv_a0, rwkv_a_up, rwkv_g_up, rwkv_k_k, rwkv_k_a,
              rwkv_r_k, rwkv_ln_g, rwkv_ln_b, pool_w, pool_scale, attn_sink, gate_up,
              w_branch_a, w_branch_b, w_branch_c, w_out, ffn_w1, ffn_w3, ffn_w2):
    rope = _rope_tables(x.shape[1])
    s_lat = jax.nn.silu(c)
    s_ctx = jax.nn.silu(c_ctx)[None]
    for l in range(DEPTH):
        mod = (s_lat @ mod_down[l]) @ mod_up[l] + mod_b[l]
        mod_c = (s_ctx @ mod_down[l]) @ mod_up[l] + mod_b[l]
        x, ctx = _layer(x, ctx, mod, mod_c, norm_g[l], w_in[l], shift_mu[l],
                        rwkv_w0[l], rwkv_w_up[l], rwkv_a0[l], rwkv_a_up[l], rwkv_g_up[l],
                        rwkv_k_k[l], rwkv_k_a[l], rwkv_r_k[l], rwkv_ln_g[l], rwkv_ln_b[l],
                        pool_w[l], pool_scale[l], attn_sink[l], gate_up[l],
                        w_branch_a[l], w_branch_b[l], w_branch_c[l], w_out[l],
                        ffn_w1[l], ffn_w3[l], ffn_w2[l], rope, l < DEPTH - 1)
    return x
```

```python
import functools
import math

import jax
import jax.numpy as jnp
from jax import lax
from jax.experimental import pallas as pl
from jax.experimental.pallas import tpu as pltpu

F32 = jnp.float32
BF16 = jnp.bfloat16
HIGHEST = lax.Precision.HIGHEST

LANES = 128
SUBLANES = 8
VMEM_LIMIT = 56 * 1024 * 1024

NORM_EPS = 1e-6
GRID_W = 64
ROPE_BASE = 10000.0
NEG_INF = -1e30

A_HEADS = 24
A_HEAD_DIM = 64
A_WIDTH = A_HEADS * A_HEAD_DIM
A_DECAY_RANK = 64
A_ICLR_RANK = 64
A_GATE_RANK = 224
A_GN_EPS = 64e-5
A_COLS = 3 * A_WIDTH + 2 * A_DECAY_RANK + 2 * A_ICLR_RANK + A_GATE_RANK
SCAN_CHUNK = 64
POOL_WINDOWS = (2, 4, 8, 16)
POOL_GROUP_W = 384
B_WIDTH = len(POOL_WINDOWS) * POOL_GROUP_W
POOL_HALO = 8
C_Q_HEADS = 16
C_KV_HEADS = 4
C_GROUP = C_Q_HEADS // C_KV_HEADS
C_HEAD_DIM = 128
C_WIDTH = C_Q_HEADS * C_HEAD_DIM
C_KV_WIDTH = C_KV_HEADS * C_HEAD_DIM
ATTN_BLOCK = 128
GATE_RANK = 256
N_BRANCH = 3

OFF_B = A_COLS
OFF_Q = OFF_B + B_WIDTH
OFF_K = OFF_Q + C_WIDTH
OFF_V = OFF_K + C_KV_WIDTH
OFF_G = OFF_V + C_KV_WIDTH
IN_COLS = OFF_G + GATE_RANK
A_PAD = 5120
A_GD_OFF = 3 * A_WIDTH + 2 * A_DECAY_RANK + 2 * A_ICLR_RANK
A_GD_PAD = A_PAD - A_GD_OFF
P_K = A_PAD
P_V = P_K + C_KV_WIDTH
P_B = P_V + C_KV_WIDTH
P_Q = P_B + B_WIDTH
P_G = P_Q + C_WIDTH
P_COLS = P_G + GATE_RANK


def _dot(a, b, precision=None):
    return jnp.dot(a, b, precision=precision, preferred_element_type=F32)


def _dot_nt(a, b):
    return lax.dot_general(a, b, (((1,), (1,)), ((), ())), preferred_element_type=F32)


def _dot_tn(a, b):
    return lax.dot_general(a, b, (((0,), (0,)), ((), ())), preferred_element_type=F32)


def _tile(n, cap, mult):
    best = None
    for t in range(mult, min(n, cap) + 1, mult):
        if n % t == 0:
            best = t
    assert best is not None, (n, cap, mult)
    return best


def _params(*sem):
    return pltpu.CompilerParams(dimension_semantics=sem, vmem_limit_bytes=VMEM_LIMIT)


def _sigmoid(x):
    return 1.0 / (1.0 + jnp.exp(-x))


def _silu(x):
    return x * _sigmoid(x)


def _head_block_ones():
    r = lax.broadcasted_iota(jnp.int32, (LANES, LANES), 0)
    c = lax.broadcasted_iota(jnp.int32, (LANES, LANES), 1)
    return jnp.where((r // A_HEAD_DIM) == (c // A_HEAD_DIM), 1.0, 0.0).astype(F32)


def _head_sums(x, ones_bd):
    return jnp.concatenate(
        [_dot(x[:, j:j + LANES], ones_bd, HIGHEST) for j in range(0, x.shape[1], LANES)], axis=1)


def _mod_kernel(c_ref, down_ref, up_ref, b_ref, o_ref):
    s = _silu(c_ref[...]).astype(BF16)
    low = _dot(s, down_ref[0]).astype(BF16)
    o_ref[0] = _dot(low, up_ref[0]) + b_ref[0]


def _modulation(c8, down, up, bias):
    depth, d, rank = down.shape
    n = up.shape[2]
    return pl.pallas_call(
        _mod_kernel,
        grid=(depth,),
        in_specs=[pl.BlockSpec((SUBLANES, d), lambda l: (0, 0)),
                  pl.BlockSpec((1, d, rank), lambda l: (l, 0, 0)),
                  pl.BlockSpec((1, rank, n), lambda l: (l, 0, 0)),
                  pl.BlockSpec((1, 1, n), lambda l: (l, 0, 0))],
        out_specs=pl.BlockSpec((1, SUBLANES, n), lambda l: (l, 0, 0)),
        out_shape=jax.ShapeDtypeStruct((depth, SUBLANES, n), F32),
        compiler_params=_params("arbitrary"),
        name="modulation",
    )(c8, down, up, bias.reshape(depth, 1, n))


def _rms(x, g):
    return x * lax.rsqrt(jnp.mean(x * x, axis=-1, keepdims=True) + NORM_EPS) * g


def _resid_norm_kernel(*refs, ctx_tiles, has_m, emit_h):
    refs = list(refs)
    x_ref = refs.pop(0)
    m_ref = refs.pop(0) if has_m else None
    g_ref = refs.pop(0)
    mod_ref = refs.pop(0)
    is_ctx = pl.program_id(0) < ctx_tiles

    def pick(i):
        return jnp.where(is_ctx, mod_ref[i:i + 1, :], mod_ref[i + 1:i + 2, :])

    x = x_ref[...]
    if has_m:
        x = x + pick(0) * _rms(m_ref[...], g_ref[0:1, :])
        refs.pop(0)[...] = x
    if emit_h:
        h = _rms(x, g_ref[1:2, :]) * (1.0 + pick(2)) + pick(4)
        refs.pop(0)[...] = h.astype(BF16)


def _resid_norm(x, m, g2, mod6, *, n_ctx, emit_h):
    R, D = x.shape
    te = _tile(math.gcd(n_ctx, R - n_ctx), 256, 16)
    has_m = m is not None
    row = pl.BlockSpec((te, D), lambda i: (i, 0))
    ins = [x] + ([m] if has_m else []) + [g2, mod6]
    in_specs = [row] * (2 if has_m else 1) + [pl.BlockSpec((2, D), lambda i: (0, 0)),
                                              pl.BlockSpec((6, D), lambda i: (0, 0))]
    out_shape, out_specs = [], []
    if has_m:
        out_shape.append(jax.ShapeDtypeStruct((R, D), F32))
        out_specs.append(row)
    if emit_h:
        out_shape.append(jax.ShapeDtypeStruct((R, D), BF16))
        out_specs.append(row)
    outs = pl.pallas_call(
        functools.partial(_resid_norm_kernel, ctx_tiles=n_ctx // te, has_m=has_m, emit_h=emit_h),
        grid=(R // te,),
        in_specs=in_specs, out_specs=out_specs, out_shape=out_shape,
        compiler_params=_params("parallel"),
        name="resid_norm",
    )(*ins)
    outs = list(outs)
    x1 = outs.pop(0) if has_m else None
    h = outs.pop(0) if emit_h else None
    return x1, h


def _mm_kernel(x_ref, w_ref, o_ref, *scratch, nk):
    part = _dot(x_ref[...].astype(BF16), w_ref[...])
    if nk == 1:
        o_ref[...] = part.astype(o_ref.dtype)
        return
    acc_ref, = scratch
    k = pl.program_id(2)

    @pl.when(k == 0)
    def _():
        acc_ref[...] = part

    @pl.when(k > 0)
    def _():
        acc_ref[...] += part

    @pl.when(k == nk - 1)
    def _():
        o_ref[...] = acc_ref[...].astype(o_ref.dtype)


def _matmul(x, w, *, out_dtype=F32, tm_cap=1280, tn_cap=512, tk_cap=4096, name="matmul"):
    M, K = x.shape
    N = w.shape[1]
    tm = _tile(M, tm_cap, 16)
    tn = _tile(N, tn_cap, LANES)
    tk = _tile(K, tk_cap, LANES)
    nk = K // tk
    return pl.pallas_call(
        functools.partial(_mm_kernel, nk=nk),
        grid=(M // tm, N // tn, nk),
        in_specs=[pl.BlockSpec((tm, tk), lambda i, j, k: (i, k)),
                  pl.BlockSpec((tk, tn), lambda i, j, k: (k, j))],
        out_specs=pl.BlockSpec((tm, tn), lambda i, j, k: (i, j)),
        out_shape=jax.ShapeDtypeStruct((M, N), out_dtype),
        scratch_shapes=[pltpu.VMEM((tm, tn), F32)] if nk > 1 else [],
        compiler_params=_params("parallel", "parallel", "arbitrary"),
        name=name,
    )(x, w)


def _ffn_up_kernel(h_ref, w1_ref, w3_ref, o_ref):
    h = h_ref[...]
    o_ref[...] = (_silu(_dot(h, w1_ref[...])) * _dot(h, w3_ref[...])).astype(o_ref.dtype)


def _ffn_up(h, w1, w3):
    M, K = h.shape
    N = w1.shape[1]
    tm = _tile(M, 1280, 16)
    tn = _tile(N, 512, LANES)
    wspec = pl.BlockSpec((K, tn), lambda i, j: (0, j))
    return pl.pallas_call(
        _ffn_up_kernel,
        grid=(M // tm, N // tn),
        in_specs=[pl.BlockSpec((tm, K), lambda i, j: (i, 0)), wspec, wspec],
        out_specs=pl.BlockSpec((tm, tn), lambda i, j: (i, j)),
        out_shape=jax.ShapeDtypeStruct((M, N), BF16),
        compiler_params=_params("parallel", "parallel"),
        name="ffn_up",
    )(h, w1, w3)


def _merge_kernel(pg_ref, ya_ref, yb_ref, yc_ref, ga_ref, gb_ref, gc_ref, wa_ref, wb_ref, wc_ref, o_ref):
    pg = pg_ref[...].astype(BF16)
    acc = _sigmoid(_dot(pg, ga_ref[...])) * _dot(ya_ref[...], wa_ref[...])
    acc += _sigmoid(_dot(pg, gb_ref[...])) * _dot(yb_ref[...], wb_ref[...])
    acc += _sigmoid(_dot(pg, gc_ref[...])) * _dot(yc_ref[...], wc_ref[...])
    o_ref[...] = acc.astype(o_ref.dtype)


def _merge(p, ya, yb, yc, gate_up, wa, wb, wc):
    R = p.shape[0]
    D = wa.shape[1]
    tm = _tile(R, 1280, 16)
    tn = _tile(D, 512, LANES)
    nj = D // tn

    def rows(width):
        return pl.BlockSpec((tm, width), lambda i, j: (i, 0))

    def gate(branch):
        return pl.BlockSpec((GATE_RANK, tn), lambda i, j: (0, branch * nj + j))

    def wcol(width):
        return pl.BlockSpec((width, tn), lambda i, j: (0, j))

    return pl.pallas_call(
        _merge_kernel,
        grid=(R // tm, nj),
        in_specs=[pl.BlockSpec((tm, GATE_RANK), lambda i, j: (i, P_G // GATE_RANK)),
                  rows(A_WIDTH), rows(B_WIDTH), rows(C_WIDTH),
                  gate(0), gate(1), gate(2), wcol(A_WIDTH), wcol(B_WIDTH), wcol(C_WIDTH)],
        out_specs=pl.BlockSpec((tm, tn), lambda i, j: (i, j)),
        out_shape=jax.ShapeDtypeStruct((R, D), BF16),
        compiler_params=_params("parallel", "parallel"),
        name="merge",
    )(p, ya, yb, yc, gate_up, gate_up, gate_up, wa, wb, wc)


def _segment_flags(i, ctx_tiles, n_tiles):
    first = jnp.logical_or(i == 0, i == ctx_tiles)
    last = jnp.logical_or(i == ctx_tiles - 1, i == n_tiles - 1)
    return first, last


def _rwkv_prep_kernel(u_ref, prev_ref, next_ref, mu_ref, w0_ref, wup_ref, a0_ref, aup_ref, kk_ref, ka_ref,
                      r_out, v_out, kap_out, lw0_out, lw1_out, k0_out, k1_out, b0_out, b1_out, gs_out,
                      *, ctx_tiles, n_tiles):
    i = pl.program_id(0)
    first, last = _segment_flags(i, ctx_tiles, n_tiles)
    u = u_ref[...]
    tm = u.shape[0]
    rid = lax.broadcasted_iota(jnp.int32, (tm, 1), 0)
    prev_row = jnp.where(first, 0.0, prev_ref[SUBLANES - 1:SUBLANES, :])
    next_row = jnp.where(last, 0.0, next_ref[0:1, :])
    prev = jnp.where(rid == 0, prev_row, pltpu.roll(u, 1, axis=0))
    nxt = jnp.where(rid == tm - 1, next_row, pltpu.roll(u, tm - 1, axis=0))
    s = u + mu_ref[...] * (0.5 * (prev + nxt) - u)

    W = A_WIDTH
    r = s[:, 0:W]
    k = s[:, W:2 * W]
    v = s[:, 2 * W:3 * W]
    o = 3 * W
    wd = jnp.tanh(s[:, o:o + LANES]).astype(BF16)
    ad = s[:, o + LANES:o + 2 * LANES].astype(BF16)
    gd = s[:, A_GD_OFF:A_PAD]

    ones_bd = _head_block_ones()
    kk = k * kk_ref[...]
    nrm = jnp.sqrt(_head_sums(kk * kk, ones_bd))
    kk = kk / jnp.maximum(nrm, 1e-12)

    r_out[...] = r
    v_out[...] = v
    kap_out[...] = kk
    gs_out[...] = _sigmoid(gd).astype(BF16)
    for d, (lw_out, k_out, b_out) in enumerate(((lw0_out, k0_out, b0_out), (lw1_out, k1_out, b1_out))):
        z = w0_ref[d:d + 1, :] + _dot(wd, wup_ref[d])
        w_log = -(jnp.maximum(-z, 0.0) + jnp.log(1.0 + jnp.exp(-jnp.abs(z)))) - 0.5
        lw_out[...] = -jnp.exp(w_log)
        a = _sigmoid(a0_ref[d:d + 1, :] + _dot(ad, aup_ref[d]))
        k_out[...] = k * (1.0 + (a - 1.0) * ka_ref[...])
        b_out[...] = kk * a


def _rwkv_prep(p, mu, w0, wup2, a0, aup2, k_k, k_a, *, n_ctx):
    R = p.shape[0]
    tm = _tile(math.gcd(n_ctx, R - n_ctx), 128, 16)
    n_tiles = R // tm
    hb = tm // SUBLANES
    n_hblocks = R // SUBLANES
    W = A_WIDTH

    def const(shape):
        return pl.BlockSpec(shape, lambda i: (0,) * len(shape))

    wide = pl.BlockSpec((tm, W), lambda i: (i, 0))
    f32w = jax.ShapeDtypeStruct((R, W), F32)
    return pl.pallas_call(
        functools.partial(_rwkv_prep_kernel, ctx_tiles=n_ctx // tm, n_tiles=n_tiles),
        grid=(n_tiles,),
        in_specs=[pl.BlockSpec((tm, A_PAD), lambda i: (i, 0)),
                  pl.BlockSpec((SUBLANES, A_PAD), lambda i: (jnp.maximum(i * hb - 1, 0), 0)),
                  pl.BlockSpec((SUBLANES, A_PAD), lambda i: (jnp.minimum((i + 1) * hb, n_hblocks - 1), 0)),
                  const((1, A_PAD)), const((2, W)), const((2, LANES, W)), const((2, W)), const((2, LANES, W)),
                  const((1, W)), const((1, W))],
        out_specs=[wide] * 9 + [pl.BlockSpec((tm, A_GD_PAD), lambda i: (i, 0))],
        out_shape=[f32w] * 9 + [jax.ShapeDtypeStruct((R, A_GD_PAD), BF16)],
        compiler_params=_params("parallel"),
        name="rwkv_prep",
    )(p, p, p, mu, w0, wup2, a0, aup2, k_k, k_a)


def _scan_kernel(lw_ref, k_ref, b_ref, kap_ref, v_ref, r_ref, y_ref, st_ref, *, reverse, npairs):
    C = SCAN_CHUNK
    N = A_HEAD_DIM

    @pl.when(pl.program_id(1) == 0)
    def _():
        st_ref[...] = jnp.zeros_like(st_ref)

    row = lax.broadcasted_iota(jnp.int32, (C, C), 0)
    col = lax.broadcasted_iota(jnp.int32, (C, C), 1)
    earlier = (col > row) if reverse else (col < row)
    diag = col == row
    incl = jnp.logical_or(earlier, diag)
    tri = jnp.where(incl, 1.0, 0.0).astype(F32)
    eye = jnp.where(diag, 1.0, 0.0).astype(F32)

    def same_block(n):
        return (row // n) == (col // n)

    last = 0 if reverse else C - 1

    hs = (slice(0, N), slice(N, 2 * N))
    heads = range(2 * npairs)
    kap_h, r_h, k_t, b_t, k_e, b_e, e_tot, v = ([] for _ in range(8))
    for p in range(npairs):
        sl = slice(p * LANES, (p + 1) * LANES)
        lw = lw_ref[:, sl]
        c = _dot(tri, lw, HIGHEST)
        ctot = c[last:last + 1, :]
        e_nc = jnp.exp(-c)
        e_tc = jnp.exp(ctot - c)
        kap_p = kap_ref[:, sl] * jnp.exp(c - lw)
        r_p = r_ref[:, sl] * jnp.exp(c)
        k_p = k_ref[:, sl]
        b_p = b_ref[:, sl]
        v_p = v_ref[:, sl]
        e_p = jnp.exp(ctot)
        for h in hs:
            kap_h.append(kap_p[:, h])
            r_h.append(r_p[:, h])
            k_t.append((k_p * e_nc)[:, h].astype(BF16))
            b_t.append((b_p * e_nc)[:, h].astype(BF16))
            k_e.append((k_p * e_tc)[:, h].astype(BF16))
            b_e.append((b_p * e_tc)[:, h].astype(BF16))
            e_tot.append(e_p[:, h])
            v.append(v_p[:, h].astype(BF16))

    def bd(a, b):
        return _dot(a.astype(BF16), b.astype(BF16))

    x = [jnp.concatenate([kap_h[i], r_h[i]], axis=0).astype(BF16) for i in heads]
    g1 = [_dot_nt(x[i], k_t[i]) for i in heads]
    g2 = [_dot_nt(x[i], b_t[i]) for i in heads]
    a_kk = [jnp.where(earlier, g1[i][:C], 0.0).astype(BF16) for i in heads]
    a_rk = [jnp.where(incl, g1[i][C:], 0.0).astype(BF16) for i in heads]
    a_kb = [jnp.where(earlier, g2[i][:C], 0.0) for i in heads]
    a_rb = [jnp.where(incl, g2[i][C:], 0.0).astype(BF16) for i in heads]
    blk8 = same_block(8)
    a0 = [jnp.where(blk8, a_kb[i], 0.0) for i in heads]
    a2 = [bd(a0[i], a0[i]) for i in heads]
    a4 = [bd(a2[i], a2[i]) for i in heads]
    t = [bd(eye - a0[i], eye + a2[i]) for i in heads]
    t = [bd(t[i], eye + a4[i]) for i in heads]
    for n in (16, 32, 64):
        m = jnp.logical_and(same_block(n), jnp.logical_not(same_block(n // 2)))
        off = [jnp.where(m, a_kb[i], 0.0) for i in heads]
        ot = [bd(off[i], t[i]) for i in heads]
        t = [t[i] - bd(t[i], ot[i]) for i in heads]
    tb = [t[i].astype(BF16) for i in heads]
    akv = [_dot(a_kk[i], v[i]) for i in heads]
    pm = [_dot(tb[i], kap_h[i].astype(BF16)) for i in heads]
    qm = [bd(tb[i], akv[i]) for i in heads]
    pmb = [pm[i].astype(BF16) for i in heads]
    qmb = [qm[i].astype(BF16) for i in heads]
    y0 = [_dot(a_rk[i], v[i]) - _dot(a_rb[i], qmb[i]) for i in heads]
    rp = [r_h[i] - _dot(a_rb[i], pmb[i]) for i in heads]
    mt = [eye * e_tot[i] - _dot_tn(b_e[i], pmb[i]) for i in heads]
    n0 = [_dot_tn(k_e[i], v[i]) - _dot_tn(b_e[i], qmb[i]) for i in heads]
    st = [st_ref[i] for i in heads]
    ys = [_dot(rp[i], st[i], HIGHEST) + y0[i] for i in heads]
    for i in heads:
        st_ref[i] = _dot(mt[i], st[i], HIGHEST) + n0[i]
    for p in range(npairs):
        y_ref[:, p * LANES:(p + 1) * LANES] = jnp.concatenate([ys[2 * p], ys[2 * p + 1]], axis=1)


def _delta_scan(lw, k, b, kap, v, r, *, n_ctx, reverse, pairs_per_block=4):
    R, W = lw.shape
    C = SCAN_CHUNK
    nchunks = R // C
    ctx_chunks = n_ctx // C
    npairs = W // LANES
    pb = pairs_per_block
    assert R % C == 0 and n_ctx % C == 0 and npairs % pb == 0

    if reverse:
        def rows(s):
            return jnp.where(s < ctx_chunks, ctx_chunks - 1 - s, nchunks - 1 - (s - ctx_chunks))
    else:
        def rows(s):
            return s

    spec = pl.BlockSpec((C, pb * LANES), lambda g, s: (rows(s), g))
    return pl.pallas_call(
        functools.partial(_scan_kernel, reverse=reverse, npairs=pb),
        grid=(npairs // pb, nchunks),
        in_specs=[spec] * 6,
        out_specs=spec,
        out_shape=jax.ShapeDtypeStruct((R, W), F32),
        scratch_shapes=[pltpu.VMEM((2 * pb, A_HEAD_DIM, A_HEAD_DIM), F32)],
        compiler_params=_params("parallel", "arbitrary"),
        name="delta_scan_rev" if reverse else "delta_scan_fwd",
    )(lw, k, b, kap, v, r)


def _rwkv_readout_kernel(yf_ref, yb_ref, r_ref, v_ref, k0_ref, k1_ref, gs_ref, gup_ref, rk_ref, lng_ref, lnb_ref,
                         o_ref):
    ones_bd = _head_block_ones()
    inv_n = 1.0 / A_HEAD_DIM
    ro = yf_ref[...] + yb_ref[...]
    mu = _head_sums(ro, ones_bd) * inv_n
    cen = ro - mu
    var = _head_sums(cen * cen, ones_bd) * inv_n
    yn = cen * lax.rsqrt(var + A_GN_EPS) * lng_ref[...] + lnb_ref[...]
    bonus = _head_sums(r_ref[...] * (k0_ref[...] + k1_ref[...]) * rk_ref[...], ones_bd) * v_ref[...]
    g = _dot(gs_ref[...], gup_ref[...])
    o_ref[...] = ((yn + bonus) * g).astype(o_ref.dtype)


def _rwkv_readout(yf, yb, r, v, k0, k1, gs, gup, r_k, ln_g, ln_b):
    R, W = yf.shape
    tm = _tile(R, 128, 16)
    wide = pl.BlockSpec((tm, W), lambda i: (i, 0))
    vec = pl.BlockSpec((1, W), lambda i: (0, 0))
    return pl.pallas_call(
        _rwkv_readout_kernel,
        grid=(R // tm,),
        in_specs=[wide] * 6 + [pl.BlockSpec((tm, A_GD_PAD), lambda i: (i, 0)),
                               pl.BlockSpec((A_GD_PAD, W), lambda i: (0, 0)), vec, vec, vec],
        out_specs=wide,
        out_shape=jax.ShapeDtypeStruct((R, W), BF16),
        compiler_params=_params("parallel"),
        name="rwkv_readout",
    )(yf, yb, r, v, k0, k1, gs, gup, r_k, ln_g, ln_b)


def _pool_kernel(u_ref, prev_ref, next_ref, w_ref, scale_ref, o_ref, ext_ref, *, ctx_tiles, n_tiles, n_ctx, n_lat):
    i = pl.program_id(0)
    first, last = _segment_flags(i, ctx_tiles, n_tiles)
    tm = u_ref.shape[0]
    H = POOL_HALO
    ext_ref[0:H, :] = jnp.where(first, 0.0, prev_ref[...])
    ext_ref[H:H + tm, :] = u_ref[...]
    ext_ref[H + tm:H + tm + H, :] = jnp.where(last, 0.0, next_ref[...])
    is_ctx = i < ctx_tiles
    seg_len = jnp.where(is_ctx, n_ctx, n_lat)
    t = lax.broadcasted_iota(jnp.int32, (tm, 1), 0) + i * tm - jnp.where(is_ctx, 0, n_ctx)
    for gi, win in enumerate(POOL_WINDOWS):
        cols = slice(gi * POOL_GROUP_W, (gi + 1) * POOL_GROUP_W)
        acc = None
        for o in range(-(win // 2), win // 2):
            term = ext_ref[H + o:H + o + tm, cols]
            acc = term if acc is None else acc + term
        lo = jnp.maximum(t - win // 2, 0)
        hi = jnp.minimum(t + win // 2 - 1, seg_len - 1)
        cnt = (hi - lo + 1).astype(F32)
        pooled = acc / cnt - u_ref[:, cols]
        y = _dot(pooled.astype(BF16), w_ref[gi]) * scale_ref[:, cols]
        o_ref[:, cols] = y.astype(o_ref.dtype)


def _pool(p, pool_w, pool_scale, *, n_ctx):
    R = p.shape[0]
    tm = _tile(math.gcd(n_ctx, R - n_ctx), 256, 16)
    n_tiles = R // tm
    hb = tm // POOL_HALO
    n_hblocks = R // POOL_HALO
    cb = P_B // B_WIDTH
    assert P_B % B_WIDTH == 0
    return pl.pallas_call(
        functools.partial(_pool_kernel, ctx_tiles=n_ctx // tm, n_tiles=n_tiles, n_ctx=n_ctx, n_lat=R - n_ctx),
        grid=(n_tiles,),
        in_specs=[pl.BlockSpec((tm, B_WIDTH), lambda i: (i, cb)),
                  pl.BlockSpec((POOL_HALO, B_WIDTH), lambda i: (jnp.maximum(i * hb - 1, 0), cb)),
                  pl.BlockSpec((POOL_HALO, B_WIDTH), lambda i: (jnp.minimum((i + 1) * hb, n_hblocks - 1), cb)),
                  pl.BlockSpec((len(POOL_WINDOWS), POOL_GROUP_W, POOL_GROUP_W), lambda i: (0, 0, 0)),
                  pl.BlockSpec((1, B_WIDTH), lambda i: (0, 0))],
        out_specs=pl.BlockSpec((tm, B_WIDTH), lambda i: (i, 0)),
        out_shape=jax.ShapeDtypeStruct((R, B_WIDTH), BF16),
        scratch_shapes=[pltpu.VMEM((tm + 2 * POOL_HALO, B_WIDTH), F32)],
        compiler_params=_params("parallel"),
        name="pool",
    )(p, p, p, pool_w, pool_scale)


ROPE_BLOCK = 512
QKV_WIDTH = C_WIDTH + 2 * C_KV_WIDTH
N_QK_BLOCKS = (C_WIDTH + C_KV_WIDTH) // ROPE_BLOCK


def _rope_kernel(u_ref, cos_ref, sin_ref, o_ref):
    j = pl.program_id(1)
    u = u_ref[...]

    @pl.when(j < N_QK_BLOCKS)
    def _():
        cos = cos_ref[...]
        sin = sin_ref[...]
        lane = lax.broadcasted_iota(jnp.int32, (1, C_HEAD_DIM), 1)
        low = (lane % (C_HEAD_DIM // 2)) < (C_HEAD_DIM // 4)
        for h in range(ROPE_BLOCK // C_HEAD_DIM):
            cols = slice(h * C_HEAD_DIM, (h + 1) * C_HEAD_DIM)
            x = u[:, cols]
            partner = jnp.where(low, pltpu.roll(x, 3 * C_HEAD_DIM // 4, axis=1), pltpu.roll(x, C_HEAD_DIM // 4, axis=1))
            o_ref[:, cols] = (x * cos + partner * sin).astype(o_ref.dtype)

    @pl.when(j >= N_QK_BLOCKS)
    def _():
        o_ref[...] = u.astype(o_ref.dtype)


def _rope_pack(p, cos_tab, sin_tab):
    R = p.shape[0]
    tm = _tile(R, 256, 16)
    nq = C_WIDTH // ROPE_BLOCK
    assert P_Q % ROPE_BLOCK == 0 and P_K % ROPE_BLOCK == 0 and P_V == P_K + ROPE_BLOCK

    def src(i, j):
        return i, jnp.where(j < nq, P_Q // ROPE_BLOCK + j, P_K // ROPE_BLOCK + j - nq)

    tab = pl.BlockSpec((tm, C_HEAD_DIM), lambda i, j: (i, 0))
    return pl.pallas_call(
        _rope_kernel,
        grid=(R // tm, QKV_WIDTH // ROPE_BLOCK),
        in_specs=[pl.BlockSpec((tm, ROPE_BLOCK), src), tab, tab],
        out_specs=pl.BlockSpec((tm, ROPE_BLOCK), lambda i, j: (i, j)),
        out_shape=jax.ShapeDtypeStruct((R, QKV_WIDTH), BF16),
        compiler_params=_params("parallel", "arbitrary"),
        name="rope_pack",
    )(p, cos_tab, sin_tab)


def _rope_tables(n_ctx, n_lat):
    half = C_HEAD_DIM // 2
    t = jnp.arange(n_lat)
    row = (t // GRID_W).astype(F32)
    col = (t % GRID_W).astype(F32)
    inv = ROPE_BASE ** (-jnp.arange(0, half, 2, dtype=F32) / half)
    ar = row[:, None] * inv[None]
    ac = col[:, None] * inv[None]
    cos = jnp.concatenate([jnp.cos(ar), jnp.cos(ar), jnp.cos(ac), jnp.cos(ac)], axis=1)
    sin = jnp.concatenate([-jnp.sin(ar), jnp.sin(ar), -jnp.sin(ac), jnp.sin(ac)], axis=1)
    cos = jnp.concatenate([jnp.ones((n_ctx, C_HEAD_DIM), F32), cos], axis=0)
    sin = jnp.concatenate([jnp.zeros((n_ctx, C_HEAD_DIM), F32), sin], axis=0)
    return cos, sin


def _attn_kernel(q_ref, kp_ref, kc_ref, kn_ref, kx_ref, vp_ref, vc_ref, vn_ref, vx_ref, sink_ref, o_ref,
                 *, ctx_blocks, n_blocks):
    i = pl.program_id(0)
    B = ATTN_BLOCK
    keys = jnp.concatenate([kp_ref[...], kc_ref[...], kn_ref[...], kx_ref[...]], axis=0)
    vals = jnp.concatenate([vp_ref[...], vc_ref[...], vn_ref[...], vx_ref[...]], axis=0)
    nk = keys.shape[0]
    qrow = lax.broadcasted_iota(jnp.int32, (B, nk), 0)
    kcol = lax.broadcasted_iota(jnp.int32, (B, nk), 1)
    rel = kcol - B - qrow
    local_ok = jnp.logical_and(jnp.abs(rel) <= ATTN_BLOCK, i >= ctx_blocks)
    local_ok = jnp.logical_and(local_ok, jnp.logical_or(kcol >= B, i > ctx_blocks))
    local_ok = jnp.logical_and(local_ok, jnp.logical_or(kcol < 2 * B, i < n_blocks - 1))
    valid = jnp.logical_or(kcol >= 3 * B, local_ok)
    scale = C_HEAD_DIM ** -0.5
    sinks = sink_ref[0]
    for g in range(C_GROUP):
        cols = slice(g * C_HEAD_DIM, (g + 1) * C_HEAD_DIM)
        s = _dot_nt(q_ref[:, cols], keys) * scale
        s = jnp.where(valid, s, NEG_INF)
        sk = sinks[g:g + 1, 0:1]
        m = jnp.maximum(jnp.max(s, axis=-1, keepdims=True), sk)
        e = jnp.exp(s - m)
        denom = jnp.sum(e, axis=-1, keepdims=True) + jnp.exp(sk - m)
        pr = (e / denom).astype(BF16)
        o_ref[:, cols] = _dot(pr, vals).astype(o_ref.dtype)


def _attention(qkv, sink8, *, n_ctx):
    R = qkv.shape[0]
    B = ATTN_BLOCK
    n_blocks = R // B
    ctx_blocks = n_ctx // B
    assert n_ctx % B == 0 and R % B == 0
    qw = C_GROUP * C_HEAD_DIM
    k0 = C_WIDTH // C_HEAD_DIM
    v0 = (C_WIDTH + C_KV_WIDTH) // C_HEAD_DIM

    def nb(i, d):
        return jnp.clip(i + d, ctx_blocks, n_blocks - 1)

    def kv(c0, d):
        return pl.BlockSpec((B, C_HEAD_DIM), lambda i, h: (nb(i, d), c0 + h))

    def kv_ctx(c0):
        return pl.BlockSpec((n_ctx, C_HEAD_DIM), lambda i, h: (0, c0 + h))

    return pl.pallas_call(
        functools.partial(_attn_kernel, ctx_blocks=ctx_blocks, n_blocks=n_blocks),
        grid=(n_blocks, C_KV_HEADS),
        in_specs=[pl.BlockSpec((B, qw), lambda i, h: (i, h)),
                  kv(k0, -1), kv(k0, 0), kv(k0, 1), kv_ctx(k0),
                  kv(v0, -1), kv(v0, 0), kv(v0, 1), kv_ctx(v0),
                  pl.BlockSpec((1, SUBLANES, LANES), lambda i, h: (h, 0, 0))],
        out_specs=pl.BlockSpec((B, qw), lambda i, h: (i, h)),
        out_shape=jax.ShapeDtypeStruct((R, C_WIDTH), BF16),
        compiler_params=_params("parallel", "arbitrary"),
        name="window_attention",
    )(qkv, qkv, qkv, qkv, qkv, qkv, qkv, qkv, qkv, sink8)


def _pad_to(a, axis, size):
    pad = [(0, 0)] * a.ndim
    pad[axis] = (0, size - a.shape[axis])
    return jnp.pad(a, pad)


def _relayout_w_in(w_in):
    seg = lambda lo, hi: w_in[..., lo:hi]
    return jnp.concatenate([
        _pad_to(seg(0, OFF_B), -1, A_PAD), seg(OFF_K, OFF_V), seg(OFF_V, OFF_G), seg(OFF_B, OFF_Q),
        seg(OFF_Q, OFF_K), seg(OFF_G, IN_COLS)], axis=-1).astype(BF16)


def _low_rank_pair(up):
    z = jnp.zeros_like(up[:, 0])
    return jnp.stack([jnp.concatenate([up[:, 0], z], axis=1), jnp.concatenate([z, up[:, 1]], axis=1)], axis=1).astype(BF16)


def kernel(x, c, ctx, c_ctx, mod_down, mod_up, mod_b, norm_g, w_in, shift_mu, rwkv_w0, rwkv_w_up, rwkv_a0,
           rwkv_a_up, rwkv_g_up, rwkv_k_k, rwkv_k_a, rwkv_r_k, rwkv_ln_g, rwkv_ln_b, pool_w, pool_scale,
           attn_sink, gate_up, w_branch_a, w_branch_b, w_branch_c, w_out, ffn_w1, ffn_w3, ffn_w2):
    assert x.shape[0] == 1 and ctx.shape[0] == 1 and c.shape[0] == 1
    depth = w_in.shape[0]
    T, D = x.shape[1], x.shape[2]
    L = ctx.shape[1]
    d_ff = ffn_w1.shape[2]
    d_ff_pad = -(-d_ff // 512) * 512
    assert 2 * A_DECAY_RANK == LANES and 2 * A_ICLR_RANK == LANES

    w_in_p = _relayout_w_in(w_in)
    mu_p = _pad_to(shift_mu, -1, A_PAD)[:, None, :]
    wup2 = _low_rank_pair(rwkv_w_up)
    aup2 = _low_rank_pair(rwkv_a_up)
    gup_p = _pad_to(rwkv_g_up, 1, A_GD_PAD).astype(BF16)
    pool_w_b = pool_w.astype(BF16)
    gate_up_b = gate_up.astype(BF16)
    wa_b, wb_b, wc_b = (w.astype(BF16) for w in (w_branch_a, w_branch_b, w_branch_c))
    w_out_b = w_out.astype(BF16)
    w1_b = _pad_to(ffn_w1, 2, d_ff_pad).astype(BF16)
    w3_b = _pad_to(ffn_w3, 2, d_ff_pad).astype(BF16)
    w2_b = _pad_to(ffn_w2, 1, d_ff_pad).astype(BF16)
    sink8 = jnp.broadcast_to(
        _pad_to(attn_sink.reshape(depth, C_KV_HEADS, C_GROUP), 2, SUBLANES)[..., None],
        (depth, C_KV_HEADS, SUBLANES, LANES))
    cos_tab, sin_tab = _rope_tables(L, T)

    c8 = _pad_to(jnp.concatenate([c_ctx[None], c], axis=0), 0, SUBLANES)
    mod = _modulation(c8, mod_down.astype(BF16), mod_up.astype(BF16), mod_b).reshape(depth, SUBLANES, 6, D)

    def mod6(l, shift_i, scale_i, gate_i):
        m = mod[l]
        return jnp.stack([m[0, gate_i], m[1, gate_i], m[0, scale_i], m[1, scale_i], m[0, shift_i], m[1, shift_i]])

    xs = jnp.concatenate([ctx[0], x[0]], axis=0)
    _, h = _resid_norm(xs, None, jnp.stack([norm_g[0, 0], norm_g[0, 0]]), mod6(0, 0, 1, 2), n_ctx=L, emit_h=True)
    for l in range(depth):
        p = _matmul(h, w_in_p[l], tn_cap=768, name="w_in")
        r, v, kap, lw0, lw1, k0, k1, b0, b1, gs = _rwkv_prep(
            p, mu_p[l], rwkv_w0[l], wup2[l], rwkv_a0[l], aup2[l], rwkv_k_k[l][None], rwkv_k_a[l][None], n_ctx=L)
        yf = _delta_scan(lw0, k0, b0, kap, v, r, n_ctx=L, reverse=False)
        yr = _delta_scan(lw1, k1, b1, kap, v, r, n_ctx=L, reverse=True)
        y_a = _rwkv_readout(yf, yr, r, v, k0, k1, gs, gup_p[l], rwkv_r_k[l][None], rwkv_ln_g[l][None],
                            rwkv_ln_b[l][None])
        y_b = _pool(p, pool_w_b[l], pool_scale[l][None], n_ctx=L)
        y_c = _attention(_rope_pack(p, cos_tab, sin_tab), sink8[l], n_ctx=L)
        acc = _merge(p, y_a, y_b, y_c, gate_up_b[l], wa_b[l], wb_b[l], wc_b[l])
        mix = _matmul(acc, w_out_b[l], name="w_out")
        xs, h2 = _resid_norm(xs, mix, norm_g[l, 1:3], mod6(l, 3, 4, 2), n_ctx=L, emit_h=True)
        f = _matmul(_ffn_up(h2, w1_b[l], w3_b[l]), w2_b[l], tk_cap=2816, name="ffn_down")
        if l + 1 < depth:
            g2 = jnp.stack([norm_g[l, 3], norm_g[l + 1, 0]])
            m6 = jnp.concatenate([mod6(l, 0, 1, 5)[:2], mod6(l + 1, 0, 1, 2)[2:]], axis=0)
            xs, h = _resid_norm(xs, f, g2, m6, n_ctx=L, emit_h=True)
        else:
            xs, _ = _resid_norm(xs, f, jnp.stack([norm_g[l, 3], norm_g[l, 3]]), mod6(l, 0, 1, 5), n_ctx=L,
                                emit_h=False)
    return xs[L:][None]
```

```python
import functools
import math

import jax
import jax.numpy as jnp
from jax import lax
from jax.experimental import pallas as pl
from jax.experimental.pallas import tpu as pltpu

F32 = jnp.float32
BF16 = jnp.bfloat16
HIGHEST = lax.Precision.HIGHEST

LANES = 128
SUBLANES = 8
VMEM_LIMIT = 56 * 1024 * 1024

NORM_EPS = 1e-6
GRID_W = 64
ROPE_BASE = 10000.0
NEG_INF = -1e30

A_HEADS = 24
A_HEAD_DIM = 64
A_WIDTH = A_HEADS * A_HEAD_DIM
A_DECAY_RANK = 64
A_ICLR_RANK = 64
A_GATE_RANK = 224
A_GN_EPS = 64e-5
A_COLS = 3 * A_WIDTH + 2 * A_DECAY_RANK + 2 * A_ICLR_RANK + A_GATE_RANK
SCAN_CHUNK = 64
POOL_WINDOWS = (2, 4, 8, 16)
POOL_GROUP_W = 384
B_WIDTH = len(POOL_WINDOWS) * POOL_GROUP_W
POOL_HALO = 8
C_Q_HEADS = 16
C_KV_HEADS = 4
C_GROUP = C_Q_HEADS // C_KV_HEADS
C_HEAD_DIM = 128
C_WIDTH = C_Q_HEADS * C_HEAD_DIM
C_KV_WIDTH = C_KV_HEADS * C_HEAD_DIM
ATTN_BLOCK = 128
GATE_RANK = 256
N_BRANCH = 3

OFF_B = A_COLS
OFF_Q = OFF_B + B_WIDTH
OFF_K = OFF_Q + C_WIDTH
OFF_V = OFF_K + C_KV_WIDTH
OFF_G = OFF_V + C_KV_WIDTH
IN_COLS = OFF_G + GATE_RANK
A_PAD = 5120
A_GD_OFF = 3 * A_WIDTH + 2 * A_DECAY_RANK + 2 * A_ICLR_RANK
A_GD_PAD = A_PAD - A_GD_OFF
P_K = A_PAD
P_V = P_K + C_KV_WIDTH
P_B = P_V + C_KV_WIDTH
P_Q = P_B + B_WIDTH
P_G = P_Q + C_WIDTH
P_COLS = P_G + GATE_RANK


def _dot(a, b, precision=None):
    return jnp.dot(a, b, precision=precision, preferred_element_type=F32)


def _dot_nt(a, b):
    return lax.dot_general(a, b, (((1,), (1,)), ((), ())), preferred_element_type=F32)


def _dot_tn(a, b):
    return lax.dot_general(a, b, (((0,), (0,)), ((), ())), preferred_element_type=F32)


def _tile(n, cap, mult):
    best = None
    for t in range(mult, min(n, cap) + 1, mult):
        if n % t == 0:
            best = t
    assert best is not None, (n, cap, mult)
    return best


def _params(*sem):
    return pltpu.CompilerParams(dimension_semantics=sem, vmem_limit_bytes=VMEM_LIMIT)


def _sigmoid(x):
    return 1.0 / (1.0 + jnp.exp(-x))


def _silu(x):
    return x * _sigmoid(x)


def _head_block_ones():
    r = lax.broadcasted_iota(jnp.int32, (LANES, LANES), 0)
    c = lax.broadcasted_iota(jnp.int32, (LANES, LANES), 1)
    return jnp.where((r // A_HEAD_DIM) == (c // A_HEAD_DIM), 1.0, 0.0).astype(F32)


def _head_sums(x, ones_bd):
    return jnp.concatenate(
        [_dot(x[:, j:j + LANES], ones_bd, HIGHEST) for j in range(0, x.shape[1], LANES)], axis=1)


def _mod_kernel(c_ref, down_ref, up_ref, b_ref, o_ref):
    s = _silu(c_ref[...]).astype(BF16)
    low = _dot(s, down_ref[0]).astype(BF16)
    o_ref[0] = _dot(low, up_ref[0]) + b_ref[0]


def _modulation(c8, down, up, bias):
    depth, d, rank = down.shape
    n = up.shape[2]
    return pl.pallas_call(
        _mod_kernel,
        grid=(depth,),
        in_specs=[pl.BlockSpec((SUBLANES, d), lambda l: (0, 0)),
                  pl.BlockSpec((1, d, rank), lambda l: (l, 0, 0)),
                  pl.BlockSpec((1, rank, n), lambda l: (l, 0, 0)),
                  pl.BlockSpec((1, 1, n), lambda l: (l, 0, 0))],
        out_specs=pl.BlockSpec((1, SUBLANES, n), lambda l: (l, 0, 0)),
        out_shape=jax.ShapeDtypeStruct((depth, SUBLANES, n), F32),
        compiler_params=_params("arbitrary"),
        name="modulation",
    )(c8, down, up, bias.reshape(depth, 1, n))


def _rms(x, g):
    return x * lax.rsqrt(jnp.mean(x * x, axis=-1, keepdims=True) + NORM_EPS) * g


def _resid_norm_kernel(*refs, ctx_tiles, has_m, emit_h):
    refs = list(refs)
    x_ref = refs.pop(0)
    m_ref = refs.pop(0) if has_m else None
    g_ref = refs.pop(0)
    mod_ref = refs.pop(0)
    is_ctx = pl.program_id(0) < ctx_tiles

    def pick(i):
        return jnp.where(is_ctx, mod_ref[i:i + 1, :], mod_ref[i + 1:i + 2, :])

    x = x_ref[...]
    if has_m:
        x = x + pick(0) * _rms(m_ref[...], g_ref[0:1, :])
        refs.pop(0)[...] = x
    if emit_h:
        h = _rms(x, g_ref[1:2, :]) * (1.0 + pick(2)) + pick(4)
        refs.pop(0)[...] = h.astype(BF16)


def _resid_norm(x, m, g2, mod6, *, n_ctx, emit_h):
    R, D = x.shape
    te = _tile(math.gcd(n_ctx, R - n_ctx), 256, 16)
    has_m = m is not None
    row = pl.BlockSpec((te, D), lambda i: (i, 0))
    ins = [x] + ([m] if has_m else []) + [g2, mod6]
    in_specs = [row] * (2 if has_m else 1) + [pl.BlockSpec((2, D), lambda i: (0, 0)),
                                              pl.BlockSpec((6, D), lambda i: (0, 0))]
    out_shape, out_specs = [], []
    if has_m:
        out_shape.append(jax.ShapeDtypeStruct((R, D), F32))
        out_specs.append(row)
    if emit_h:
        out_shape.append(jax.ShapeDtypeStruct((R, D), BF16))
        out_specs.append(row)
    outs = pl.pallas_call(
        functools.partial(_resid_norm_kernel, ctx_tiles=n_ctx // te, has_m=has_m, emit_h=emit_h),
        grid=(R // te,),
        in_specs=in_specs, out_specs=out_specs, out_shape=out_shape,
        compiler_params=_params("parallel"),
        name="resid_norm",
    )(*ins)
    outs = list(outs)
    x1 = outs.pop(0) if has_m else None
    h = outs.pop(0) if emit_h else None
    return x1, h


def _mm_kernel(x_ref, w_ref, o_ref, *scratch, nk):
    part = _dot(x_ref[...].astype(BF16), w_ref[...])
    if nk == 1:
        o_ref[...] = part.astype(o_ref.dtype)
        return
    acc_ref, = scratch
    k = pl.program_id(2)

    @pl.when(k == 0)
    def _():
        acc_ref[...] = part

    @pl.when(k > 0)
    def _():
        acc_ref[...] += part

    @pl.when(k == nk - 1)
    def _():
        o_ref[...] = acc_ref[...].astype(o_ref.dtype)


def _matmul(x, w, layer, *, out_dtype=F32, tm_cap=1280, tn_cap=512, tk_cap=4096, name="matmul"):
    M, K = x.shape
    N = w.shape[2]
    tm = _tile(M, tm_cap, 16)
    tn = _tile(N, tn_cap, LANES)
    tk = _tile(K, tk_cap, LANES)
    nk = K // tk
    return pl.pallas_call(
        functools.partial(_mm_kernel, nk=nk),
        grid=(M // tm, N // tn, nk),
        in_specs=[pl.BlockSpec((tm, tk), lambda i, j, k: (i, k)),
                  pl.BlockSpec((None, tk, tn), lambda i, j, k: (layer, k, j))],
        out_specs=pl.BlockSpec((tm, tn), lambda i, j, k: (i, j)),
        out_shape=jax.ShapeDtypeStruct((M, N), out_dtype),
        scratch_shapes=[pltpu.VMEM((tm, tn), F32)] if nk > 1 else [],
        compiler_params=_params("parallel", "parallel", "arbitrary"),
        name=name,
    )(x, w)


def _ffn_up_kernel(h_ref, w1_ref, w3_ref, o_ref):
    h = h_ref[...]
    o_ref[...] = (_silu(_dot(h, w1_ref[...])) * _dot(h, w3_ref[...])).astype(o_ref.dtype)


def _ffn_up(h, w1, w3, layer):
    M, K = h.shape
    N = w1.shape[2]
    tm = _tile(M, 1280, 16)
    tn = min(512, N)
    assert N % LANES == 0
    wspec = pl.BlockSpec((None, K, tn), lambda i, j: (layer, 0, j))
    return pl.pallas_call(
        _ffn_up_kernel,
        grid=(M // tm, pl.cdiv(N, tn)),
        in_specs=[pl.BlockSpec((tm, K), lambda i, j: (i, 0)), wspec, wspec],
        out_specs=pl.BlockSpec((tm, tn), lambda i, j: (i, j)),
        out_shape=jax.ShapeDtypeStruct((M, N), BF16),
        compiler_params=_params("parallel", "parallel"),
        name="ffn_up",
    )(h, w1, w3)


def _merge_kernel(pg_ref, ya_ref, yb_ref, yc_ref, ga_ref, gb_ref, gc_ref, wa_ref, wb_ref, wc_ref, o_ref):
    pg = pg_ref[...].astype(BF16)
    acc = _sigmoid(_dot(pg, ga_ref[...])) * _dot(ya_ref[...], wa_ref[...])
    acc += _sigmoid(_dot(pg, gb_ref[...])) * _dot(yb_ref[...], wb_ref[...])
    acc += _sigmoid(_dot(pg, gc_ref[...])) * _dot(yc_ref[...], wc_ref[...])
    o_ref[...] = acc.astype(o_ref.dtype)


def _merge(p, ya, yb, yc, gate_up, wa, wb, wc, layer):
    R = p.shape[0]
    D = wa.shape[2]
    tm = _tile(R, 1280, 16)
    tn = _tile(D, 512, LANES)
    nj = D // tn

    def rows(width):
        return pl.BlockSpec((tm, width), lambda i, j: (i, 0))

    def gate(branch):
        return pl.BlockSpec((None, GATE_RANK, tn), lambda i, j: (layer, 0, branch * nj + j))

    def wcol(width):
        return pl.BlockSpec((None, width, tn), lambda i, j: (layer, 0, j))

    return pl.pallas_call(
        _merge_kernel,
        grid=(R // tm, nj),
        in_specs=[pl.BlockSpec((tm, GATE_RANK), lambda i, j: (i, P_G // GATE_RANK)),
                  rows(A_WIDTH), rows(B_WIDTH), rows(C_WIDTH),
                  gate(0), gate(1), gate(2), wcol(A_WIDTH), wcol(B_WIDTH), wcol(C_WIDTH)],
        out_specs=pl.BlockSpec((tm, tn), lambda i, j: (i, j)),
        out_shape=jax.ShapeDtypeStruct((R, D), BF16),
        compiler_params=_params("parallel", "parallel"),
        name="merge",
    )(p, ya, yb, yc, gate_up, gate_up, gate_up, wa, wb, wc)


def _segment_flags(i, ctx_tiles, n_tiles):
    first = jnp.logical_or(i == 0, i == ctx_tiles)
    last = jnp.logical_or(i == ctx_tiles - 1, i == n_tiles - 1)
    return first, last


def _rwkv_prep_kernel(u_ref, prev_ref, next_ref, mu_ref, w0_ref, wup_ref, a0_ref, aup_ref, kk_ref, ka_ref,
                      r_out, v_out, kap_out, lw0_out, lw1_out, k0_out, k1_out, b0_out, b1_out, gs_out,
                      *, ctx_tiles, n_tiles):
    i = pl.program_id(0)
    first, last = _segment_flags(i, ctx_tiles, n_tiles)
    u = u_ref[...]
    tm = u.shape[0]
    rid = lax.broadcasted_iota(jnp.int32, (tm, 1), 0)
    prev_row = jnp.where(first, 0.0, prev_ref[SUBLANES - 1:SUBLANES, :])
    next_row = jnp.where(last, 0.0, next_ref[0:1, :])
    prev = jnp.where(rid == 0, prev_row, pltpu.roll(u, 1, axis=0))
    nxt = jnp.where(rid == tm - 1, next_row, pltpu.roll(u, tm - 1, axis=0))
    s = u + mu_ref[...] * (0.5 * (prev + nxt) - u)

    W = A_WIDTH
    r = s[:, 0:W]
    k = s[:, W:2 * W]
    v = s[:, 2 * W:3 * W]
    o = 3 * W
    wd = jnp.tanh(s[:, o:o + LANES]).astype(BF16)
    ad = s[:, o + LANES:o + 2 * LANES].astype(BF16)
    gd = s[:, A_GD_OFF:A_PAD]

    ones_bd = _head_block_ones()
    kk = k * kk_ref[...]
    nrm = jnp.sqrt(_head_sums(kk * kk, ones_bd))
    kk = kk / jnp.maximum(nrm, 1e-12)

    r_out[...] = r
    v_out[...] = v
    kap_out[...] = kk
    gs_out[...] = _sigmoid(gd).astype(BF16)
    for d, (lw_out, k_out, b_out) in enumerate(((lw0_out, k0_out, b0_out), (lw1_out, k1_out, b1_out))):
        z = w0_ref[d:d + 1, :] + _dot(wd, wup_ref[d])
        w_log = -(jnp.maximum(-z, 0.0) + jnp.log(1.0 + jnp.exp(-jnp.abs(z)))) - 0.5
        lw_out[...] = -jnp.exp(w_log)
        a = _sigmoid(a0_ref[d:d + 1, :] + _dot(ad, aup_ref[d]))
        k_out[...] = k * (1.0 + (a - 1.0) * ka_ref[...])
        b_out[...] = kk * a


def _rwkv_prep(p, mu, w0, wup2, a0, aup2, k_k, k_a, *, n_ctx):
    R = p.shape[0]
    tm = _tile(math.gcd(n_ctx, R - n_ctx), 128, 16)
    n_tiles = R // tm
    hb = tm // SUBLANES
    n_hblocks = R // SUBLANES
    W = A_WIDTH

    def const(shape):
        return pl.BlockSpec(shape, lambda i: (0,) * len(shape))

    wide = pl.BlockSpec((tm, W), lambda i: (i, 0))
    f32w = jax.ShapeDtypeStruct((R, W), F32)
    return pl.pallas_call(
        functools.partial(_rwkv_prep_kernel, ctx_tiles=n_ctx // tm, n_tiles=n_tiles),
        grid=(n_tiles,),
        in_specs=[pl.BlockSpec((tm, A_PAD), lambda i: (i, 0)),
                  pl.BlockSpec((SUBLANES, A_PAD), lambda i: (jnp.maximum(i * hb - 1, 0), 0)),
                  pl.BlockSpec((SUBLANES, A_PAD), lambda i: (jnp.minimum((i + 1) * hb, n_hblocks - 1), 0)),
                  const((1, A_PAD)), const((2, W)), const((2, LANES, W)), const((2, W)), const((2, LANES, W)),
                  const((1, W)), const((1, W))],
        out_specs=[wide] * 9 + [pl.BlockSpec((tm, A_GD_PAD), lambda i: (i, 0))],
        out_shape=[f32w] * 9 + [jax.ShapeDtypeStruct((R, A_GD_PAD), BF16)],
        compiler_params=_params("parallel"),
        name="rwkv_prep",
    )(p, p, p, mu, w0, wup2, a0, aup2, k_k, k_a)


def _scan_kernel(lw_ref, k_ref, b_ref, kap_ref, v_ref, r_ref, y_ref, s_ref, *, reverse, npairs):
    C = SCAN_CHUNK
    N = A_HEAD_DIM

    @pl.when(pl.program_id(1) == 0)
    def _():
        s_ref[...] = jnp.zeros_like(s_ref)

    row = lax.broadcasted_iota(jnp.int32, (C, C), 0)
    col = lax.broadcasted_iota(jnp.int32, (C, C), 1)
    earlier = (col > row) if reverse else (col < row)
    diag = col == row
    incl = jnp.logical_or(earlier, diag)
    tri = jnp.where(incl, 1.0, 0.0).astype(BF16)
    tri3 = jnp.concatenate([tri, tri, tri], axis=1)
    eye = jnp.where(diag, 1.0, 0.0).astype(F32)

    def same_block(n):
        return (row // n) == (col // n)

    last = 0 if reverse else C - 1

    hs = (slice(0, N), slice(N, 2 * N))
    heads = range(2 * npairs)
    x, k_t, b_t, k_e, b_e, e_tot, v = ([] for _ in range(7))
    for p in range(npairs):
        sl = slice(p * LANES, (p + 1) * LANES)
        lw = lw_ref[:, sl]
        lw_hi = lw.astype(BF16)
        rem = lw - lw_hi.astype(F32)
        lw_mid = rem.astype(BF16)
        lw_lo = (rem - lw_mid.astype(F32)).astype(BF16)
        c = _dot(tri3, jnp.concatenate([lw_hi, lw_mid, lw_lo], axis=0))
        ctot = c[last:last + 1, :]
        e_nc = jnp.exp(-c)
        e_tc = jnp.exp(ctot - c)
        kap_p = kap_ref[:, sl] * jnp.exp(c - lw)
        r_p = r_ref[:, sl] * jnp.exp(c)
        k_p = k_ref[:, sl]
        b_p = b_ref[:, sl]
        v_p = v_ref[:, sl]
        e_p = jnp.exp(ctot)
        for h in hs:
            x.append(jnp.concatenate([kap_p[:, h], r_p[:, h]], axis=0).astype(BF16))
            k_t.append((k_p * e_nc)[:, h].astype(BF16))
            b_t.append((b_p * e_nc)[:, h].astype(BF16))
            k_e.append((k_p * e_tc)[:, h].astype(BF16))
            b_e.append((b_p * e_tc)[:, h].astype(BF16))
            e_tot.append(e_p[:, h])
            v.append(v_p[:, h].astype(BF16))

    def bd(a, b):
        return _dot(a.astype(BF16), b.astype(BF16))

    g1 = [_dot_nt(x[i], k_t[i]) for i in heads]
    g2 = [_dot_nt(x[i], b_t[i]) for i in heads]
    a_kk = [jnp.where(earlier, g1[i][:C], 0.0).astype(BF16) for i in heads]
    a_rk = [jnp.where(incl, g1[i][C:], 0.0).astype(BF16) for i in heads]
    a_kb = [jnp.where(earlier, g2[i][:C], 0.0) for i in heads]
    a_rb = [jnp.where(incl, g2[i][C:], 0.0).astype(BF16) for i in heads]
    blk8 = same_block(8)
    a0 = [jnp.where(blk8, a_kb[i], 0.0) for i in heads]
    a2 = [bd(a0[i], a0[i]) for i in heads]
    a4 = [bd(a2[i], a2[i]) for i in heads]
    t = [bd(eye - a0[i], eye + a2[i]) for i in heads]
    t = [bd(t[i], eye + a4[i]) for i in heads]
    for n in (16, 32, 64):
        m = jnp.logical_and(same_block(n), jnp.logical_not(same_block(n // 2)))
        off = [jnp.where(m, a_kb[i], 0.0) for i in heads]
        ot = [bd(off[i], t[i]) for i in heads]
        t = [t[i] - bd(t[i], ot[i]) for i in heads]
    s0 = [s_ref[i] for i in heads]
    xs = [_dot_nt(x[i], s0[i].astype(BF16)) for i in heads]
    akv = [_dot(a_kk[i], v[i]) for i in heads]
    u = [bd(t[i], xs[i][:C] + akv[i]).astype(BF16) for i in heads]
    ys = [xs[i][C:] + _dot(a_rk[i], v[i]) - _dot(a_rb[i], u[i]) for i in heads]
    for i in heads:
        s_ref[i] = s0[i] * e_tot[i] + _dot_tn(v[i], k_e[i]) - _dot_tn(u[i], b_e[i])
    for p in range(npairs):
        y_ref[:, p * LANES:(p + 1) * LANES] = jnp.concatenate([ys[2 * p], ys[2 * p + 1]], axis=1)


def _delta_scan(lw, k, b, kap, v, r, *, n_ctx, reverse, pairs_per_block=None):
    R, W = lw.shape
    C = SCAN_CHUNK
    nchunks = R // C
    ctx_chunks = n_ctx // C
    npairs = W // LANES
    pb = npairs if pairs_per_block is None else pairs_per_block
    assert R % C == 0 and n_ctx % C == 0 and npairs % pb == 0

    if reverse:
        def rows(s):
            return jnp.where(s < ctx_chunks, ctx_chunks - 1 - s, nchunks - 1 - (s - ctx_chunks))
    else:
        def rows(s):
            return s

    spec = pl.BlockSpec((C, pb * LANES), lambda g, s: (rows(s), g))
    return pl.pallas_call(
        functools.partial(_scan_kernel, reverse=reverse, npairs=pb),
        grid=(npairs // pb, nchunks),
        in_specs=[spec] * 6,
        out_specs=spec,
        out_shape=jax.ShapeDtypeStruct((R, W), F32),
        scratch_shapes=[pltpu.VMEM((2 * pb, A_HEAD_DIM, A_HEAD_DIM), F32)],
        compiler_params=_params("parallel", "arbitrary"),
        name="delta_scan_rev" if reverse else "delta_scan_fwd",
    )(lw, k, b, kap, v, r)


def _rwkv_readout_kernel(yf_ref, yb_ref, r_ref, v_ref, k0_ref, k1_ref, gs_ref, gup_ref, rk_ref, lng_ref, lnb_ref,
                         o_ref):
    ones_bd = _head_block_ones()
    inv_n = 1.0 / A_HEAD_DIM
    ro = yf_ref[...] + yb_ref[...]
    mu = _head_sums(ro, ones_bd) * inv_n
    cen = ro - mu
    var = _head_sums(cen * cen, ones_bd) * inv_n
    yn = cen * lax.rsqrt(var + A_GN_EPS) * lng_ref[...] + lnb_ref[...]
    bonus = _head_sums(r_ref[...] * (k0_ref[...] + k1_ref[...]) * rk_ref[...], ones_bd) * v_ref[...]
    g = _dot(gs_ref[...], gup_ref[...])
    o_ref[...] = ((yn + bonus) * g).astype(o_ref.dtype)


def _rwkv_readout(yf, yb, r, v, k0, k1, gs, gup, r_k, ln_g, ln_b):
    R, W = yf.shape
    tm = _tile(R, 128, 16)
    wide = pl.BlockSpec((tm, W), lambda i: (i, 0))
    vec = pl.BlockSpec((1, W), lambda i: (0, 0))
    return pl.pallas_call(
        _rwkv_readout_kernel,
        grid=(R // tm,),
        in_specs=[wide] * 6 + [pl.BlockSpec((tm, A_GD_PAD), lambda i: (i, 0)),
                               pl.BlockSpec((A_GD_PAD, W), lambda i: (0, 0)), vec, vec, vec],
        out_specs=wide,
        out_shape=jax.ShapeDtypeStruct((R, W), BF16),
        compiler_params=_params("parallel"),
        name="rwkv_readout",
    )(yf, yb, r, v, k0, k1, gs, gup, r_k, ln_g, ln_b)


def _pool_kernel(u_ref, prev_ref, next_ref, w_ref, scale_ref, o_ref, ext_ref, *, ctx_tiles, n_tiles, n_ctx, n_lat):
    i = pl.program_id(0)
    first, last = _segment_flags(i, ctx_tiles, n_tiles)
    tm = u_ref.shape[0]
    H = POOL_HALO
    ext_ref[0:H, :] = jnp.where(first, 0.0, prev_ref[...])
    ext_ref[H:H + tm, :] = u_ref[...]
    ext_ref[H + tm:H + tm + H, :] = jnp.where(last, 0.0, next_ref[...])
    is_ctx = i < ctx_tiles
    seg_len = jnp.where(is_ctx, n_ctx, n_lat)
    t = lax.broadcasted_iota(jnp.int32, (tm, 1), 0) + i * tm - jnp.where(is_ctx, 0, n_ctx)
    for gi, win in enumerate(POOL_WINDOWS):
        cols = slice(gi * POOL_GROUP_W, (gi + 1) * POOL_GROUP_W)
        acc = None
        for o in range(-(win // 2), win // 2):
            term = ext_ref[H + o:H + o + tm, cols]
            acc = term if acc is None else acc + term
        lo = jnp.maximum(t - win // 2, 0)
        hi = jnp.minimum(t + win // 2 - 1, seg_len - 1)
        cnt = (hi - lo + 1).astype(F32)
        pooled = acc / cnt - u_ref[:, cols]
        y = _dot(pooled.astype(BF16), w_ref[gi]) * scale_ref[:, cols]
        o_ref[:, cols] = y.astype(o_ref.dtype)


def _pool(p, pool_w, pool_scale, *, n_ctx):
    R = p.shape[0]
    tm = _tile(math.gcd(n_ctx, R - n_ctx), 256, 16)
    n_tiles = R // tm
    hb = tm // POOL_HALO
    n_hblocks = R // POOL_HALO
    cb = P_B // B_WIDTH
    assert P_B % B_WIDTH == 0
    return pl.pallas_call(
        functools.partial(_pool_kernel, ctx_tiles=n_ctx // tm, n_tiles=n_tiles, n_ctx=n_ctx, n_lat=R - n_ctx),
        grid=(n_tiles,),
        in_specs=[pl.BlockSpec((tm, B_WIDTH), lambda i: (i, cb)),
                  pl.BlockSpec((POOL_HALO, B_WIDTH), lambda i: (jnp.maximum(i * hb - 1, 0), cb)),
                  pl.BlockSpec((POOL_HALO, B_WIDTH), lambda i: (jnp.minimum((i + 1) * hb, n_hblocks - 1), cb)),
                  pl.BlockSpec((len(POOL_WINDOWS), POOL_GROUP_W, POOL_GROUP_W), lambda i: (0, 0, 0)),
                  pl.BlockSpec((1, B_WIDTH), lambda i: (0, 0))],
        out_specs=pl.BlockSpec((tm, B_WIDTH), lambda i: (i, 0)),
        out_shape=jax.ShapeDtypeStruct((R, B_WIDTH), BF16),
        scratch_shapes=[pltpu.VMEM((tm + 2 * POOL_HALO, B_WIDTH), F32)],
        compiler_params=_params("parallel"),
        name="pool",
    )(p, p, p, pool_w, pool_scale)


ROPE_BLOCK = 512
QKV_WIDTH = C_WIDTH + 2 * C_KV_WIDTH
N_QK_BLOCKS = (C_WIDTH + C_KV_WIDTH) // ROPE_BLOCK


def _rope_kernel(u_ref, cos_ref, sin_ref, o_ref):
    j = pl.program_id(1)
    u = u_ref[...]

    @pl.when(j < N_QK_BLOCKS)
    def _():
        cos = cos_ref[...]
        sin = sin_ref[...]
        for h in range(ROPE_BLOCK // C_HEAD_DIM):
            cols = slice(h * C_HEAD_DIM, (h + 1) * C_HEAD_DIM)
            x = u[:, cols]
            partner = pltpu.roll(x, C_HEAD_DIM // 2, axis=1)
            o_ref[:, cols] = (x * cos + partner * sin).astype(o_ref.dtype)

    @pl.when(j >= N_QK_BLOCKS)
    def _():
        o_ref[...] = u.astype(o_ref.dtype)


def _rope_pack(p, cos_tab, sin_tab):
    R = p.shape[0]
    tm = _tile(R, 256, 16)
    nq = C_WIDTH // ROPE_BLOCK
    assert P_Q % ROPE_BLOCK == 0 and P_K % ROPE_BLOCK == 0 and P_V == P_K + ROPE_BLOCK

    def src(i, j):
        return i, jnp.where(j < nq, P_Q // ROPE_BLOCK + j, P_K // ROPE_BLOCK + j - nq)

    tab = pl.BlockSpec((tm, C_HEAD_DIM), lambda i, j: (i, 0))
    return pl.pallas_call(
        _rope_kernel,
        grid=(R // tm, QKV_WIDTH // ROPE_BLOCK),
        in_specs=[pl.BlockSpec((tm, ROPE_BLOCK), src), tab, tab],
        out_specs=pl.BlockSpec((tm, ROPE_BLOCK), lambda i, j: (i, j)),
        out_shape=jax.ShapeDtypeStruct((R, QKV_WIDTH), BF16),
        compiler_params=_params("parallel", "arbitrary"),
        name="rope_pack",
    )(p, cos_tab, sin_tab)


def _rope_tables(n_ctx, n_lat):
    half = C_HEAD_DIM // 2
    t = jnp.arange(n_lat)
    row = (t // GRID_W).astype(F32)
    col = (t % GRID_W).astype(F32)
    inv = ROPE_BASE ** (-jnp.arange(0, half, 2, dtype=F32) / half)
    ar = row[:, None] * inv[None]
    ac = col[:, None] * inv[None]
    cos = jnp.concatenate([jnp.cos(ar), jnp.cos(ac), jnp.cos(ar), jnp.cos(ac)], axis=1)
    sin = jnp.concatenate([-jnp.sin(ar), -jnp.sin(ac), jnp.sin(ar), jnp.sin(ac)], axis=1)
    cos = jnp.concatenate([jnp.ones((n_ctx, C_HEAD_DIM), F32), cos], axis=0)
    sin = jnp.concatenate([jnp.zeros((n_ctx, C_HEAD_DIM), F32), sin], axis=0)
    return cos, sin


def _attn_kernel(q_ref, kp_ref, kc_ref, kn_ref, kx_ref, vp_ref, vc_ref, vn_ref, vx_ref, sink_ref, o_ref,
                 *, ctx_qblocks, n_qblocks):
    i = pl.program_id(0)
    B = ATTN_BLOCK
    tq = q_ref.shape[0]
    nloc = tq + 2 * B
    qrow = lax.broadcasted_iota(jnp.int32, (tq, nloc), 0)
    kcol = lax.broadcasted_iota(jnp.int32, (tq, nloc), 1)
    rel = kcol - B - qrow
    lo = jnp.where(i == ctx_qblocks, B, 0)
    hi = jnp.where(i < ctx_qblocks, 0, jnp.where(i == n_qblocks - 1, B + tq, nloc))
    bias = jnp.where(jnp.abs(rel) <= ATTN_BLOCK, 0.0, NEG_INF)
    bias = jnp.where(kcol >= lo, bias, NEG_INF)
    bias = jnp.where(kcol < hi, bias, NEG_INF)
    scale = C_HEAD_DIM ** -0.5
    for h in range(C_KV_HEADS):
        kc = slice(h * C_HEAD_DIM, (h + 1) * C_HEAD_DIM)
        k_loc = jnp.concatenate([kp_ref[:, kc], kc_ref[:, kc], kn_ref[:, kc]], axis=0)
        v_loc = jnp.concatenate([vp_ref[:, kc], vc_ref[:, kc], vn_ref[:, kc]], axis=0)
        k_ctx = kx_ref[:, kc]
        v_ctx = vx_ref[:, kc]
        for g in range(C_GROUP):
            j = h * C_GROUP + g
            cols = slice(j * C_HEAD_DIM, (j + 1) * C_HEAD_DIM)
            q = q_ref[:, cols]
            s_loc = _dot_nt(q, k_loc) * scale + bias
            s_ctx = _dot_nt(q, k_ctx) * scale
            sk = sink_ref[j:j + 1, 0:1]
            m = jnp.maximum(jnp.maximum(jnp.max(s_loc, axis=-1, keepdims=True),
                                        jnp.max(s_ctx, axis=-1, keepdims=True)), sk)
            e_loc = jnp.exp(s_loc - m)
            e_ctx = jnp.exp(s_ctx - m)
            denom = (jnp.sum(e_loc, axis=-1, keepdims=True) + jnp.sum(e_ctx, axis=-1, keepdims=True)
                     + jnp.exp(sk - m))
            o = _dot(e_loc.astype(BF16), v_loc) + _dot(e_ctx.astype(BF16), v_ctx)
            o_ref[:, cols] = (o / denom).astype(o_ref.dtype)


def _attention(qkv, sink16, *, n_ctx):
    R = qkv.shape[0]
    B = ATTN_BLOCK
    tq = _tile(math.gcd(n_ctx, R - n_ctx), 2 * B, B)
    per = tq // B
    n_qblocks = R // tq
    n_blocks = R // B
    ctx_blocks = n_ctx // B
    assert n_ctx % B == 0 and R % B == 0
    kcol = C_WIDTH // C_KV_WIDTH
    vcol = kcol + 1

    def prev_spec(c):
        return pl.BlockSpec((B, C_KV_WIDTH), lambda i: (jnp.clip(i * per - 1, ctx_blocks, n_blocks - 1), c))

    def next_spec(c):
        return pl.BlockSpec((B, C_KV_WIDTH), lambda i: (jnp.clip((i + 1) * per, ctx_blocks, n_blocks - 1), c))

    def own_spec(c):
        return pl.BlockSpec((tq, C_KV_WIDTH), lambda i: (i, c))

    def ctx_spec(c):
        return pl.BlockSpec((n_ctx, C_KV_WIDTH), lambda i: (0, c))

    return pl.pallas_call(
        functools.partial(_attn_kernel, ctx_qblocks=n_ctx // tq, n_qblocks=n_qblocks),
        grid=(n_qblocks,),
        in_specs=[pl.BlockSpec((tq, C_WIDTH), lambda i: (i, 0)),
                  prev_spec(kcol), own_spec(kcol), next_spec(kcol), ctx_spec(kcol),
                  prev_spec(vcol), own_spec(vcol), next_spec(vcol), ctx_spec(vcol),
                  pl.BlockSpec((C_Q_HEADS, LANES), lambda i: (0, 0))],
        out_specs=pl.BlockSpec((tq, C_WIDTH), lambda i: (i, 0)),
        out_shape=jax.ShapeDtypeStruct((R, C_WIDTH), BF16),
        compiler_params=_params("parallel"),
        name="window_attention",
    )(qkv, qkv, qkv, qkv, qkv, qkv, qkv, qkv, qkv, sink16)


def _pad_to(a, axis, size):
    pad = [(0, 0)] * a.ndim
    pad[axis] = (0, size - a.shape[axis])
    return jnp.pad(a, pad)


def _relayout_w_in(w_in):
    seg = lambda lo, hi: w_in[..., lo:hi]

    def rope_order(w):
        lead = w.shape[:-1]
        w = w.reshape(*lead, -1, 2, 2, C_HEAD_DIM // 4)
        return jnp.swapaxes(w, -3, -2).reshape(*lead, -1)

    return jnp.concatenate([
        _pad_to(seg(0, OFF_B), -1, A_PAD), rope_order(seg(OFF_K, OFF_V)), seg(OFF_V, OFF_G), seg(OFF_B, OFF_Q),
        rope_order(seg(OFF_Q, OFF_K)), seg(OFF_G, IN_COLS)], axis=-1).astype(BF16)


def _low_rank_pair(up):
    z = jnp.zeros_like(up[:, 0])
    return jnp.stack([jnp.concatenate([up[:, 0], z], axis=1), jnp.concatenate([z, up[:, 1]], axis=1)], axis=1).astype(BF16)


def kernel(x, c, ctx, c_ctx, mod_down, mod_up, mod_b, norm_g, w_in, shift_mu, rwkv_w0, rwkv_w_up, rwkv_a0,
           rwkv_a_up, rwkv_g_up, rwkv_k_k, rwkv_k_a, rwkv_r_k, rwkv_ln_g, rwkv_ln_b, pool_w, pool_scale,
           attn_sink, gate_up, w_branch_a, w_branch_b, w_branch_c, w_out, ffn_w1, ffn_w3, ffn_w2):
    assert x.shape[0] == 1 and ctx.shape[0] == 1 and c.shape[0] == 1
    depth = w_in.shape[0]
    T, D = x.shape[1], x.shape[2]
    L = ctx.shape[1]
    assert 2 * A_DECAY_RANK == LANES and 2 * A_ICLR_RANK == LANES

    w_in_p = _relayout_w_in(w_in)
    mu_p = _pad_to(shift_mu, -1, A_PAD)[:, None, :]
    wup2 = _low_rank_pair(rwkv_w_up)
    aup2 = _low_rank_pair(rwkv_a_up)
    gup_p = _pad_to(rwkv_g_up, 1, A_GD_PAD).astype(BF16)
    pool_w_b = pool_w.astype(BF16)
    gate_up_b = gate_up.astype(BF16)
    wa_b, wb_b, wc_b = (w.astype(BF16) for w in (w_branch_a, w_branch_b, w_branch_c))
    w_out_b = w_out.astype(BF16)
    w1_b, w3_b, w2_b = (w.astype(BF16) for w in (ffn_w1, ffn_w3, ffn_w2))
    sink16 = jnp.broadcast_to(attn_sink[..., None], (depth, C_Q_HEADS, LANES))
    cos_tab, sin_tab = _rope_tables(L, T)

    c8 = _pad_to(jnp.concatenate([c_ctx[None], c], axis=0), 0, SUBLANES)
    mod = _modulation(c8, mod_down.astype(BF16), mod_up.astype(BF16), mod_b).reshape(depth, SUBLANES, 6, D)

    def mod6(l, shift_i, scale_i, gate_i):
        m = mod[l]
        return jnp.stack([m[0, gate_i], m[1, gate_i], m[0, scale_i], m[1, scale_i], m[0, shift_i], m[1, shift_i]])

    xs = jnp.concatenate([ctx[0], x[0]], axis=0)
    _, h = _resid_norm(xs, None, jnp.stack([norm_g[0, 0], norm_g[0, 0]]), mod6(0, 0, 1, 2), n_ctx=L, emit_h=True)
    for l in range(depth):
        p = _matmul(h, w_in_p, l, tn_cap=768, name="w_in")
        r, v, kap, lw0, lw1, k0, k1, b0, b1, gs = _rwkv_prep(
            p, mu_p[l], rwkv_w0[l], wup2[l], rwkv_a0[l], aup2[l], rwkv_k_k[l][None], rwkv_k_a[l][None], n_ctx=L)
        yf = _delta_scan(lw0, k0, b0, kap, v, r, n_ctx=L, reverse=False)
        yr = _delta_scan(lw1, k1, b1, kap, v, r, n_ctx=L, reverse=True)
        y_a = _rwkv_readout(yf, yr, r, v, k0, k1, gs, gup_p[l], rwkv_r_k[l][None], rwkv_ln_g[l][None],
                            rwkv_ln_b[l][None])
        y_b = _pool(p, pool_w_b[l], pool_scale[l][None], n_ctx=L)
        y_c = _attention(_rope_pack(p, cos_tab, sin_tab), sink16[l], n_ctx=L)
        acc = _merge(p, y_a, y_b, y_c, gate_up_b, wa_b, wb_b, wc_b, l)
        mix = _matmul(acc, w_out_b, l, name="w_out")
        xs, h2 = _resid_norm(xs, mix, norm_g[l, 1:3], mod6(l, 3, 4, 2), n_ctx=L, emit_h=True)
        f = _matmul(_ffn_up(h2, w1_b, w3_b, l), w2_b, l, tk_cap=5504, name="ffn_down")
        if l + 1 < depth:
            g2 = jnp.stack([norm_g[l, 3], norm_g[l + 1, 0]])
            m6 = jnp.concatenate([mod6(l, 0, 1, 5)[:2], mod6(l + 1, 0, 1, 2)[2:]], axis=0)
            xs, h = _resid_norm(xs, f, g2, m6, n_ctx=L, emit_h=True)
        else:
            xs, _ = _resid_norm(xs, f, jnp.stack([norm_g[l, 3], norm_g[l, 3]]), mod6(l, 0, 1, 5), n_ctx=L,
                                emit_h=False)
    return xs[L:][None]
```

```python
import functools
import math

import jax
import jax.numpy as jnp
from jax import lax
from jax.experimental import pallas as pl
from jax.experimental.pallas import tpu as pltpu

F32 = jnp.float32
BF16 = jnp.bfloat16

LANES = 128
SUBLANES = 8
VMEM_LIMIT = 56 * 1024 * 1024

NORM_EPS = 1e-6
GRID_W = 64
ROPE_BASE = 10000.0
NEG_INF = -1e30

A_HEADS = 24
A_HEAD_DIM = 64
A_WIDTH = A_HEADS * A_HEAD_DIM
A_DECAY_RANK = 64
A_ICLR_RANK = 64
A_GATE_RANK = 224
A_GN_EPS = 64e-5
A_COLS = 3 * A_WIDTH + 2 * A_DECAY_RANK + 2 * A_ICLR_RANK + A_GATE_RANK
SCAN_CHUNK = 64
POOL_WINDOWS = (2, 4, 8, 16)
POOL_GROUP_W = 384
B_WIDTH = len(POOL_WINDOWS) * POOL_GROUP_W
POOL_HALO = 8
C_Q_HEADS = 16
C_KV_HEADS = 4
C_GROUP = C_Q_HEADS // C_KV_HEADS
C_HEAD_DIM = 128
C_WIDTH = C_Q_HEADS * C_HEAD_DIM
C_KV_WIDTH = C_KV_HEADS * C_HEAD_DIM
ATTN_BLOCK = 128
GATE_RANK = 256
N_BRANCH = 3

OFF_B = A_COLS
OFF_Q = OFF_B + B_WIDTH
OFF_K = OFF_Q + C_WIDTH
OFF_V = OFF_K + C_KV_WIDTH
OFF_G = OFF_V + C_KV_WIDTH
IN_COLS = OFF_G + GATE_RANK
A_PAD = 5120
A_GD_OFF = 3 * A_WIDTH + 2 * A_DECAY_RANK + 2 * A_ICLR_RANK
A_GD_PAD = A_PAD - A_GD_OFF
P_K = A_PAD
P_V = P_K + C_KV_WIDTH
P_B = P_V + C_KV_WIDTH
P_Q = P_B + B_WIDTH
P_G = P_Q + C_WIDTH
P_COLS = P_G + GATE_RANK


def _dot(a, b):
    return jnp.dot(a, b, preferred_element_type=F32)


def _dot_nt(a, b):
    return lax.dot_general(a, b, (((1,), (1,)), ((), ())), preferred_element_type=F32)


def _dot_tn(a, b):
    return lax.dot_general(a, b, (((0,), (0,)), ((), ())), preferred_element_type=F32)


def _tile(n, cap, mult):
    best = None
    for t in range(mult, min(n, cap) + 1, mult):
        if n % t == 0:
            best = t
    assert best is not None, (n, cap, mult)
    return best


def _params(*sem):
    return pltpu.CompilerParams(dimension_semantics=sem, vmem_limit_bytes=VMEM_LIMIT)


def _sigmoid(x):
    return 1.0 / (1.0 + jnp.exp(-x))


def _silu(x):
    return x * _sigmoid(x)


def _split3(x):
    hi = x.astype(BF16)
    rem = x - hi.astype(F32)
    mid = rem.astype(BF16)
    lo = (rem - mid.astype(F32)).astype(BF16)
    return hi, mid, lo


def _head_block_ones():
    r = lax.broadcasted_iota(jnp.int32, (3 * LANES, LANES), 0)
    c = lax.broadcasted_iota(jnp.int32, (3 * LANES, LANES), 1)
    return jnp.where(((r % LANES) // A_HEAD_DIM) == (c // A_HEAD_DIM), 1.0, 0.0).astype(BF16)


def _head_sums(x, ones_bd):
    out = []
    for j in range(0, x.shape[1], LANES):
        out.append(_dot(jnp.concatenate(_split3(x[:, j:j + LANES]), axis=1), ones_bd))
    return jnp.concatenate(out, axis=1)


def _mod_kernel(c_ref, down_ref, up_ref, b_ref, o_ref):
    s = _silu(c_ref[...]).astype(BF16)
    low = _dot(s, down_ref[0]).astype(BF16)
    o_ref[0] = _dot(low, up_ref[0]) + b_ref[0]


def _modulation(c8, down, up, bias):
    depth, d, rank = down.shape
    n = up.shape[2]
    return pl.pallas_call(
        _mod_kernel,
        grid=(depth,),
        in_specs=[pl.BlockSpec((SUBLANES, d), lambda l: (0, 0)),
                  pl.BlockSpec((1, d, rank), lambda l: (l, 0, 0)),
                  pl.BlockSpec((1, rank, n), lambda l: (l, 0, 0)),
                  pl.BlockSpec((1, 1, n), lambda l: (l, 0, 0))],
        out_specs=pl.BlockSpec((1, SUBLANES, n), lambda l: (l, 0, 0)),
        out_shape=jax.ShapeDtypeStruct((depth, SUBLANES, n), F32),
        compiler_params=_params("arbitrary"),
        name="modulation",
    )(c8, down, up, bias.reshape(depth, 1, n))


def _rms(x, g):
    return x * lax.rsqrt(jnp.mean(x * x, axis=-1, keepdims=True) + NORM_EPS) * g


def _resid_norm_kernel(*refs, ctx_tiles, has_m, emit_h):
    refs = list(refs)
    x_ref = refs.pop(0)
    m_ref = refs.pop(0) if has_m else None
    g_ref = refs.pop(0)
    mod_ref = refs.pop(0)
    is_ctx = pl.program_id(0) < ctx_tiles

    def pick(i):
        return jnp.where(is_ctx, mod_ref[i:i + 1, :], mod_ref[i + 1:i + 2, :])

    x = x_ref[...]
    if has_m:
        x = x + pick(0) * _rms(m_ref[...].astype(F32), g_ref[0:1, :])
        refs.pop(0)[...] = x
    if emit_h:
        h = _rms(x, g_ref[1:2, :]) * (1.0 + pick(2)) + pick(4)
        refs.pop(0)[...] = h.astype(BF16)


def _resid_norm(x, m, g2, mod6, *, n_ctx, emit_h, latent_only=False):
    R, D = x.shape
    te = _tile(math.gcd(n_ctx, R - n_ctx), 256, 16)
    has_m = m is not None
    skip = n_ctx // te if latent_only else 0
    n_out = R - skip * te
    row_in = pl.BlockSpec((te, D), lambda i: (i + skip, 0))
    row = pl.BlockSpec((te, D), lambda i: (i, 0))
    ins = [x] + ([m] if has_m else []) + [g2, mod6]
    in_specs = [row_in] * (2 if has_m else 1) + [pl.BlockSpec((2, D), lambda i: (0, 0)),
                                                 pl.BlockSpec((6, D), lambda i: (0, 0))]
    out_shape, out_specs = [], []
    if has_m:
        out_shape.append(jax.ShapeDtypeStruct((n_out, D), F32))
        out_specs.append(row)
    if emit_h:
        out_shape.append(jax.ShapeDtypeStruct((n_out, D), BF16))
        out_specs.append(row)
    outs = pl.pallas_call(
        functools.partial(_resid_norm_kernel, ctx_tiles=n_ctx // te - skip, has_m=has_m, emit_h=emit_h),
        grid=(n_out // te,),
        in_specs=in_specs, out_specs=out_specs, out_shape=out_shape,
        compiler_params=_params("parallel"),
        name="resid_norm",
    )(*ins)
    outs = list(outs)
    x1 = outs.pop(0) if has_m else None
    h = outs.pop(0) if emit_h else None
    return x1, h


def _mm_kernel(x_ref, w_ref, o_ref, *scratch, nk):
    part = _dot(x_ref[...].astype(BF16), w_ref[...])
    if nk == 1:
        o_ref[...] = part.astype(o_ref.dtype)
        return
    acc_ref, = scratch
    k = pl.program_id(2)

    @pl.when(k == 0)
    def _():
        acc_ref[...] = part

    @pl.when(k > 0)
    def _():
        acc_ref[...] += part

    @pl.when(k == nk - 1)
    def _():
        o_ref[...] = acc_ref[...].astype(o_ref.dtype)


def _matmul(x, w, layer, *, out_dtype=F32, tm_cap=1280, tn_cap=512, tk_cap=4096, name="matmul"):
    M, K = x.shape
    N = w.shape[2]
    tm = _tile(M, tm_cap, 16)
    tn = _tile(N, tn_cap, LANES)
    tk = _tile(K, tk_cap, LANES)
    nk = K // tk
    return pl.pallas_call(
        functools.partial(_mm_kernel, nk=nk),
        grid=(M // tm, N // tn, nk),
        in_specs=[pl.BlockSpec((tm, tk), lambda i, j, k: (i, k)),
                  pl.BlockSpec((None, tk, tn), lambda i, j, k: (layer, k, j))],
        out_specs=pl.BlockSpec((tm, tn), lambda i, j, k: (i, j)),
        out_shape=jax.ShapeDtypeStruct((M, N), out_dtype),
        scratch_shapes=[pltpu.VMEM((tm, tn), F32)] if nk > 1 else [],
        compiler_params=_params("parallel", "parallel", "arbitrary"),
        name=name,
    )(x, w)


def _ffn_up_kernel(h_ref, w1_ref, w3_ref, o_ref):
    h = h_ref[...]
    o_ref[...] = (_silu(_dot(h, w1_ref[...])) * _dot(h, w3_ref[...])).astype(o_ref.dtype)


def _ffn_up(h, w1, w3, layer):
    M, K = h.shape
    N = w1.shape[2]
    tm = _tile(M, 1280, 16)
    tn = min(512, N)
    assert N % LANES == 0
    wspec = pl.BlockSpec((None, K, tn), lambda i, j: (layer, 0, j))
    return pl.pallas_call(
        _ffn_up_kernel,
        grid=(M // tm, pl.cdiv(N, tn)),
        in_specs=[pl.BlockSpec((tm, K), lambda i, j: (i, 0)), wspec, wspec],
        out_specs=pl.BlockSpec((tm, tn), lambda i, j: (i, j)),
        out_shape=jax.ShapeDtypeStruct((M, N), BF16),
        compiler_params=_params("parallel", "parallel"),
        name="ffn_up",
    )(h, w1, w3)


def _merge_kernel(pg_ref, ya_ref, yb_ref, yc_ref, ga_ref, gb_ref, gc_ref, wa_ref, wb_ref, wc_ref, o_ref):
    pg = pg_ref[...].astype(BF16)
    acc = _sigmoid(_dot(pg, ga_ref[...])) * _dot(ya_ref[...], wa_ref[...])
    acc += _sigmoid(_dot(pg, gb_ref[...])) * _dot(yb_ref[...], wb_ref[...])
    acc += _sigmoid(_dot(pg, gc_ref[...])) * _dot(yc_ref[...], wc_ref[...])
    o_ref[...] = acc.astype(o_ref.dtype)


def _merge(p, ya, yb, yc, gate_up, wa, wb, wc, layer):
    R = p.shape[0]
    D = wa.shape[2]
    tm = _tile(R, 1280, 16)
    tn = _tile(D, 512, LANES)
    nj = D // tn

    def rows(width):
        return pl.BlockSpec((tm, width), lambda i, j: (i, 0))

    def gate(branch):
        return pl.BlockSpec((None, GATE_RANK, tn), lambda i, j: (layer, 0, branch * nj + j))

    def wcol(width):
        return pl.BlockSpec((None, width, tn), lambda i, j: (layer, 0, j))

    return pl.pallas_call(
        _merge_kernel,
        grid=(R // tm, nj),
        in_specs=[pl.BlockSpec((tm, GATE_RANK), lambda i, j: (i, P_G // GATE_RANK)),
                  rows(A_WIDTH), rows(B_WIDTH), rows(C_WIDTH),
                  gate(0), gate(1), gate(2), wcol(A_WIDTH), wcol(B_WIDTH), wcol(C_WIDTH)],
        out_specs=pl.BlockSpec((tm, tn), lambda i, j: (i, j)),
        out_shape=jax.ShapeDtypeStruct((R, D), BF16),
        compiler_params=_params("parallel", "parallel"),
        name="merge",
    )(p, ya, yb, yc, gate_up, gate_up, gate_up, wa, wb, wc)


def _segment_flags(i, ctx_tiles, n_tiles):
    first = jnp.logical_or(i == 0, i == ctx_tiles)
    last = jnp.logical_or(i == ctx_tiles - 1, i == n_tiles - 1)
    return first, last


def _rwkv_prep_kernel(u_ref, prev_ref, next_ref, mu_ref, w0_ref, wup_ref, a0_ref, aup_ref, kk_ref, ka_ref,
                      r_out, v_out, kap_out, lw0_out, lw1_out, k0_out, k1_out, b0_out, b1_out, gs_out,
                      *, ctx_tiles, n_tiles):
    i = pl.program_id(0)
    first, last = _segment_flags(i, ctx_tiles, n_tiles)
    u = u_ref[...]
    tm = u.shape[0]
    rid = lax.broadcasted_iota(jnp.int32, (tm, 1), 0)
    prev_row = jnp.where(first, 0.0, prev_ref[SUBLANES - 1:SUBLANES, :])
    next_row = jnp.where(last, 0.0, next_ref[0:1, :])
    prev = jnp.where(rid == 0, prev_row, pltpu.roll(u, 1, axis=0))
    nxt = jnp.where(rid == tm - 1, next_row, pltpu.roll(u, tm - 1, axis=0))
    s = u + mu_ref[...] * (0.5 * (prev + nxt) - u)

    W = A_WIDTH
    r = s[:, 0:W]
    k = s[:, W:2 * W]
    v = s[:, 2 * W:3 * W]
    o = 3 * W
    wd = jnp.tanh(s[:, o:o + LANES]).astype(BF16)
    ad = s[:, o + LANES:o + 2 * LANES].astype(BF16)
    gd = s[:, A_GD_OFF:A_PAD]

    ones_bd = _head_block_ones()
    kk = k * kk_ref[...]
    nrm = jnp.sqrt(_head_sums(kk * kk, ones_bd))
    kk = kk / jnp.maximum(nrm, 1e-12)

    r_out[...] = r.astype(BF16)
    v_out[...] = v.astype(BF16)
    kap_out[...] = kk.astype(BF16)
    gs_out[...] = _sigmoid(gd).astype(BF16)
    for d, (lw_out, k_out, b_out) in enumerate(((lw0_out, k0_out, b0_out), (lw1_out, k1_out, b1_out))):
        z = w0_ref[d:d + 1, :] + _dot(wd, wup_ref[d])
        lw_out[...] = -math.exp(-0.5) * _sigmoid(z)
        a = _sigmoid(a0_ref[d:d + 1, :] + _dot(ad, aup_ref[d]))
        k_out[...] = (k * (1.0 + (a - 1.0) * ka_ref[...])).astype(BF16)
        b_out[...] = (kk * a).astype(BF16)


def _rwkv_prep(p, mu, w0, wup2, a0, aup2, k_k, k_a, *, n_ctx):
    R = p.shape[0]
    tm = _tile(math.gcd(n_ctx, R - n_ctx), 128, 16)
    n_tiles = R // tm
    hb = tm // SUBLANES
    n_hblocks = R // SUBLANES
    W = A_WIDTH

    def const(shape):
        return pl.BlockSpec(shape, lambda i: (0,) * len(shape))

    wide = pl.BlockSpec((tm, W), lambda i: (i, 0))
    f32w = jax.ShapeDtypeStruct((R, W), F32)
    bf16w = jax.ShapeDtypeStruct((R, W), BF16)
    return pl.pallas_call(
        functools.partial(_rwkv_prep_kernel, ctx_tiles=n_ctx // tm, n_tiles=n_tiles),
        grid=(n_tiles,),
        in_specs=[pl.BlockSpec((tm, A_PAD), lambda i: (i, 0)),
                  pl.BlockSpec((SUBLANES, A_PAD), lambda i: (jnp.maximum(i * hb - 1, 0), 0)),
                  pl.BlockSpec((SUBLANES, A_PAD), lambda i: (jnp.minimum((i + 1) * hb, n_hblocks - 1), 0)),
                  const((1, A_PAD)), const((2, W)), const((2, LANES, W)), const((2, W)), const((2, LANES, W)),
                  const((1, W)), const((1, W))],
        out_specs=[wide] * 9 + [pl.BlockSpec((tm, A_GD_PAD), lambda i: (i, 0))],
        out_shape=[bf16w] * 3 + [f32w] * 2 + [bf16w] * 4 + [jax.ShapeDtypeStruct((R, A_GD_PAD), BF16)],
        compiler_params=_params("parallel"),
        name="rwkv_prep",
    )(p, p, p, mu, w0, wup2, a0, aup2, k_k, k_a)


def _scan_kernel(lw_ref, k_ref, b_ref, kap_ref, v_ref, r_ref, y_ref, s_ref, *, reverse, npairs):
    C = SCAN_CHUNK
    N = A_HEAD_DIM

    @pl.when(pl.program_id(1) == 0)
    def _():
        s_ref[...] = jnp.zeros_like(s_ref)

    row = lax.broadcasted_iota(jnp.int32, (C, C), 0)
    col = lax.broadcasted_iota(jnp.int32, (C, C), 1)
    earlier = (col > row) if reverse else (col < row)
    diag = col == row
    incl = jnp.logical_or(earlier, diag)
    tri = jnp.where(incl, 1.0, 0.0).astype(BF16)
    tri3 = jnp.concatenate([tri, tri, tri], axis=1)
    eye = jnp.where(diag, 1.0, 0.0).astype(F32)

    def same_block(n):
        return (row // n) == (col // n)

    last = 0 if reverse else C - 1

    hs = (slice(0, N), slice(N, 2 * N))
    heads = range(2 * npairs)
    x, k_t, b_t, k_e, b_e, e_tot, v = ([] for _ in range(7))
    for p in range(npairs):
        sl = slice(p * LANES, (p + 1) * LANES)
        lw = lw_ref[:, sl]
        c = _dot(tri3, jnp.concatenate(_split3(lw), axis=0))
        ctot = c[last:last + 1, :]
        e_nc = jnp.exp(-c)
        e_tc = jnp.exp(ctot - c)
        kap_p = kap_ref[:, sl].astype(F32) * jnp.exp(c - lw)
        r_p = r_ref[:, sl].astype(F32) * jnp.exp(c)
        k_p = k_ref[:, sl].astype(F32)
        b_p = b_ref[:, sl].astype(F32)
        v_p = v_ref[:, sl]
        e_p = jnp.exp(ctot)
        for h in hs:
            x.append(jnp.concatenate([kap_p[:, h], r_p[:, h]], axis=0).astype(BF16))
            k_t.append((k_p * e_nc)[:, h].astype(BF16))
            b_t.append((b_p * e_nc)[:, h].astype(BF16))
            k_e.append((k_p * e_tc)[:, h].astype(BF16))
            b_e.append((b_p * e_tc)[:, h].astype(BF16))
            e_tot.append(e_p[:, h])
            v.append(v_p[:, h].astype(BF16))

    def bd(a, b):
        return _dot(a.astype(BF16), b.astype(BF16))

    g1 = [_dot_nt(x[i], k_t[i]) for i in heads]
    g2 = [_dot_nt(x[i], b_t[i]) for i in heads]
    a_kk = [jnp.where(earlier, g1[i][:C], 0.0).astype(BF16) for i in heads]
    a_rk = [jnp.where(incl, g1[i][C:], 0.0).astype(BF16) for i in heads]
    a_kb = [jnp.where(earlier, g2[i][:C], 0.0) for i in heads]
    a_rb = [jnp.where(incl, g2[i][C:], 0.0).astype(BF16) for i in heads]
    blk8 = same_block(8)
    a0 = [jnp.where(blk8, a_kb[i], 0.0) for i in heads]
    a2 = [bd(a0[i], a0[i]) for i in heads]
    a4 = [bd(a2[i], a2[i]) for i in heads]
    t = [bd(eye - a0[i], eye + a2[i]) for i in heads]
    t = [bd(t[i], eye + a4[i]) for i in heads]
    for n in (16, 32, 64):
        m = jnp.logical_and(same_block(n), jnp.logical_not(same_block(n // 2)))
        off = [jnp.where(m, a_kb[i], 0.0) for i in heads]
        ot = [bd(off[i], t[i]) for i in heads]
        t = [t[i] - bd(t[i], ot[i]) for i in heads]
    s0 = [s_ref[i] for i in heads]
    xs = [_dot_nt(x[i], s0[i].astype(BF16)) for i in heads]
    akv = [_dot(a_kk[i], v[i]) for i in heads]
    u = [bd(t[i], xs[i][:C] + akv[i]).astype(BF16) for i in heads]
    ys = [xs[i][C:] + _dot(a_rk[i], v[i]) - _dot(a_rb[i], u[i]) for i in heads]
    for i in heads:
        s_ref[i] = s0[i] * e_tot[i] + _dot_tn(v[i], k_e[i]) - _dot_tn(u[i], b_e[i])
    for p in range(npairs):
        y_ref[:, p * LANES:(p + 1) * LANES] = jnp.concatenate([ys[2 * p], ys[2 * p + 1]], axis=1)


def _delta_scan(lw, k, b, kap, v, r, *, n_ctx, reverse, pairs_per_block=None):
    R, W = lw.shape
    C = SCAN_CHUNK
    nchunks = R // C
    ctx_chunks = n_ctx // C
    npairs = W // LANES
    pb = npairs if pairs_per_block is None else pairs_per_block
    assert R % C == 0 and n_ctx % C == 0 and npairs % pb == 0

    if reverse:
        def rows(s):
            return jnp.where(s < ctx_chunks, ctx_chunks - 1 - s, nchunks - 1 - (s - ctx_chunks))
    else:
        def rows(s):
            return s

    spec = pl.BlockSpec((C, pb * LANES), lambda g, s: (rows(s), g))
    return pl.pallas_call(
        functools.partial(_scan_kernel, reverse=reverse, npairs=pb),
        grid=(npairs // pb, nchunks),
        in_specs=[spec] * 6,
        out_specs=spec,
        out_shape=jax.ShapeDtypeStruct((R, W), F32),
        scratch_shapes=[pltpu.VMEM((2 * pb, A_HEAD_DIM, A_HEAD_DIM), F32)],
        compiler_params=_params("parallel", "arbitrary"),
        name="delta_scan_rev" if reverse else "delta_scan_fwd",
    )(lw, k, b, kap, v, r)


def _rwkv_readout_kernel(yf_ref, yb_ref, r_ref, v_ref, k0_ref, k1_ref, gs_ref, gup_ref, rk_ref, lng_ref, lnb_ref,
                         o_ref):
    ones_bd = _head_block_ones()
    inv_n = 1.0 / A_HEAD_DIM
    ro = yf_ref[...] + yb_ref[...]
    mu = _head_sums(ro, ones_bd) * inv_n
    cen = ro - mu
    var = _head_sums(cen * cen, ones_bd) * inv_n
    yn = cen * lax.rsqrt(var + A_GN_EPS) * lng_ref[...] + lnb_ref[...]
    rk = r_ref[...].astype(F32) * (k0_ref[...].astype(F32) + k1_ref[...].astype(F32)) * rk_ref[...]
    bonus = _head_sums(rk, ones_bd) * v_ref[...].astype(F32)
    g = _dot(gs_ref[...], gup_ref[...])
    o_ref[...] = ((yn + bonus) * g).astype(o_ref.dtype)


def _rwkv_readout(yf, yb, r, v, k0, k1, gs, gup, r_k, ln_g, ln_b):
    R, W = yf.shape
    tm = _tile(R, 128, 16)
    wide = pl.BlockSpec((tm, W), lambda i: (i, 0))
    vec = pl.BlockSpec((1, W), lambda i: (0, 0))
    return pl.pallas_call(
        _rwkv_readout_kernel,
        grid=(R // tm,),
        in_specs=[wide] * 6 + [pl.BlockSpec((tm, A_GD_PAD), lambda i: (i, 0)),
                               pl.BlockSpec((A_GD_PAD, W), lambda i: (0, 0)), vec, vec, vec],
        out_specs=wide,
        out_shape=jax.ShapeDtypeStruct((R, W), BF16),
        compiler_params=_params("parallel"),
        name="rwkv_readout",
    )(yf, yb, r, v, k0, k1, gs, gup, r_k, ln_g, ln_b)


def _pool_kernel(u_ref, prev_ref, next_ref, w_ref, scale_ref, o_ref, ext_ref, *, ctx_tiles, n_tiles, n_ctx, n_lat):
    i = pl.program_id(0)
    first, last = _segment_flags(i, ctx_tiles, n_tiles)
    tm = u_ref.shape[0]
    H = POOL_HALO
    ext_ref[0:H, :] = jnp.where(first, 0.0, prev_ref[...])
    ext_ref[H:H + tm, :] = u_ref[...]
    ext_ref[H + tm:H + tm + H, :] = jnp.where(last, 0.0, next_ref[...])
    is_ctx = i < ctx_tiles
    seg_len = jnp.where(is_ctx, n_ctx, n_lat)
    t = lax.broadcasted_iota(jnp.int32, (tm, 1), 0) + i * tm - jnp.where(is_ctx, 0, n_ctx)
    for gi, win in enumerate(POOL_WINDOWS):
        cols = slice(gi * POOL_GROUP_W, (gi + 1) * POOL_GROUP_W)
        acc = None
        for o in range(-(win // 2), win // 2):
            term = ext_ref[H + o:H + o + tm, cols]
            acc = term if acc is None else acc + term
        lo = jnp.maximum(t - win // 2, 0)
        hi = jnp.minimum(t + win // 2 - 1, seg_len - 1)
        cnt = (hi - lo + 1).astype(F32)
        pooled = acc / cnt - u_ref[:, cols]
        y = _dot(pooled.astype(BF16), w_ref[gi]) * scale_ref[:, cols]
        o_ref[:, cols] = y.astype(o_ref.dtype)


def _pool(p, pool_w, pool_scale, *, n_ctx):
    R = p.shape[0]
    tm = _tile(math.gcd(n_ctx, R - n_ctx), 256, 16)
    n_tiles = R // tm
    hb = tm // POOL_HALO
    n_hblocks = R // POOL_HALO
    cb = P_B // B_WIDTH
    assert P_B % B_WIDTH == 0
    return pl.pallas_call(
        functools.partial(_pool_kernel, ctx_tiles=n_ctx // tm, n_tiles=n_tiles, n_ctx=n_ctx, n_lat=R - n_ctx),
        grid=(n_tiles,),
        in_specs=[pl.BlockSpec((tm, B_WIDTH), lambda i: (i, cb)),
                  pl.BlockSpec((POOL_HALO, B_WIDTH), lambda i: (jnp.maximum(i * hb - 1, 0), cb)),
                  pl.BlockSpec((POOL_HALO, B_WIDTH), lambda i: (jnp.minimum((i + 1) * hb, n_hblocks - 1), cb)),
                  pl.BlockSpec((len(POOL_WINDOWS), POOL_GROUP_W, POOL_GROUP_W), lambda i: (0, 0, 0)),
                  pl.BlockSpec((1, B_WIDTH), lambda i: (0, 0))],
        out_specs=pl.BlockSpec((tm, B_WIDTH), lambda i: (i, 0)),
        out_shape=jax.ShapeDtypeStruct((R, B_WIDTH), BF16),
        scratch_shapes=[pltpu.VMEM((tm + 2 * POOL_HALO, B_WIDTH), F32)],
        compiler_params=_params("parallel"),
        name="pool",
    )(p, p, p, pool_w, pool_scale)


def _rope(x, cos, sin):
    return x * cos + pltpu.roll(x, C_HEAD_DIM // 2, axis=1) * sin


def _rope_tables(n_ctx, n_lat):
    half = C_HEAD_DIM // 2
    t = jnp.arange(n_lat)
    row = (t // GRID_W).astype(F32)
    col = (t % GRID_W).astype(F32)
    inv = ROPE_BASE ** (-jnp.arange(0, half, 2, dtype=F32) / half)
    ar = row[:, None] * inv[None]
    ac = col[:, None] * inv[None]
    cos = jnp.concatenate([jnp.cos(ar), jnp.cos(ac), jnp.cos(ar), jnp.cos(ac)], axis=1)
    sin = jnp.concatenate([-jnp.sin(ar), -jnp.sin(ac), jnp.sin(ar), jnp.sin(ac)], axis=1)
    cos = jnp.concatenate([jnp.ones((n_ctx, C_HEAD_DIM), F32), cos], axis=0)
    sin = jnp.concatenate([jnp.zeros((n_ctx, C_HEAD_DIM), F32), sin], axis=0)
    return cos, sin


def _attn_kernel(q0_ref, q1_ref, q2_ref, q3_ref, kp_ref, kc_ref, kn_ref, kx_ref, vp_ref, vc_ref, vn_ref, vx_ref,
                 cp_ref, sp_ref, co_ref, so_ref, cn_ref, sn_ref, sink_ref, o_ref, *, ctx_qblocks, n_qblocks):
    i = pl.program_id(0)
    B = ATTN_BLOCK
    q_refs = (q0_ref, q1_ref, q2_ref, q3_ref)
    tq = q0_ref.shape[0]
    cos_o = co_ref[...]
    sin_o = so_ref[...]
    nloc = tq + 2 * B
    qrow = lax.broadcasted_iota(jnp.int32, (tq, nloc), 0)
    kcol = lax.broadcasted_iota(jnp.int32, (tq, nloc), 1)
    rel = kcol - B - qrow
    lo = jnp.where(i == ctx_qblocks, B, 0)
    hi = jnp.where(i < ctx_qblocks, 0, jnp.where(i == n_qblocks - 1, B + tq, nloc))
    bias = jnp.where(jnp.abs(rel) <= ATTN_BLOCK, 0.0, NEG_INF)
    bias = jnp.where(kcol >= lo, bias, NEG_INF)
    bias = jnp.where(kcol < hi, bias, NEG_INF)
    scale = C_HEAD_DIM ** -0.5
    for h in range(C_KV_HEADS):
        kc = slice(h * C_HEAD_DIM, (h + 1) * C_HEAD_DIM)
        k_loc = jnp.concatenate([_rope(kp_ref[:, kc], cp_ref[...], sp_ref[...]).astype(BF16),
                                 _rope(kc_ref[:, kc], cos_o, sin_o).astype(BF16),
                                 _rope(kn_ref[:, kc], cn_ref[...], sn_ref[...]).astype(BF16)], axis=0)
        v_loc = jnp.concatenate([vp_ref[:, kc].astype(BF16), vc_ref[:, kc].astype(BF16),
                                 vn_ref[:, kc].astype(BF16)], axis=0)
        k_ctx = kx_ref[:, kc].astype(BF16)
        v_ctx = vx_ref[:, kc].astype(BF16)
        for g in range(C_GROUP):
            j = h * C_GROUP + g
            cols = slice(j * C_HEAD_DIM, (j + 1) * C_HEAD_DIM)
            q = _rope(q_refs[h][:, g * C_HEAD_DIM:(g + 1) * C_HEAD_DIM], cos_o, sin_o).astype(BF16)
            s_loc = _dot_nt(q, k_loc) * scale + bias
            s_ctx = _dot_nt(q, k_ctx) * scale
            sk = sink_ref[j:j + 1, 0:1]
            m = jnp.maximum(jnp.maximum(jnp.max(s_loc, axis=-1, keepdims=True),
                                        jnp.max(s_ctx, axis=-1, keepdims=True)), sk)
            e_loc = jnp.exp(s_loc - m)
            e_ctx = jnp.exp(s_ctx - m)
            denom = (jnp.sum(e_loc, axis=-1, keepdims=True) + jnp.sum(e_ctx, axis=-1, keepdims=True)
                     + jnp.exp(sk - m))
            o = _dot(e_loc.astype(BF16), v_loc) + _dot(e_ctx.astype(BF16), v_ctx)
            o_ref[:, cols] = (o / denom).astype(o_ref.dtype)


def _attention(p, cos_tab, sin_tab, sink16, *, n_ctx):
    R = p.shape[0]
    B = ATTN_BLOCK
    tq = _tile(math.gcd(n_ctx, R - n_ctx), 2 * B, B)
    per = tq // B
    n_qblocks = R // tq
    n_blocks = R // B
    ctx_blocks = n_ctx // B
    W = C_KV_WIDTH
    assert n_ctx % B == 0 and R % B == 0 and P_Q % W == 0 and P_K % W == 0 and P_V % W == 0
    assert C_GROUP * C_HEAD_DIM == W
    kcol, vcol = P_K // W, P_V // W

    def prev_rows(i):
        return jnp.clip(i * per - 1, ctx_blocks, n_blocks - 1)

    def next_rows(i):
        return jnp.clip((i + 1) * per, ctx_blocks, n_blocks - 1)

    def kv_specs(c):
        return [pl.BlockSpec((B, W), lambda i: (prev_rows(i), c)), pl.BlockSpec((tq, W), lambda i: (i, c)),
                pl.BlockSpec((B, W), lambda i: (next_rows(i), c)), pl.BlockSpec((n_ctx, W), lambda i: (0, c))]

    tab_specs = []
    for rows, fn in ((B, prev_rows), (tq, lambda i: i), (B, next_rows)):
        tab_specs += [pl.BlockSpec((rows, C_HEAD_DIM), lambda i, fn=fn: (fn(i), 0))] * 2
    q_specs = [pl.BlockSpec((tq, W), lambda i, h=h: (i, P_Q // W + h)) for h in range(C_KV_HEADS)]

    return pl.pallas_call(
        functools.partial(_attn_kernel, ctx_qblocks=n_ctx // tq, n_qblocks=n_qblocks),
        grid=(n_qblocks,),
        in_specs=q_specs + kv_specs(kcol) + kv_specs(vcol) + tab_specs
        + [pl.BlockSpec((C_Q_HEADS, LANES), lambda i: (0, 0))],
        out_specs=pl.BlockSpec((tq, C_WIDTH), lambda i: (i, 0)),
        out_shape=jax.ShapeDtypeStruct((R, C_WIDTH), BF16),
        compiler_params=_params("parallel"),
        name="window_attention",
    )(*([p] * 12), cos_tab, sin_tab, cos_tab, sin_tab, cos_tab, sin_tab, sink16)


def _pad_to(a, axis, size):
    pad = [(0, 0)] * a.ndim
    pad[axis] = (0, size - a.shape[axis])
    return jnp.pad(a, pad)


def _relayout_w_in(w_in):
    seg = lambda lo, hi: w_in[..., lo:hi]

    def rope_order(w):
        lead = w.shape[:-1]
        w = w.reshape(*lead, -1, 2, 2, C_HEAD_DIM // 4)
        return jnp.swapaxes(w, -3, -2).reshape(*lead, -1)

    return jnp.concatenate([
        _pad_to(seg(0, OFF_B), -1, A_PAD), rope_order(seg(OFF_K, OFF_V)), seg(OFF_V, OFF_G), seg(OFF_B, OFF_Q),
        rope_order(seg(OFF_Q, OFF_K)), seg(OFF_G, IN_COLS)], axis=-1).astype(BF16)


def _low_rank_pair(up):
    z = jnp.zeros_like(up[:, 0])
    return jnp.stack([jnp.concatenate([up[:, 0], z], axis=1), jnp.concatenate([z, up[:, 1]], axis=1)], axis=1).astype(BF16)


def kernel(x, c, ctx, c_ctx, mod_down, mod_up, mod_b, norm_g, w_in, shift_mu, rwkv_w0, rwkv_w_up, rwkv_a0,
           rwkv_a_up, rwkv_g_up, rwkv_k_k, rwkv_k_a, rwkv_r_k, rwkv_ln_g, rwkv_ln_b, pool_w, pool_scale,
           attn_sink, gate_up, w_branch_a, w_branch_b, w_branch_c, w_out, ffn_w1, ffn_w3, ffn_w2):
    assert x.shape[0] == 1 and ctx.shape[0] == 1 and c.shape[0] == 1
    depth = w_in.shape[0]
    T, D = x.shape[1], x.shape[2]
    L = ctx.shape[1]
    assert 2 * A_DECAY_RANK == LANES and 2 * A_ICLR_RANK == LANES

    w_in_p = _relayout_w_in(w_in)
    mu_p = _pad_to(shift_mu, -1, A_PAD)[:, None, :]
    wup2 = _low_rank_pair(rwkv_w_up)
    aup2 = _low_rank_pair(rwkv_a_up)
    gup_p = _pad_to(rwkv_g_up, 1, A_GD_PAD).astype(BF16)
    pool_w_b = pool_w.astype(BF16)
    gate_up_b = gate_up.astype(BF16)
    wa_b, wb_b, wc_b = (w.astype(BF16) for w in (w_branch_a, w_branch_b, w_branch_c))
    w_out_b = w_out.astype(BF16)
    w1_b, w3_b, w2_b = (w.astype(BF16) for w in (ffn_w1, ffn_w3, ffn_w2))
    sink16 = jnp.broadcast_to(attn_sink[..., None], (depth, C_Q_HEADS, LANES))
    cos_tab, sin_tab = _rope_tables(L, T)

    c8 = _pad_to(jnp.concatenate([c_ctx[None], c], axis=0), 0, SUBLANES)
    mod = _modulation(c8, mod_down.astype(BF16), mod_up.astype(BF16), mod_b).reshape(depth, SUBLANES, 6, D)

    def mod6(l, shift_i, scale_i, gate_i):
        m = mod[l]
        return jnp.stack([m[0, gate_i], m[1, gate_i], m[0, scale_i], m[1, scale_i], m[0, shift_i], m[1, shift_i]])

    xs = jnp.concatenate([ctx[0], x[0]], axis=0)
    _, h = _resid_norm(xs, None, jnp.stack([norm_g[0, 0], norm_g[0, 0]]), mod6(0, 0, 1, 2), n_ctx=L, emit_h=True)
    for l in range(depth):
        p = _matmul(h, w_in_p, l, tn_cap=768, name="w_in")
        r, v, kap, lw0, lw1, k0, k1, b0, b1, gs = _rwkv_prep(
            p, mu_p[l], rwkv_w0[l], wup2[l], rwkv_a0[l], aup2[l], rwkv_k_k[l][None], rwkv_k_a[l][None], n_ctx=L)
        yf = _delta_scan(lw0, k0, b0, kap, v, r, n_ctx=L, reverse=False)
        yr = _delta_scan(lw1, k1, b1, kap, v, r, n_ctx=L, reverse=True)
        y_a = _rwkv_readout(yf, yr, r, v, k0, k1, gs, gup_p[l], rwkv_r_k[l][None], rwkv_ln_g[l][None],
                            rwkv_ln_b[l][None])
        y_b = _pool(p, pool_w_b[l], pool_scale[l][None], n_ctx=L)
        y_c = _attention(p, cos_tab, sin_tab, sink16[l], n_ctx=L)
        acc = _merge(p, y_a, y_b, y_c, gate_up_b, wa_b, wb_b, wc_b, l)
        mix = _matmul(acc, w_out_b, l, out_dtype=BF16, name="w_out")
        xs, h2 = _resid_norm(xs, mix, norm_g[l, 1:3], mod6(l, 3, 4, 2), n_ctx=L, emit_h=True)
        f = _matmul(_ffn_up(h2, w1_b, w3_b, l), w2_b, l, out_dtype=BF16, tk_cap=5504, name="ffn_down")
        if l + 1 < depth:
            g2 = jnp.stack([norm_g[l, 3], norm_g[l + 1, 0]])
            m6 = jnp.concatenate([mod6(l, 0, 1, 5)[:2], mod6(l + 1, 0, 1, 2)[2:]], axis=0)
            xs, h = _resid_norm(xs, f, g2, m6, n_ctx=L, emit_h=True)
        else:
            xs, _ = _resid_norm(xs, f, jnp.stack([norm_g[l, 3], norm_g[l, 3]]), mod6(l, 0, 1, 5), n_ctx=L,
                                emit_h=False, latent_only=True)
    return xs[None]
```

```python
import functools
import math

import jax
import jax.numpy as jnp
from jax import lax
from jax.experimental import pallas as pl
from jax.experimental.pallas import tpu as pltpu

F32 = jnp.float32
BF16 = jnp.bfloat16

LANES = 128
SUBLANES = 8
VMEM_LIMIT = 56 * 1024 * 1024

NORM_EPS = 1e-6
GRID_W = 64
ROPE_BASE = 10000.0
NEG_INF = -1e30

A_HEADS = 24
A_HEAD_DIM = 64
A_WIDTH = A_HEADS * A_HEAD_DIM
A_DECAY_RANK = 64
A_ICLR_RANK = 64
A_GATE_RANK = 224
A_GN_EPS = 64e-5
A_COLS = 3 * A_WIDTH + 2 * A_DECAY_RANK + 2 * A_ICLR_RANK + A_GATE_RANK
SCAN_CHUNK = 64
POOL_WINDOWS = (2, 4, 8, 16)
POOL_GROUP_W = 384
B_WIDTH = len(POOL_WINDOWS) * POOL_GROUP_W
POOL_HALO = 8
C_Q_HEADS = 16
C_KV_HEADS = 4
C_GROUP = C_Q_HEADS // C_KV_HEADS
C_HEAD_DIM = 128
C_WIDTH = C_Q_HEADS * C_HEAD_DIM
C_KV_WIDTH = C_KV_HEADS * C_HEAD_DIM
ATTN_BLOCK = 128
GATE_RANK = 256
N_BRANCH = 3

OFF_B = A_COLS
OFF_Q = OFF_B + B_WIDTH
OFF_K = OFF_Q + C_WIDTH
OFF_V = OFF_K + C_KV_WIDTH
OFF_G = OFF_V + C_KV_WIDTH
IN_COLS = OFF_G + GATE_RANK
A_PAD = 5120
A_GD_OFF = 3 * A_WIDTH + 2 * A_DECAY_RANK + 2 * A_ICLR_RANK
A_GD_PAD = A_PAD - A_GD_OFF
P_K = A_PAD
P_V = P_K + C_KV_WIDTH
P_B = P_V + C_KV_WIDTH
P_Q = P_B + B_WIDTH
P_G = P_Q + C_WIDTH
P_COLS = P_G + GATE_RANK


def _dot(a, b):
    return jnp.dot(a, b, preferred_element_type=F32)


def _dot_nt(a, b):
    return lax.dot_general(a, b, (((1,), (1,)), ((), ())), preferred_element_type=F32)


def _dot_tn(a, b):
    return lax.dot_general(a, b, (((0,), (0,)), ((), ())), preferred_element_type=F32)


def _tile(n, cap, mult):
    best = None
    for t in range(mult, min(n, cap) + 1, mult):
        if n % t == 0:
            best = t
    assert best is not None, (n, cap, mult)
    return best


def _params(*sem):
    return pltpu.CompilerParams(dimension_semantics=sem, vmem_limit_bytes=VMEM_LIMIT)


def _sigmoid(x):
    return 1.0 / (1.0 + jnp.exp(-x))


def _silu(x):
    return x * _sigmoid(x)


def _split3(x):
    hi = x.astype(BF16)
    rem = x - hi.astype(F32)
    mid = rem.astype(BF16)
    lo = (rem - mid.astype(F32)).astype(BF16)
    return hi, mid, lo


def _head_block_ones():
    r = lax.broadcasted_iota(jnp.int32, (3 * LANES, LANES), 0)
    c = lax.broadcasted_iota(jnp.int32, (3 * LANES, LANES), 1)
    return jnp.where(((r % LANES) // A_HEAD_DIM) == (c // A_HEAD_DIM), 1.0, 0.0).astype(BF16)


def _head_sums(x, ones_bd):
    out = []
    for j in range(0, x.shape[1], LANES):
        out.append(_dot(jnp.concatenate(_split3(x[:, j:j + LANES]), axis=1), ones_bd))
    return jnp.concatenate(out, axis=1)


def _mod_kernel(c_ref, down_ref, up_ref, b_ref, o_ref):
    s = _silu(c_ref[...]).astype(BF16)
    low = _dot(s, down_ref[0]).astype(BF16)
    o_ref[0] = _dot(low, up_ref[0]) + b_ref[0]


def _modulation(c8, down, up, bias):
    depth, d, rank = down.shape
    n = up.shape[2]
    return pl.pallas_call(
        _mod_kernel,
        grid=(depth,),
        in_specs=[pl.BlockSpec((SUBLANES, d), lambda l: (0, 0)),
                  pl.BlockSpec((1, d, rank), lambda l: (l, 0, 0)),
                  pl.BlockSpec((1, rank, n), lambda l: (l, 0, 0)),
                  pl.BlockSpec((1, 1, n), lambda l: (l, 0, 0))],
        out_specs=pl.BlockSpec((1, SUBLANES, n), lambda l: (l, 0, 0)),
        out_shape=jax.ShapeDtypeStruct((depth, SUBLANES, n), F32),
        compiler_params=_params("arbitrary"),
        name="modulation",
    )(c8, down, up, bias.reshape(depth, 1, n))


def _rms(x, g):
    return x * lax.rsqrt(jnp.mean(x * x, axis=-1, keepdims=True) + NORM_EPS) * g


def _resid_norm_kernel(*refs, ctx_tiles, has_m, emit_h):
    refs = list(refs)
    x_ref = refs.pop(0)
    m_ref = refs.pop(0) if has_m else None
    g_ref = refs.pop(0)
    mod_ref = refs.pop(0)
    is_ctx = pl.program_id(0) < ctx_tiles

    def pick(i):
        return jnp.where(is_ctx, mod_ref[i:i + 1, :], mod_ref[i + 1:i + 2, :])

    x = x_ref[...]
    if has_m:
        x = x + pick(0) * _rms(m_ref[...].astype(F32), g_ref[0:1, :])
        refs.pop(0)[...] = x
    if emit_h:
        h = _rms(x, g_ref[1:2, :]) * (1.0 + pick(2)) + pick(4)
        refs.pop(0)[...] = h.astype(BF16)


def _resid_norm(x, m, g2, mod6, *, n_ctx, emit_h, latent_only=False):
    R, D = x.shape
    te = _tile(math.gcd(n_ctx, R - n_ctx), 256, 16)
    has_m = m is not None
    skip = n_ctx // te if latent_only else 0
    n_out = R - skip * te
    row_in = pl.BlockSpec((te, D), lambda i: (i + skip, 0))
    row = pl.BlockSpec((te, D), lambda i: (i, 0))
    ins = [x] + ([m] if has_m else []) + [g2, mod6]
    in_specs = [row_in] * (2 if has_m else 1) + [pl.BlockSpec((2, D), lambda i: (0, 0)),
                                                 pl.BlockSpec((6, D), lambda i: (0, 0))]
    out_shape, out_specs = [], []
    if has_m:
        out_shape.append(jax.ShapeDtypeStruct((n_out, D), F32))
        out_specs.append(row)
    if emit_h:
        out_shape.append(jax.ShapeDtypeStruct((n_out, D), BF16))
        out_specs.append(row)
    outs = pl.pallas_call(
        functools.partial(_resid_norm_kernel, ctx_tiles=n_ctx // te - skip, has_m=has_m, emit_h=emit_h),
        grid=(n_out // te,),
        in_specs=in_specs, out_specs=out_specs, out_shape=out_shape,
        compiler_params=_params("parallel"),
        name="resid_norm",
    )(*ins)
    outs = list(outs)
    x1 = outs.pop(0) if has_m else None
    h = outs.pop(0) if emit_h else None
    return x1, h


def _mm_kernel(x_ref, w_ref, o_ref, *scratch, nk):
    part = _dot(x_ref[...].astype(BF16), w_ref[...])
    if nk == 1:
        o_ref[...] = part.astype(o_ref.dtype)
        return
    acc_ref, = scratch
    k = pl.program_id(2)

    @pl.when(k == 0)
    def _():
        acc_ref[...] = part

    @pl.when(k > 0)
    def _():
        acc_ref[...] += part

    @pl.when(k == nk - 1)
    def _():
        o_ref[...] = acc_ref[...].astype(o_ref.dtype)


def _matmul(x, w, layer, *, out_dtype=F32, tm_cap=1280, tn_cap=512, tk_cap=4096, name="matmul"):
    M, K = x.shape
    N = w.shape[2]
    tm = _tile(M, tm_cap, 16)
    tn = _tile(N, tn_cap, LANES)
    tk = _tile(K, tk_cap, LANES)
    nk = K // tk
    return pl.pallas_call(
        functools.partial(_mm_kernel, nk=nk),
        grid=(M // tm, N // tn, nk),
        in_specs=[pl.BlockSpec((tm, tk), lambda i, j, k: (i, k)),
                  pl.BlockSpec((None, tk, tn), lambda i, j, k: (layer, k, j))],
        out_specs=pl.BlockSpec((tm, tn), lambda i, j, k: (i, j)),
        out_shape=jax.ShapeDtypeStruct((M, N), out_dtype),
        scratch_shapes=[pltpu.VMEM((tm, tn), F32)] if nk > 1 else [],
        compiler_params=_params("parallel", "parallel", "arbitrary"),
        name=name,
    )(x, w)


def _ffn_up_kernel(h_ref, w1_ref, w3_ref, o_ref):
    h = h_ref[...]
    o_ref[...] = (_silu(_dot(h, w1_ref[...].astype(BF16))) * _dot(h, w3_ref[...].astype(BF16))).astype(o_ref.dtype)


def _ffn_up(h, w1, w3, layer):
    M, K = h.shape
    N = w1.shape[2]
    tm = _tile(M, 1280, 16)
    tn = _tile(N, 256, LANES)
    wspec = pl.BlockSpec((None, K, tn), lambda i, j: (layer, 0, j))
    return pl.pallas_call(
        _ffn_up_kernel,
        grid=(M // tm, N // tn),
        in_specs=[pl.BlockSpec((tm, K), lambda i, j: (i, 0)), wspec, wspec],
        out_specs=pl.BlockSpec((tm, tn), lambda i, j: (i, j)),
        out_shape=jax.ShapeDtypeStruct((M, N), BF16),
        compiler_params=_params("parallel", "parallel"),
        name="ffn_up",
    )(h, w1, w3)


def _merge_kernel(pg_ref, ya_ref, yb_ref, yc_ref, ga_ref, gb_ref, gc_ref, wa_ref, wb_ref, wc_ref, o_ref):
    pg = pg_ref[...].astype(BF16)
    acc = _sigmoid(_dot(pg, ga_ref[...])) * _dot(ya_ref[...], wa_ref[...])
    acc += _sigmoid(_dot(pg, gb_ref[...])) * _dot(yb_ref[...], wb_ref[...])
    acc += _sigmoid(_dot(pg, gc_ref[...])) * _dot(yc_ref[...], wc_ref[...])
    o_ref[...] = acc.astype(o_ref.dtype)


def _merge(p, ya, yb, yc, gate_up, wa, wb, wc, layer):
    R = p.shape[0]
    D = wa.shape[2]
    tm = _tile(R, 1280, 16)
    tn = _tile(D, 512, LANES)
    nj = D // tn

    def rows(width):
        return pl.BlockSpec((tm, width), lambda i, j: (i, 0))

    def gate(branch):
        return pl.BlockSpec((None, GATE_RANK, tn), lambda i, j: (layer, 0, branch * nj + j))

    def wcol(width):
        return pl.BlockSpec((None, width, tn), lambda i, j: (layer, 0, j))

    return pl.pallas_call(
        _merge_kernel,
        grid=(R // tm, nj),
        in_specs=[pl.BlockSpec((tm, GATE_RANK), lambda i, j: (i, P_G // GATE_RANK)),
                  rows(A_WIDTH), rows(B_WIDTH), rows(C_WIDTH),
                  gate(0), gate(1), gate(2), wcol(A_WIDTH), wcol(B_WIDTH), wcol(C_WIDTH)],
        out_specs=pl.BlockSpec((tm, tn), lambda i, j: (i, j)),
        out_shape=jax.ShapeDtypeStruct((R, D), BF16),
        compiler_params=_params("parallel", "parallel"),
        name="merge",
    )(p, ya, yb, yc, gate_up, gate_up, gate_up, wa, wb, wc)


def _segment_flags(i, ctx_tiles, n_tiles):
    first = jnp.logical_or(i == 0, i == ctx_tiles)
    last = jnp.logical_or(i == ctx_tiles - 1, i == n_tiles - 1)
    return first, last


def _rwkv_prep_kernel(u_ref, prev_ref, next_ref, mu_ref, w0_ref, wup_ref, a0_ref, aup_ref, kk_ref, ka_ref,
                      r_out, v_out, kap_out, lw0_out, lw1_out, k0_out, k1_out, b0_out, b1_out, gs_out,
                      *, ctx_tiles, n_tiles):
    i = pl.program_id(0)
    first, last = _segment_flags(i, ctx_tiles, n_tiles)
    u = u_ref[...]
    tm = u.shape[0]
    rid = lax.broadcasted_iota(jnp.int32, (tm, 1), 0)
    prev_row = jnp.where(first, 0.0, prev_ref[SUBLANES - 1:SUBLANES, :])
    next_row = jnp.where(last, 0.0, next_ref[0:1, :])
    prev = jnp.where(rid == 0, prev_row, pltpu.roll(u, 1, axis=0))
    nxt = jnp.where(rid == tm - 1, next_row, pltpu.roll(u, tm - 1, axis=0))
    s = u + mu_ref[...] * (0.5 * (prev + nxt) - u)

    W = A_WIDTH
    r = s[:, 0:W]
    k = s[:, W:2 * W]
    v = s[:, 2 * W:3 * W]
    o = 3 * W
    wd = jnp.tanh(s[:, o:o + LANES]).astype(BF16)
    ad = s[:, o + LANES:o + 2 * LANES].astype(BF16)
    gd = s[:, A_GD_OFF:A_PAD]

    ones_bd = _head_block_ones()
    kk = k * kk_ref[...]
    nrm = jnp.sqrt(_head_sums(kk * kk, ones_bd))
    kk = kk / jnp.maximum(nrm, 1e-12)

    r_out[...] = r.astype(BF16)
    v_out[...] = v.astype(BF16)
    kap_out[...] = kk.astype(BF16)
    gs_out[...] = _sigmoid(gd).astype(BF16)
    for d, (lw_out, k_out, b_out) in enumerate(((lw0_out, k0_out, b0_out), (lw1_out, k1_out, b1_out))):
        z = w0_ref[d:d + 1, :] + _dot(wd, wup_ref[d])
        lw_out[...] = -math.exp(-0.5) * _sigmoid(z)
        a = _sigmoid(a0_ref[d:d + 1, :] + _dot(ad, aup_ref[d]))
        k_out[...] = (k * (1.0 + (a - 1.0) * ka_ref[...])).astype(BF16)
        b_out[...] = (kk * a).astype(BF16)


def _rwkv_prep(p, mu, w0, wup2, a0, aup2, k_k, k_a, *, n_ctx):
    R = p.shape[0]
    tm = _tile(math.gcd(n_ctx, R - n_ctx), 256, 16)
    n_tiles = R // tm
    hb = tm // SUBLANES
    n_hblocks = R // SUBLANES
    W = A_WIDTH

    def const(shape):
        return pl.BlockSpec(shape, lambda i: (0,) * len(shape))

    wide = pl.BlockSpec((tm, W), lambda i: (i, 0))
    f32w = jax.ShapeDtypeStruct((R, W), F32)
    bf16w = jax.ShapeDtypeStruct((R, W), BF16)
    return pl.pallas_call(
        functools.partial(_rwkv_prep_kernel, ctx_tiles=n_ctx // tm, n_tiles=n_tiles),
        grid=(n_tiles,),
        in_specs=[pl.BlockSpec((tm, A_PAD), lambda i: (i, 0)),
                  pl.BlockSpec((SUBLANES, A_PAD), lambda i: (jnp.maximum(i * hb - 1, 0), 0)),
                  pl.BlockSpec((SUBLANES, A_PAD), lambda i: (jnp.minimum((i + 1) * hb, n_hblocks - 1), 0)),
                  const((1, A_PAD)), const((2, W)), const((2, LANES, W)), const((2, W)), const((2, LANES, W)),
                  const((1, W)), const((1, W))],
        out_specs=[wide] * 9 + [pl.BlockSpec((tm, A_GD_PAD), lambda i: (i, 0))],
        out_shape=[bf16w] * 3 + [f32w] * 2 + [bf16w] * 4 + [jax.ShapeDtypeStruct((R, A_GD_PAD), BF16)],
        compiler_params=_params("parallel"),
        name="rwkv_prep",
    )(p, p, p, mu, w0, wup2, a0, aup2, k_k, k_a)


def _scan_kernel(lw_ref, k_ref, b_ref, kap_ref, v_ref, r_ref, y_ref, s_ref, *, reverse, npairs):
    C = SCAN_CHUNK
    N = A_HEAD_DIM

    @pl.when(pl.program_id(1) == 0)
    def _():
        s_ref[...] = jnp.zeros_like(s_ref)

    row = lax.broadcasted_iota(jnp.int32, (C, C), 0)
    col = lax.broadcasted_iota(jnp.int32, (C, C), 1)
    earlier = (col > row) if reverse else (col < row)
    diag = col == row
    incl = jnp.logical_or(earlier, diag)
    tri = jnp.where(incl, 1.0, 0.0).astype(BF16)
    tri3 = jnp.concatenate([tri, tri, tri], axis=1)
    eye = jnp.where(diag, 1.0, 0.0).astype(F32)

    def same_block(n):
        return (row // n) == (col // n)

    last = 0 if reverse else C - 1

    hs = (slice(0, N), slice(N, 2 * N))
    heads = range(2 * npairs)
    x, k_t, b_t, k_e, b_e, e_tot, v = ([] for _ in range(7))
    for p in range(npairs):
        sl = slice(p * LANES, (p + 1) * LANES)
        lw = lw_ref[:, sl]
        c = _dot(tri3, jnp.concatenate(_split3(lw), axis=0))
        ctot = c[last:last + 1, :]
        e_nc = jnp.exp(-c)
        e_tc = jnp.exp(ctot - c)
        kap_p = kap_ref[:, sl].astype(F32) * jnp.exp(c - lw)
        r_p = r_ref[:, sl].astype(F32) * jnp.exp(c)
        k_p = k_ref[:, sl].astype(F32)
        b_p = b_ref[:, sl].astype(F32)
        v_p = v_ref[:, sl]
        e_p = jnp.exp(ctot)
        for h in hs:
            x.append(jnp.concatenate([kap_p[:, h], r_p[:, h]], axis=0).astype(BF16))
            k_t.append((k_p * e_nc)[:, h].astype(BF16))
            b_t.append((b_p * e_nc)[:, h].astype(BF16))
            k_e.append((k_p * e_tc)[:, h].astype(BF16))
            b_e.append((b_p * e_tc)[:, h].astype(BF16))
            e_tot.append(e_p[:, h])
            v.append(v_p[:, h].astype(BF16))

    def bd(a, b):
        return _dot(a.astype(BF16), b.astype(BF16))

    g1 = [_dot_nt(x[i], k_t[i]) for i in heads]
    g2 = [_dot_nt(x[i], b_t[i]) for i in heads]
    a_kk = [jnp.where(earlier, g1[i][:C], 0.0).astype(BF16) for i in heads]
    a_rk = [jnp.where(incl, g1[i][C:], 0.0).astype(BF16) for i in heads]
    a_kb = [jnp.where(earlier, g2[i][:C], 0.0) for i in heads]
    a_rb = [jnp.where(incl, g2[i][C:], 0.0).astype(BF16) for i in heads]
    blk8 = same_block(8)
    a0 = [jnp.where(blk8, a_kb[i], 0.0) for i in heads]
    a2 = [bd(a0[i], a0[i]) for i in heads]
    a4 = [bd(a2[i], a2[i]) for i in heads]
    t = [bd(eye - a0[i], eye + a2[i]) for i in heads]
    t = [bd(t[i], eye + a4[i]) for i in heads]
    for n in (16, 32, 64):
        m = jnp.logical_and(same_block(n), jnp.logical_not(same_block(n // 2)))
        off = [jnp.where(m, a_kb[i], 0.0) for i in heads]
        ot = [bd(off[i], t[i]) for i in heads]
        t = [t[i] - bd(t[i], ot[i]) for i in heads]
    s0 = [s_ref[i] for i in heads]
    xs = [_dot_nt(x[i], s0[i].astype(BF16)) for i in heads]
    akv = [_dot(a_kk[i], v[i]) for i in heads]
    u = [bd(t[i], xs[i][:C] + akv[i]).astype(BF16) for i in heads]
    ys = [xs[i][C:] + _dot(a_rk[i], v[i]) - _dot(a_rb[i], u[i]) for i in heads]
    for i in heads:
        s_ref[i] = s0[i] * e_tot[i] + _dot_tn(v[i], k_e[i]) - _dot_tn(u[i], b_e[i])
    for p in range(npairs):
        y_ref[:, p * LANES:(p + 1) * LANES] = jnp.concatenate([ys[2 * p], ys[2 * p + 1]], axis=1)


def _delta_scan(lw, k, b, kap, v, r, *, n_ctx, reverse, pairs_per_block=None):
    R, W = lw.shape
    C = SCAN_CHUNK
    nchunks = R // C
    ctx_chunks = n_ctx // C
    npairs = W // LANES
    pb = npairs if pairs_per_block is None else pairs_per_block
    assert R % C == 0 and n_ctx % C == 0 and npairs % pb == 0

    if reverse:
        def rows(s):
            return jnp.where(s < ctx_chunks, ctx_chunks - 1 - s, nchunks - 1 - (s - ctx_chunks))
    else:
        def rows(s):
            return s

    spec = pl.BlockSpec((C, pb * LANES), lambda g, s: (rows(s), g))
    return pl.pallas_call(
        functools.partial(_scan_kernel, reverse=reverse, npairs=pb),
        grid=(npairs // pb, nchunks),
        in_specs=[spec] * 6,
        out_specs=spec,
        out_shape=jax.ShapeDtypeStruct((R, W), F32),
        scratch_shapes=[pltpu.VMEM((2 * pb, A_HEAD_DIM, A_HEAD_DIM), F32)],
        compiler_params=_params("parallel", "arbitrary"),
        name="delta_scan_rev" if reverse else "delta_scan_fwd",
    )(lw, k, b, kap, v, r)


def _rwkv_readout_kernel(yf_ref, yb_ref, r_ref, v_ref, k0_ref, k1_ref, gs_ref, gup_ref, rk_ref, lng_ref, lnb_ref,
                         o_ref):
    ones_bd = _head_block_ones()
    inv_n = 1.0 / A_HEAD_DIM
    ro = yf_ref[...] + yb_ref[...]
    mu = _head_sums(ro, ones_bd) * inv_n
    cen = ro - mu
    var = _head_sums(cen * cen, ones_bd) * inv_n
    yn = cen * lax.rsqrt(var + A_GN_EPS) * lng_ref[...] + lnb_ref[...]
    rk = r_ref[...].astype(F32) * (k0_ref[...].astype(F32) + k1_ref[...].astype(F32)) * rk_ref[...]
    bonus = _head_sums(rk, ones_bd) * v_ref[...].astype(F32)
    g = _dot(gs_ref[...], gup_ref[...])
    o_ref[...] = ((yn + bonus) * g).astype(o_ref.dtype)


def _rwkv_readout(yf, yb, r, v, k0, k1, gs, gup, r_k, ln_g, ln_b):
    R, W = yf.shape
    tm = _tile(R, 256, 16)
    wide = pl.BlockSpec((tm, W), lambda i: (i, 0))
    vec = pl.BlockSpec((1, W), lambda i: (0, 0))
    return pl.pallas_call(
        _rwkv_readout_kernel,
        grid=(R // tm,),
        in_specs=[wide] * 6 + [pl.BlockSpec((tm, A_GD_PAD), lambda i: (i, 0)),
                               pl.BlockSpec((A_GD_PAD, W), lambda i: (0, 0)), vec, vec, vec],
        out_specs=wide,
        out_shape=jax.ShapeDtypeStruct((R, W), BF16),
        compiler_params=_params("parallel"),
        name="rwkv_readout",
    )(yf, yb, r, v, k0, k1, gs, gup, r_k, ln_g, ln_b)


def _pool_kernel(u_ref, prev_ref, next_ref, w_ref, scale_ref, o_ref, ext_ref, *, ctx_tiles, n_tiles, n_ctx, n_lat):
    i = pl.program_id(0)
    first, last = _segment_flags(i, ctx_tiles, n_tiles)
    tm = u_ref.shape[0]
    H = POOL_HALO
    ext_ref[0:H, :] = jnp.where(first, 0.0, prev_ref[...])
    ext_ref[H:H + tm, :] = u_ref[...]
    ext_ref[H + tm:H + tm + H, :] = jnp.where(last, 0.0, next_ref[...])
    is_ctx = i < ctx_tiles
    seg_len = jnp.where(is_ctx, n_ctx, n_lat)
    t = lax.broadcasted_iota(jnp.int32, (tm, 1), 0) + i * tm - jnp.where(is_ctx, 0, n_ctx)
    for gi, win in enumerate(POOL_WINDOWS):
        cols = slice(gi * POOL_GROUP_W, (gi + 1) * POOL_GROUP_W)
        acc = None
        for o in range(-(win // 2), win // 2):
            term = ext_ref[H + o:H + o + tm, cols]
            acc = term if acc is None else acc + term
        lo = jnp.maximum(t - win // 2, 0)
        hi = jnp.minimum(t + win // 2 - 1, seg_len - 1)
        cnt = (hi - lo + 1).astype(F32)
        pooled = acc / cnt - u_ref[:, cols]
        y = _dot(pooled.astype(BF16), w_ref[gi]) * scale_ref[:, cols]
        o_ref[:, cols] = y.astype(o_ref.dtype)


def _pool(p, pool_w, pool_scale, *, n_ctx):
    R = p.shape[0]
    tm = _tile(math.gcd(n_ctx, R - n_ctx), 256, 16)
    n_tiles = R // tm
    hb = tm // POOL_HALO
    n_hblocks = R // POOL_HALO
    cb = P_B // B_WIDTH
    assert P_B % B_WIDTH == 0
    return pl.pallas_call(
        functools.partial(_pool_kernel, ctx_tiles=n_ctx // tm, n_tiles=n_tiles, n_ctx=n_ctx, n_lat=R - n_ctx),
        grid=(n_tiles,),
        in_specs=[pl.BlockSpec((tm, B_WIDTH), lambda i: (i, cb)),
                  pl.BlockSpec((POOL_HALO, B_WIDTH), lambda i: (jnp.maximum(i * hb - 1, 0), cb)),
                  pl.BlockSpec((POOL_HALO, B_WIDTH), lambda i: (jnp.minimum((i + 1) * hb, n_hblocks - 1), cb)),
                  pl.BlockSpec((len(POOL_WINDOWS), POOL_GROUP_W, POOL_GROUP_W), lambda i: (0, 0, 0)),
                  pl.BlockSpec((1, B_WIDTH), lambda i: (0, 0))],
        out_specs=pl.BlockSpec((tm, B_WIDTH), lambda i: (i, 0)),
        out_shape=jax.ShapeDtypeStruct((R, B_WIDTH), BF16),
        scratch_shapes=[pltpu.VMEM((tm + 2 * POOL_HALO, B_WIDTH), F32)],
        compiler_params=_params("parallel"),
        name="pool",
    )(p, p, p, pool_w, pool_scale)


def _rope(x, cos, sin):
    return x * cos + pltpu.roll(x, C_HEAD_DIM // 2, axis=1) * sin


def _rope_tables(n_ctx, n_lat):
    half = C_HEAD_DIM // 2
    t = jnp.arange(n_lat)
    row = (t // GRID_W).astype(F32)
    col = (t % GRID_W).astype(F32)
    inv = ROPE_BASE ** (-jnp.arange(0, half, 2, dtype=F32) / half)
    ar = row[:, None] * inv[None]
    ac = col[:, None] * inv[None]
    cos = jnp.concatenate([jnp.cos(ar), jnp.cos(ac), jnp.cos(ar), jnp.cos(ac)], axis=1)
    sin = jnp.concatenate([-jnp.sin(ar), -jnp.sin(ac), jnp.sin(ar), jnp.sin(ac)], axis=1)
    cos = jnp.concatenate([jnp.ones((n_ctx, C_HEAD_DIM), F32), cos], axis=0)
    sin = jnp.concatenate([jnp.zeros((n_ctx, C_HEAD_DIM), F32), sin], axis=0)
    return cos, sin


def _attn_kernel(q0_ref, q1_ref, q2_ref, q3_ref, kp_ref, kc_ref, kn_ref, kx_ref, vp_ref, vc_ref, vn_ref, vx_ref,
                 cp_ref, sp_ref, co_ref, so_ref, cn_ref, sn_ref, sink_ref, o_ref, *, ctx_qblocks, n_qblocks):
    i = pl.program_id(0)
    B = ATTN_BLOCK
    q_refs = (q0_ref, q1_ref, q2_ref, q3_ref)
    tq = q0_ref.shape[0]
    cos_o = co_ref[...]
    sin_o = so_ref[...]
    nloc = tq + 2 * B
    qrow = lax.broadcasted_iota(jnp.int32, (tq, nloc), 0)
    kcol = lax.broadcasted_iota(jnp.int32, (tq, nloc), 1)
    rel = kcol - B - qrow
    lo = jnp.where(i == ctx_qblocks, B, 0)
    hi = jnp.where(i < ctx_qblocks, 0, jnp.where(i == n_qblocks - 1, B + tq, nloc))
    bias = jnp.where(jnp.abs(rel) <= ATTN_BLOCK, 0.0, NEG_INF)
    bias = jnp.where(kcol >= lo, bias, NEG_INF)
    bias = jnp.where(kcol < hi, bias, NEG_INF)
    scale = C_HEAD_DIM ** -0.5
    for h in range(C_KV_HEADS):
        kc = slice(h * C_HEAD_DIM, (h + 1) * C_HEAD_DIM)
        k_loc = jnp.concatenate([_rope(kp_ref[:, kc], cp_ref[...], sp_ref[...]).astype(BF16),
                                 _rope(kc_ref[:, kc], cos_o, sin_o).astype(BF16),
                                 _rope(kn_ref[:, kc], cn_ref[...], sn_ref[...]).astype(BF16)], axis=0)
        v_loc = jnp.concatenate([vp_ref[:, kc].astype(BF16), vc_ref[:, kc].astype(BF16),
                                 vn_ref[:, kc].astype(BF16)], axis=0)
        k_ctx = kx_ref[:, kc].astype(BF16)
        v_ctx = vx_ref[:, kc].astype(BF16)
        for g in range(C_GROUP):
            j = h * C_GROUP + g
            cols = slice(j * C_HEAD_DIM, (j + 1) * C_HEAD_DIM)
            q = _rope(q_refs[h][:, g * C_HEAD_DIM:(g + 1) * C_HEAD_DIM], cos_o, sin_o).astype(BF16)
            s_loc = _dot_nt(q, k_loc) * scale + bias
            s_ctx = _dot_nt(q, k_ctx) * scale
            sk = sink_ref[j:j + 1, 0:1]
            m = jnp.maximum(jnp.maximum(jnp.max(s_loc, axis=-1, keepdims=True),
                                        jnp.max(s_ctx, axis=-1, keepdims=True)), sk)
            e_loc = jnp.exp(s_loc - m)
            e_ctx = jnp.exp(s_ctx - m)
            denom = (jnp.sum(e_loc, axis=-1, keepdims=True) + jnp.sum(e_ctx, axis=-1, keepdims=True)
                     + jnp.exp(sk - m))
            o = _dot(e_loc.astype(BF16), v_loc) + _dot(e_ctx.astype(BF16), v_ctx)
            o_ref[:, cols] = (o / denom).astype(o_ref.dtype)


def _attention(p, cos_tab, sin_tab, sink16, *, n_ctx):
    R = p.shape[0]
    B = ATTN_BLOCK
    tq = _tile(math.gcd(n_ctx, R - n_ctx), 2 * B, B)
    per = tq // B
    n_qblocks = R // tq
    n_blocks = R // B
    ctx_blocks = n_ctx // B
    W = C_KV_WIDTH
    assert n_ctx % B == 0 and R % B == 0 and P_Q % W == 0 and P_K % W == 0 and P_V % W == 0
    assert C_GROUP * C_HEAD_DIM == W
    kcol, vcol = P_K // W, P_V // W

    def prev_rows(i):
        return jnp.clip(i * per - 1, ctx_blocks, n_blocks - 1)

    def next_rows(i):
        return jnp.clip((i + 1) * per, ctx_blocks, n_blocks - 1)

    def kv_specs(c):
        return [pl.BlockSpec((B, W), lambda i: (prev_rows(i), c)), pl.BlockSpec((tq, W), lambda i: (i, c)),
                pl.BlockSpec((B, W), lambda i: (next_rows(i), c)), pl.BlockSpec((n_ctx, W), lambda i: (0, c))]

    tab_specs = []
    for rows, fn in ((B, prev_rows), (tq, lambda i: i), (B, next_rows)):
        tab_specs += [pl.BlockSpec((rows, C_HEAD_DIM), lambda i, fn=fn: (fn(i), 0))] * 2
    q_specs = [pl.BlockSpec((tq, W), lambda i, h=h: (i, P_Q // W + h)) for h in range(C_KV_HEADS)]

    return pl.pallas_call(
        functools.partial(_attn_kernel, ctx_qblocks=n_ctx // tq, n_qblocks=n_qblocks),
        grid=(n_qblocks,),
        in_specs=q_specs + kv_specs(kcol) + kv_specs(vcol) + tab_specs
        + [pl.BlockSpec((C_Q_HEADS, LANES), lambda i: (0, 0))],
        out_specs=pl.BlockSpec((tq, C_WIDTH), lambda i: (i, 0)),
        out_shape=jax.ShapeDtypeStruct((R, C_WIDTH), BF16),
        compiler_params=_params("parallel"),
        name="window_attention",
    )(*([p] * 12), cos_tab, sin_tab, cos_tab, sin_tab, cos_tab, sin_tab, sink16)


def _pad_to(a, axis, size):
    pad = [(0, 0)] * a.ndim
    pad[axis] = (0, size - a.shape[axis])
    return jnp.pad(a, pad)


def _relayout_w_in(w_in):
    w_in = w_in.astype(BF16)
    seg = lambda lo, hi: w_in[..., lo:hi]

    def rope_order(w):
        lead = w.shape[:-1]
        w = w.reshape(*lead, -1, 2, 2, C_HEAD_DIM // 4)
        return jnp.swapaxes(w, -3, -2).reshape(*lead, -1)

    return jnp.concatenate([
        _pad_to(seg(0, OFF_B), -1, A_PAD), rope_order(seg(OFF_K, OFF_V)), seg(OFF_V, OFF_G), seg(OFF_B, OFF_Q),
        rope_order(seg(OFF_Q, OFF_K)), seg(OFF_G, IN_COLS)], axis=-1)


def _low_rank_pair(up):
    z = jnp.zeros_like(up[:, 0])
    return jnp.stack([jnp.concatenate([up[:, 0], z], axis=1), jnp.concatenate([z, up[:, 1]], axis=1)], axis=1).astype(BF16)


def kernel(x, c, ctx, c_ctx, mod_down, mod_up, mod_b, norm_g, w_in, shift_mu, rwkv_w0, rwkv_w_up, rwkv_a0,
           rwkv_a_up, rwkv_g_up, rwkv_k_k, rwkv_k_a, rwkv_r_k, rwkv_ln_g, rwkv_ln_b, pool_w, pool_scale,
           attn_sink, gate_up, w_branch_a, w_branch_b, w_branch_c, w_out, ffn_w1, ffn_w3, ffn_w2):
    assert x.shape[0] == 1 and ctx.shape[0] == 1 and c.shape[0] == 1
    depth = w_in.shape[0]
    T, D = x.shape[1], x.shape[2]
    L = ctx.shape[1]
    assert 2 * A_DECAY_RANK == LANES and 2 * A_ICLR_RANK == LANES

    w_in_p = _relayout_w_in(w_in)
    mu_p = _pad_to(shift_mu, -1, A_PAD)[:, None, :]
    wup2 = _low_rank_pair(rwkv_w_up)
    aup2 = _low_rank_pair(rwkv_a_up)
    gup_p = _pad_to(rwkv_g_up, 1, A_GD_PAD).astype(BF16)
    pool_w_b = pool_w.astype(BF16)
    gate_up_b = gate_up.astype(BF16)
    wa_b, wb_b, wc_b = (w.astype(BF16) for w in (w_branch_a, w_branch_b, w_branch_c))
    w_out_b = w_out.astype(BF16)
    w2_b = ffn_w2.astype(BF16)
    sink16 = jnp.broadcast_to(attn_sink[..., None], (depth, C_Q_HEADS, LANES))
    cos_tab, sin_tab = _rope_tables(L, T)

    c8 = _pad_to(jnp.concatenate([c_ctx[None], c], axis=0), 0, SUBLANES)
    mod = _modulation(c8, mod_down.astype(BF16), mod_up.astype(BF16), mod_b).reshape(depth, SUBLANES, 6, D)

    def mod6(l, shift_i, scale_i, gate_i):
        m = mod[l]
        return jnp.stack([m[0, gate_i], m[1, gate_i], m[0, scale_i], m[1, scale_i], m[0, shift_i], m[1, shift_i]])

    xs = jnp.concatenate([ctx[0], x[0]], axis=0)
    _, h = _resid_norm(xs, None, jnp.stack([norm_g[0, 0], norm_g[0, 0]]), mod6(0, 0, 1, 2), n_ctx=L, emit_h=True)
    for l in range(depth):
        p = _matmul(h, w_in_p, l, tn_cap=768, name="w_in")
        r, v, kap, lw0, lw1, k0, k1, b0, b1, gs = _rwkv_prep(
            p, mu_p[l], rwkv_w0[l], wup2[l], rwkv_a0[l], aup2[l], rwkv_k_k[l][None], rwkv_k_a[l][None], n_ctx=L)
        yf = _delta_scan(lw0, k0, b0, kap, v, r, n_ctx=L, reverse=False)
        yr = _delta_scan(lw1, k1, b1, kap, v, r, n_ctx=L, reverse=True)
        y_a = _rwkv_readout(yf, yr, r, v, k0, k1, gs, gup_p[l], rwkv_r_k[l][None], rwkv_ln_g[l][None],
                            rwkv_ln_b[l][None])
        y_b = _pool(p, pool_w_b[l], pool_scale[l][None], n_ctx=L)
        y_c = _attention(p, cos_tab, sin_tab, sink16[l], n_ctx=L)
        acc = _merge(p, y_a, y_b, y_c, gate_up_b, wa_b, wb_b, wc_b, l)
        mix = _matmul(acc, w_out_b, l, out_dtype=BF16, name="w_out")
        xs, h2 = _resid_norm(xs, mix, norm_g[l, 1:3], mod6(l, 3, 4, 2), n_ctx=L, emit_h=True)
        f = _matmul(_ffn_up(h2, ffn_w1, ffn_w3, l), w2_b, l, out_dtype=BF16, tk_cap=5504, name="ffn_down")
        if l + 1 < depth:
            g2 = jnp.stack([norm_g[l, 3], norm_g[l + 1, 0]])
            m6 = jnp.concatenate([mod6(l, 0, 1, 5)[:2], mod6(l + 1, 0, 1, 2)[2:]], axis=0)
            xs, h = _resid_norm(xs, f, g2, m6, n_ctx=L, emit_h=True)
        else:
            xs, _ = _resid_norm(xs, f, jnp.stack([norm_g[l, 3], norm_g[l, 3]]), mod6(l, 0, 1, 5), n_ctx=L,
                                emit_h=False, latent_only=True)
    return xs[None]
```

```python
import functools
import math

import jax
import jax.numpy as jnp
from jax import lax
from jax.experimental import pallas as pl
from jax.experimental.pallas import tpu as pltpu

F32 = jnp.float32
BF16 = jnp.bfloat16

LANES = 128
SUBLANES = 8
VMEM_LIMIT = 56 * 1024 * 1024

NORM_EPS = 1e-6
GRID_W = 64
ROPE_BASE = 10000.0
NEG_INF = -1e30

A_HEADS = 24
A_HEAD_DIM = 64
A_WIDTH = A_HEADS * A_HEAD_DIM
A_DECAY_RANK = 64
A_ICLR_RANK = 64
A_GATE_RANK = 224
A_GN_EPS = 64e-5
A_COLS = 3 * A_WIDTH + 2 * A_DECAY_RANK + 2 * A_ICLR_RANK + A_GATE_RANK
SCAN_CHUNK = 64
POOL_WINDOWS = (2, 4, 8, 16)
POOL_GROUP_W = 384
B_WIDTH = len(POOL_WINDOWS) * POOL_GROUP_W
POOL_HALO = 8
C_Q_HEADS = 16
C_KV_HEADS = 4
C_GROUP = C_Q_HEADS // C_KV_HEADS
C_HEAD_DIM = 128
C_WIDTH = C_Q_HEADS * C_HEAD_DIM
C_KV_WIDTH = C_KV_HEADS * C_HEAD_DIM
ATTN_BLOCK = 128
GATE_RANK = 256
N_BRANCH = 3

OFF_B = A_COLS
OFF_Q = OFF_B + B_WIDTH
OFF_K = OFF_Q + C_WIDTH
OFF_V = OFF_K + C_KV_WIDTH
OFF_G = OFF_V + C_KV_WIDTH
IN_COLS = OFF_G + GATE_RANK
A_PAD = 5120
A_GD_OFF = 3 * A_WIDTH + 2 * A_DECAY_RANK + 2 * A_ICLR_RANK
A_GD_PAD = A_PAD - A_GD_OFF
P_K = A_PAD
P_V = P_K + C_KV_WIDTH
P_B = P_V + C_KV_WIDTH
P_Q = P_B + B_WIDTH
P_G = P_Q + C_WIDTH
P_COLS = P_G + GATE_RANK


def _dot(a, b):
    return jnp.dot(a, b, preferred_element_type=F32)


def _dot_nt(a, b):
    return lax.dot_general(a, b, (((1,), (1,)), ((), ())), preferred_element_type=F32)


def _dot_tn(a, b):
    return lax.dot_general(a, b, (((0,), (0,)), ((), ())), preferred_element_type=F32)


def _tile(n, cap, mult):
    best = None
    for t in range(mult, min(n, cap) + 1, mult):
        if n % t == 0:
            best = t
    assert best is not None, (n, cap, mult)
    return best


def _params(*sem):
    return pltpu.CompilerParams(dimension_semantics=sem, vmem_limit_bytes=VMEM_LIMIT)


def _sigmoid(x):
    return 1.0 / (1.0 + jnp.exp(-x))


def _silu(x):
    return x * _sigmoid(x)


def _split3(x):
    hi = x.astype(BF16)
    rem = x - hi.astype(F32)
    mid = rem.astype(BF16)
    lo = (rem - mid.astype(F32)).astype(BF16)
    return hi, mid, lo


def _head_block_ones():
    r = lax.broadcasted_iota(jnp.int32, (3 * LANES, LANES), 0)
    c = lax.broadcasted_iota(jnp.int32, (3 * LANES, LANES), 1)
    return jnp.where(((r % LANES) // A_HEAD_DIM) == (c // A_HEAD_DIM), 1.0, 0.0).astype(BF16)


def _head_sums(x, ones_bd):
    out = []
    for j in range(0, x.shape[1], LANES):
        out.append(_dot(jnp.concatenate(_split3(x[:, j:j + LANES]), axis=1), ones_bd))
    return jnp.concatenate(out, axis=1)


def _mod_kernel(c_ref, down_ref, up_ref, b_ref, o_ref):
    s = _silu(c_ref[...]).astype(BF16)
    low = _dot(s, down_ref[0].astype(BF16)).astype(BF16)
    o_ref[0] = _dot(low, up_ref[0].astype(BF16)) + b_ref[0]


def _modulation(c8, down, up, bias):
    depth, d, rank = down.shape
    n = up.shape[2]
    tn = _tile(n, d, LANES)
    return pl.pallas_call(
        _mod_kernel,
        grid=(depth, n // tn),
        in_specs=[pl.BlockSpec((SUBLANES, d), lambda l, j: (0, 0)),
                  pl.BlockSpec((1, d, rank), lambda l, j: (l, 0, 0)),
                  pl.BlockSpec((1, rank, tn), lambda l, j: (l, 0, j)),
                  pl.BlockSpec((1, 1, tn), lambda l, j: (l, 0, j))],
        out_specs=pl.BlockSpec((1, SUBLANES, tn), lambda l, j: (l, 0, j)),
        out_shape=jax.ShapeDtypeStruct((depth, SUBLANES, n), F32),
        compiler_params=_params("arbitrary", "arbitrary"),
        name="modulation",
    )(c8, down, up, bias.reshape(depth, 1, n))


def _rms(x, g):
    return x * lax.rsqrt(jnp.mean(x * x, axis=-1, keepdims=True) + NORM_EPS) * g


def _resid_norm_kernel(*refs, ctx_tiles, has_m, emit_h):
    refs = list(refs)
    x_ref = refs.pop(0)
    m_ref = refs.pop(0) if has_m else None
    g_ref = refs.pop(0)
    mod_ref = refs.pop(0)
    is_ctx = pl.program_id(0) < ctx_tiles

    def pick(i):
        return jnp.where(is_ctx, mod_ref[i:i + 1, :], mod_ref[i + 1:i + 2, :])

    x = x_ref[...]
    if has_m:
        x = x + pick(0) * _rms(m_ref[...].astype(F32), g_ref[0:1, :])
        refs.pop(0)[...] = x
    if emit_h:
        h = _rms(x, g_ref[1:2, :]) * (1.0 + pick(2)) + pick(4)
        refs.pop(0)[...] = h.astype(BF16)


def _join_norm_kernel(ctx_ref, x_ref, g_ref, mod_ref, xs_ref, h_ref, *, ctx_tiles):
    is_ctx = pl.program_id(0) < ctx_tiles
    x = jnp.where(is_ctx, ctx_ref[...], x_ref[...])
    xs_ref[...] = x
    scale = jnp.where(is_ctx, mod_ref[0:1, :], mod_ref[1:2, :])
    shift = jnp.where(is_ctx, mod_ref[2:3, :], mod_ref[3:4, :])
    h_ref[...] = (_rms(x, g_ref[...]) * (1.0 + scale) + shift).astype(BF16)


def _join_norm(ctx, x, g, mod4):
    L, D = ctx.shape
    T = x.shape[0]
    te = _tile(math.gcd(L, T), 256, 16)
    ctx_tiles = L // te
    row = pl.BlockSpec((te, D), lambda i: (i, 0))
    return pl.pallas_call(
        functools.partial(_join_norm_kernel, ctx_tiles=ctx_tiles),
        grid=((L + T) // te,),
        in_specs=[pl.BlockSpec((te, D), lambda i: (jnp.minimum(i, ctx_tiles - 1), 0)),
                  pl.BlockSpec((te, D), lambda i: (jnp.maximum(i - ctx_tiles, 0), 0)),
                  pl.BlockSpec((1, D), lambda i: (0, 0)), pl.BlockSpec((4, D), lambda i: (0, 0))],
        out_specs=[row, row],
        out_shape=[jax.ShapeDtypeStruct((L + T, D), F32), jax.ShapeDtypeStruct((L + T, D), BF16)],
        compiler_params=_params("arbitrary"),
        name="join_norm",
    )(ctx, x, g, mod4)


def _resid_norm(x, m, g2, mod6, *, n_ctx, emit_h, latent_only=False):
    R, D = x.shape
    te = _tile(math.gcd(n_ctx, R - n_ctx), 256, 16)
    has_m = m is not None
    skip = n_ctx // te if latent_only else 0
    n_out = R - skip * te
    row_in = pl.BlockSpec((te, D), lambda i: (i + skip, 0))
    row = pl.BlockSpec((te, D), lambda i: (i, 0))
    ins = [x] + ([m] if has_m else []) + [g2, mod6]
    in_specs = [row_in] * (2 if has_m else 1) + [pl.BlockSpec((2, D), lambda i: (0, 0)),
                                                 pl.BlockSpec((6, D), lambda i: (0, 0))]
    out_shape, out_specs = [], []
    if has_m:
        out_shape.append(jax.ShapeDtypeStruct((n_out, D), F32))
        out_specs.append(row)
    if emit_h:
        out_shape.append(jax.ShapeDtypeStruct((n_out, D), BF16))
        out_specs.append(row)
    outs = pl.pallas_call(
        functools.partial(_resid_norm_kernel, ctx_tiles=n_ctx // te - skip, has_m=has_m, emit_h=emit_h),
        grid=(n_out // te,),
        in_specs=in_specs, out_specs=out_specs, out_shape=out_shape,
        compiler_params=_params("parallel"),
        name="resid_norm",
    )(*ins)
    outs = list(outs)
    x1 = outs.pop(0) if has_m else None
    h = outs.pop(0) if emit_h else None
    return x1, h


def _mm_kernel(x_ref, w_ref, o_ref, *scratch, nk):
    part = _dot(x_ref[...].astype(BF16), w_ref[...].astype(BF16))
    if nk == 1:
        o_ref[...] = part.astype(o_ref.dtype)
        return
    acc_ref, = scratch
    k = pl.program_id(2)

    @pl.when(k == 0)
    def _():
        acc_ref[...] = part

    @pl.when(k > 0)
    def _():
        acc_ref[...] += part

    @pl.when(k == nk - 1)
    def _():
        o_ref[...] = acc_ref[...].astype(o_ref.dtype)


def _matmul(x, w, layer, *, out_dtype=F32, tm_cap=1280, tn_cap=512, tk_cap=4096, name="matmul"):
    M, K = x.shape
    N = w.shape[2]
    tm = _tile(M, tm_cap, 16)
    tn = _tile(N, tn_cap, LANES)
    tk = _tile(K, tk_cap, LANES)
    nk = K // tk
    return pl.pallas_call(
        functools.partial(_mm_kernel, nk=nk),
        grid=(M // tm, N // tn, nk),
        in_specs=[pl.BlockSpec((tm, tk), lambda i, j, k: (i, k)),
                  pl.BlockSpec((None, tk, tn), lambda i, j, k: (layer, k, j))],
        out_specs=pl.BlockSpec((tm, tn), lambda i, j, k: (i, j)),
        out_shape=jax.ShapeDtypeStruct((M, N), out_dtype),
        scratch_shapes=[pltpu.VMEM((tm, tn), F32)] if nk > 1 else [],
        compiler_params=_params("parallel", "parallel", "arbitrary"),
        name=name,
    )(x, w)


def _ffn_up_kernel(h_ref, w1_ref, w3_ref, o_ref):
    h = h_ref[...]
    o_ref[...] = (_silu(_dot(h, w1_ref[...].astype(BF16))) * _dot(h, w3_ref[...].astype(BF16))).astype(o_ref.dtype)


def _ffn_up(h, w1, w3, layer):
    M, K = h.shape
    N = w1.shape[2]
    tm = _tile(M, 1280, 16)
    tn = _tile(N, 256, LANES)
    wspec = pl.BlockSpec((None, K, tn), lambda i, j: (layer, 0, j))
    return pl.pallas_call(
        _ffn_up_kernel,
        grid=(M // tm, N // tn),
        in_specs=[pl.BlockSpec((tm, K), lambda i, j: (i, 0)), wspec, wspec],
        out_specs=pl.BlockSpec((tm, tn), lambda i, j: (i, j)),
        out_shape=jax.ShapeDtypeStruct((M, N), BF16),
        compiler_params=_params("parallel", "parallel"),
        name="ffn_up",
    )(h, w1, w3)


def _merge_kernel(pg_ref, ya_ref, yb_ref, yc_ref, ga_ref, gb_ref, gc_ref, wa_ref, wb_ref, wc_ref, o_ref):
    pg = pg_ref[...].astype(BF16)
    acc = _sigmoid(_dot(pg, ga_ref[...])) * _dot(ya_ref[...], wa_ref[...])
    acc += _sigmoid(_dot(pg, gb_ref[...])) * _dot(yb_ref[...], wb_ref[...])
    acc += _sigmoid(_dot(pg, gc_ref[...])) * _dot(yc_ref[...], wc_ref[...])
    o_ref[...] = acc.astype(o_ref.dtype)


def _merge(p, ya, yb, yc, gate_up, wa, wb, wc, layer):
    R = p.shape[0]
    D = wa.shape[2]
    tm = _tile(R, 1280, 16)
    tn = _tile(D, 512, LANES)
    nj = D // tn

    def rows(width):
        return pl.BlockSpec((tm, width), lambda i, j: (i, 0))

    def gate(branch):
        return pl.BlockSpec((None, GATE_RANK, tn), lambda i, j: (layer, 0, branch * nj + j))

    def wcol(width):
        return pl.BlockSpec((None, width, tn), lambda i, j: (layer, 0, j))

    return pl.pallas_call(
        _merge_kernel,
        grid=(R // tm, nj),
        in_specs=[pl.BlockSpec((tm, GATE_RANK), lambda i, j: (i, P_G // GATE_RANK)),
                  rows(A_WIDTH), rows(B_WIDTH), rows(C_WIDTH),
                  gate(0), gate(1), gate(2), wcol(A_WIDTH), wcol(B_WIDTH), wcol(C_WIDTH)],
        out_specs=pl.BlockSpec((tm, tn), lambda i, j: (i, j)),
        out_shape=jax.ShapeDtypeStruct((R, D), BF16),
        compiler_params=_params("parallel", "parallel"),
        name="merge",
    )(p, ya, yb, yc, gate_up, gate_up, gate_up, wa, wb, wc)


def _segment_flags(i, ctx_tiles, n_tiles):
    first = jnp.logical_or(i == 0, i == ctx_tiles)
    last = jnp.logical_or(i == ctx_tiles - 1, i == n_tiles - 1)
    return first, last


def _rwkv_prep_kernel(u_ref, prev_ref, next_ref, mu_ref, w0_ref, wup_ref, a0_ref, aup_ref, kk_ref, ka_ref,
                      r_out, v_out, kap_out, lw0_out, lw1_out, k0_out, k1_out, b0_out, b1_out, gs_out,
                      *, ctx_tiles, n_tiles):
    i = pl.program_id(0)
    first, last = _segment_flags(i, ctx_tiles, n_tiles)
    u = u_ref[...]
    tm = u.shape[0]
    rid = lax.broadcasted_iota(jnp.int32, (tm, 1), 0)
    prev_row = jnp.where(first, 0.0, prev_ref[SUBLANES - 1:SUBLANES, :])
    next_row = jnp.where(last, 0.0, next_ref[0:1, :])
    prev = jnp.where(rid == 0, prev_row, pltpu.roll(u, 1, axis=0))
    nxt = jnp.where(rid == tm - 1, next_row, pltpu.roll(u, tm - 1, axis=0))
    s = u + mu_ref[...] * (0.5 * (prev + nxt) - u)

    W = A_WIDTH
    r = s[:, 0:W]
    k = s[:, W:2 * W]
    v = s[:, 2 * W:3 * W]
    o = 3 * W
    wd = jnp.tanh(s[:, o:o + LANES]).astype(BF16)
    ad = s[:, o + LANES:o + 2 * LANES].astype(BF16)
    gd = s[:, A_GD_OFF:A_PAD]

    ones_bd = _head_block_ones()
    kk = k * kk_ref[...]
    nrm = jnp.sqrt(_head_sums(kk * kk, ones_bd))
    kk = kk / jnp.maximum(nrm, 1e-12)

    r_out[...] = r.astype(BF16)
    v_out[...] = v.astype(BF16)
    kap_out[...] = kk.astype(BF16)
    gs_out[...] = _sigmoid(gd).astype(BF16)
    for d, (lw_out, k_out, b_out) in enumerate(((lw0_out, k0_out, b0_out), (lw1_out, k1_out, b1_out))):
        z = w0_ref[d:d + 1, :] + _dot(wd, wup_ref[d])
        lw_out[...] = -math.exp(-0.5) * _sigmoid(z)
        a = _sigmoid(a0_ref[d:d + 1, :] + _dot(ad, aup_ref[d]))
        k_out[...] = (k * (1.0 + (a - 1.0) * ka_ref[...])).astype(BF16)
        b_out[...] = (kk * a).astype(BF16)


def _rwkv_prep(p, mu, w0, wup2, a0, aup2, k_k, k_a, *, n_ctx):
    R = p.shape[0]
    tm = _tile(math.gcd(n_ctx, R - n_ctx), 256, 16)
    n_tiles = R // tm
    hb = tm // SUBLANES
    n_hblocks = R // SUBLANES
    W = A_WIDTH

    def const(shape):
        return pl.BlockSpec(shape, lambda i: (0,) * len(shape))

    wide = pl.BlockSpec((tm, W), lambda i: (i, 0))
    f32w = jax.ShapeDtypeStruct((R, W), F32)
    bf16w = jax.ShapeDtypeStruct((R, W), BF16)
    return pl.pallas_call(
        functools.partial(_rwkv_prep_kernel, ctx_tiles=n_ctx // tm, n_tiles=n_tiles),
        grid=(n_tiles,),
        in_specs=[pl.BlockSpec((tm, A_PAD), lambda i: (i, 0)),
                  pl.BlockSpec((SUBLANES, A_PAD), lambda i: (jnp.maximum(i * hb - 1, 0), 0)),
                  pl.BlockSpec((SUBLANES, A_PAD), lambda i: (jnp.minimum((i + 1) * hb, n_hblocks - 1), 0)),
                  const((1, A_PAD)), const((2, W)), const((2, LANES, W)), const((2, W)), const((2, LANES, W)),
                  const((1, W)), const((1, W))],
        out_specs=[wide] * 9 + [pl.BlockSpec((tm, A_GD_PAD), lambda i: (i, 0))],
        out_shape=[bf16w] * 3 + [f32w] * 2 + [bf16w] * 4 + [jax.ShapeDtypeStruct((R, A_GD_PAD), BF16)],
        compiler_params=_params("parallel"),
        name="rwkv_prep",
    )(p, p, p, mu, w0, wup2, a0, aup2, k_k, k_a)


def _scan_kernel(lw_ref, k_ref, b_ref, kap_ref, v_ref, r_ref, y_ref, s_ref, *, reverse, npairs):
    C = SCAN_CHUNK
    N = A_HEAD_DIM

    @pl.when(pl.program_id(1) == 0)
    def _():
        s_ref[...] = jnp.zeros_like(s_ref)

    row = lax.broadcasted_iota(jnp.int32, (C, C), 0)
    col = lax.broadcasted_iota(jnp.int32, (C, C), 1)
    earlier = (col > row) if reverse else (col < row)
    diag = col == row
    incl = jnp.logical_or(earlier, diag)
    tri = jnp.where(incl, 1.0, 0.0).astype(BF16)
    tri3 = jnp.concatenate([tri, tri, tri], axis=1)
    eye = jnp.where(diag, 1.0, 0.0).astype(F32)

    def same_block(n):
        return (row // n) == (col // n)

    last = 0 if reverse else C - 1

    hs = (slice(0, N), slice(N, 2 * N))
    heads = range(2 * npairs)
    x, k_t, b_t, k_e, b_e, e_tot, v = ([] for _ in range(7))
    for p in range(npairs):
        sl = slice(p * LANES, (p + 1) * LANES)
        lw = lw_ref[:, sl]
        c = _dot(tri3, jnp.concatenate(_split3(lw), axis=0))
        ctot = c[last:last + 1, :]
        e_nc = jnp.exp(-c)
        e_tc = jnp.exp(ctot - c)
        kap_p = kap_ref[:, sl].astype(F32) * jnp.exp(c - lw)
        r_p = r_ref[:, sl].astype(F32) * jnp.exp(c)
        k_p = k_ref[:, sl].astype(F32)
        b_p = b_ref[:, sl].astype(F32)
        v_p = v_ref[:, sl]
        e_p = jnp.exp(ctot)
        for h in hs:
            x.append(jnp.concatenate([kap_p[:, h], r_p[:, h]], axis=0).astype(BF16))
            k_t.append((k_p * e_nc)[:, h].astype(BF16))
            b_t.append((b_p * e_nc)[:, h].astype(BF16))
            k_e.append((k_p * e_tc)[:, h].astype(BF16))
            b_e.append((b_p * e_tc)[:, h].astype(BF16))
            e_tot.append(e_p[:, h])
            v.append(v_p[:, h].astype(BF16))

    def bd(a, b):
        return _dot(a.astype(BF16), b.astype(BF16))

    g1 = [_dot_nt(x[i], k_t[i]) for i in heads]
    g2 = [_dot_nt(x[i], b_t[i]) for i in heads]
    a_kk = [jnp.where(earlier, g1[i][:C], 0.0).astype(BF16) for i in heads]
    a_rk = [jnp.where(incl, g1[i][C:], 0.0).astype(BF16) for i in heads]
    a_kb = [jnp.where(earlier, g2[i][:C], 0.0) for i in heads]
    a_rb = [jnp.where(incl, g2[i][C:], 0.0).astype(BF16) for i in heads]
    blk8 = same_block(8)
    a0 = [jnp.where(blk8, a_kb[i], 0.0) for i in heads]
    a2 = [bd(a0[i], a0[i]) for i in heads]
    a4 = [bd(a2[i], a2[i]) for i in heads]
    t = [bd(eye - a0[i], eye + a2[i]) for i in heads]
    t = [bd(t[i], eye + a4[i]) for i in heads]
    for n in (16, 32, 64):
        m = jnp.logical_and(same_block(n), jnp.logical_not(same_block(n // 2)))
        off = [jnp.where(m, a_kb[i], 0.0) for i in heads]
        ot = [bd(off[i], t[i]) for i in heads]
        t = [t[i] - bd(t[i], ot[i]) for i in heads]
    s0 = [s_ref[i] for i in heads]
    xs = [_dot_nt(x[i], s0[i].astype(BF16)) for i in heads]
    akv = [_dot(a_kk[i], v[i]) for i in heads]
    u = [bd(t[i], xs[i][:C] + akv[i]).astype(BF16) for i in heads]
    ys = [xs[i][C:] + _dot(a_rk[i], v[i]) - _dot(a_rb[i], u[i]) for i in heads]
    for i in heads:
        s_ref[i] = s0[i] * e_tot[i] + _dot_tn(v[i], k_e[i]) - _dot_tn(u[i], b_e[i])
    for p in range(npairs):
        y_ref[:, p * LANES:(p + 1) * LANES] = jnp.concatenate([ys[2 * p], ys[2 * p + 1]], axis=1)


def _delta_scan(lw, k, b, kap, v, r, *, n_ctx, reverse, pairs_per_block=None):
    R, W = lw.shape
    C = SCAN_CHUNK
    nchunks = R // C
    ctx_chunks = n_ctx // C
    npairs = W // LANES
    pb = npairs if pairs_per_block is None else pairs_per_block
    assert R % C == 0 and n_ctx % C == 0 and npairs % pb == 0

    if reverse:
        def rows(s):
            return jnp.where(s < ctx_chunks, ctx_chunks - 1 - s, nchunks - 1 - (s - ctx_chunks))
    else:
        def rows(s):
            return s

    spec = pl.BlockSpec((C, pb * LANES), lambda g, s: (rows(s), g))
    return pl.pallas_call(
        functools.partial(_scan_kernel, reverse=reverse, npairs=pb),
        grid=(npairs // pb, nchunks),
        in_specs=[spec] * 6,
        out_specs=spec,
        out_shape=jax.ShapeDtypeStruct((R, W), F32),
        scratch_shapes=[pltpu.VMEM((2 * pb, A_HEAD_DIM, A_HEAD_DIM), F32)],
        compiler_params=_params("parallel", "arbitrary"),
        name="delta_scan_rev" if reverse else "delta_scan_fwd",
    )(lw, k, b, kap, v, r)


def _rwkv_readout_kernel(yf_ref, yb_ref, r_ref, v_ref, k0_ref, k1_ref, gs_ref, gup_ref, rk_ref, lng_ref, lnb_ref,
                         o_ref):
    ones_bd = _head_block_ones()
    inv_n = 1.0 / A_HEAD_DIM
    ro = yf_ref[...] + yb_ref[...]
    mu = _head_sums(ro, ones_bd) * inv_n
    cen = ro - mu
    var = _head_sums(cen * cen, ones_bd) * inv_n
    yn = cen * lax.rsqrt(var + A_GN_EPS) * lng_ref[...] + lnb_ref[...]
    rk = r_ref[...].astype(F32) * (k0_ref[...].astype(F32) + k1_ref[...].astype(F32)) * rk_ref[...]
    bonus = _head_sums(rk, ones_bd) * v_ref[...].astype(F32)
    g = _dot(gs_ref[...], gup_ref[...])
    o_ref[...] = ((yn + bonus) * g).astype(o_ref.dtype)


def _rwkv_readout(yf, yb, r, v, k0, k1, gs, gup, r_k, ln_g, ln_b):
    R, W = yf.shape
    tm = _tile(R, 256, 16)
    wide = pl.BlockSpec((tm, W), lambda i: (i, 0))
    vec = pl.BlockSpec((1, W), lambda i: (0, 0))
    return pl.pallas_call(
        _rwkv_readout_kernel,
        grid=(R // tm,),
        in_specs=[wide] * 6 + [pl.BlockSpec((tm, A_GD_PAD), lambda i: (i, 0)),
                               pl.BlockSpec((A_GD_PAD, W), lambda i: (0, 0)), vec, vec, vec],
        out_specs=wide,
        out_shape=jax.ShapeDtypeStruct((R, W), BF16),
        compiler_params=_params("parallel"),
        name="rwkv_readout",
    )(yf, yb, r, v, k0, k1, gs, gup, r_k, ln_g, ln_b)


def _pool_kernel(u_ref, prev_ref, next_ref, w_ref, scale_ref, o_ref, ext_ref, *, ctx_tiles, n_tiles, n_ctx, n_lat):
    i = pl.program_id(0)
    first, last = _segment_flags(i, ctx_tiles, n_tiles)
    tm = u_ref.shape[0]
    H = POOL_HALO
    ext_ref[0:H, :] = jnp.where(first, 0.0, prev_ref[...])
    ext_ref[H:H + tm, :] = u_ref[...]
    ext_ref[H + tm:H + tm + H, :] = jnp.where(last, 0.0, next_ref[...])
    is_ctx = i < ctx_tiles
    seg_len = jnp.where(is_ctx, n_ctx, n_lat)
    t = lax.broadcasted_iota(jnp.int32, (tm, 1), 0) + i * tm - jnp.where(is_ctx, 0, n_ctx)
    for gi, win in enumerate(POOL_WINDOWS):
        cols = slice(gi * POOL_GROUP_W, (gi + 1) * POOL_GROUP_W)
        acc = None
        for o in range(-(win // 2), win // 2):
            term = ext_ref[H + o:H + o + tm, cols]
            acc = term if acc is None else acc + term
        lo = jnp.maximum(t - win // 2, 0)
        hi = jnp.minimum(t + win // 2 - 1, seg_len - 1)
        cnt = (hi - lo + 1).astype(F32)
        pooled = acc / cnt - u_ref[:, cols]
        y = _dot(pooled.astype(BF16), w_ref[gi]) * scale_ref[:, cols]
        o_ref[:, cols] = y.astype(o_ref.dtype)


def _pool(p, pool_w, pool_scale, *, n_ctx):
    R = p.shape[0]
    tm = _tile(math.gcd(n_ctx, R - n_ctx), 256, 16)
    n_tiles = R // tm
    hb = tm // POOL_HALO
    n_hblocks = R // POOL_HALO
    cb = P_B // B_WIDTH
    assert P_B % B_WIDTH == 0
    return pl.pallas_call(
        functools.partial(_pool_kernel, ctx_tiles=n_ctx // tm, n_tiles=n_tiles, n_ctx=n_ctx, n_lat=R - n_ctx),
        grid=(n_tiles,),
        in_specs=[pl.BlockSpec((tm, B_WIDTH), lambda i: (i, cb)),
                  pl.BlockSpec((POOL_HALO, B_WIDTH), lambda i: (jnp.maximum(i * hb - 1, 0), cb)),
                  pl.BlockSpec((POOL_HALO, B_WIDTH), lambda i: (jnp.minimum((i + 1) * hb, n_hblocks - 1), cb)),
                  pl.BlockSpec((len(POOL_WINDOWS), POOL_GROUP_W, POOL_GROUP_W), lambda i: (0, 0, 0)),
                  pl.BlockSpec((1, B_WIDTH), lambda i: (0, 0))],
        out_specs=pl.BlockSpec((tm, B_WIDTH), lambda i: (i, 0)),
        out_shape=jax.ShapeDtypeStruct((R, B_WIDTH), BF16),
        scratch_shapes=[pltpu.VMEM((tm + 2 * POOL_HALO, B_WIDTH), F32)],
        compiler_params=_params("parallel"),
        name="pool",
    )(p, p, p, pool_w, pool_scale)


def _rope(x, cos, sin):
    return x * cos + pltpu.roll(x, C_HEAD_DIM // 2, axis=1) * sin


def _rope_tables(n_ctx, n_lat):
    half = C_HEAD_DIM // 2
    t = jnp.arange(n_lat)
    row = (t // GRID_W).astype(F32)
    col = (t % GRID_W).astype(F32)
    inv = ROPE_BASE ** (-jnp.arange(0, half, 2, dtype=F32) / half)
    ar = row[:, None] * inv[None]
    ac = col[:, None] * inv[None]
    cos = jnp.concatenate([jnp.cos(ar), jnp.cos(ac), jnp.cos(ar), jnp.cos(ac)], axis=1)
    sin = jnp.concatenate([-jnp.sin(ar), -jnp.sin(ac), jnp.sin(ar), jnp.sin(ac)], axis=1)
    cos = jnp.concatenate([jnp.ones((n_ctx, C_HEAD_DIM), F32), cos], axis=0)
    sin = jnp.concatenate([jnp.zeros((n_ctx, C_HEAD_DIM), F32), sin], axis=0)
    return cos, sin


def _attn_kernel(q0_ref, q1_ref, q2_ref, q3_ref, kp_ref, kc_ref, kn_ref, kx_ref, vp_ref, vc_ref, vn_ref, vx_ref,
                 cp_ref, sp_ref, co_ref, so_ref, cn_ref, sn_ref, sink_ref, o_ref, *, ctx_qblocks, n_qblocks):
    i = pl.program_id(0)
    B = ATTN_BLOCK
    q_refs = (q0_ref, q1_ref, q2_ref, q3_ref)
    tq = q0_ref.shape[0]
    cos_o = co_ref[...]
    sin_o = so_ref[...]
    nloc = tq + 2 * B
    qrow = lax.broadcasted_iota(jnp.int32, (tq, nloc), 0)
    kcol = lax.broadcasted_iota(jnp.int32, (tq, nloc), 1)
    rel = kcol - B - qrow
    lo = jnp.where(i == ctx_qblocks, B, 0)
    hi = jnp.where(i < ctx_qblocks, 0, jnp.where(i == n_qblocks - 1, B + tq, nloc))
    bias = jnp.where(jnp.abs(rel) <= ATTN_BLOCK, 0.0, NEG_INF)
    bias = jnp.where(kcol >= lo, bias, NEG_INF)
    bias = jnp.where(kcol < hi, bias, NEG_INF)
    scale = C_HEAD_DIM ** -0.5
    for h in range(C_KV_HEADS):
        kc = slice(h * C_HEAD_DIM, (h + 1) * C_HEAD_DIM)
        k_loc = jnp.concatenate([_rope(kp_ref[:, kc], cp_ref[...], sp_ref[...]).astype(BF16),
                                 _rope(kc_ref[:, kc], cos_o, sin_o).astype(BF16),
                                 _rope(kn_ref[:, kc], cn_ref[...], sn_ref[...]).astype(BF16)], axis=0)
        v_loc = jnp.concatenate([vp_ref[:, kc].astype(BF16), vc_ref[:, kc].astype(BF16),
                                 vn_ref[:, kc].astype(BF16)], axis=0)
        k_ctx = kx_ref[:, kc].astype(BF16)
        v_ctx = vx_ref[:, kc].astype(BF16)
        for g in range(C_GROUP):
            j = h * C_GROUP + g
            cols = slice(j * C_HEAD_DIM, (j + 1) * C_HEAD_DIM)
            q = _rope(q_refs[h][:, g * C_HEAD_DIM:(g + 1) * C_HEAD_DIM], cos_o, sin_o).astype(BF16)
            s_loc = _dot_nt(q, k_loc) * scale + bias
            s_ctx = _dot_nt(q, k_ctx) * scale
            sk = sink_ref[j:j + 1, 0:1]
            m = jnp.maximum(jnp.maximum(jnp.max(s_loc, axis=-1, keepdims=True),
                                        jnp.max(s_ctx, axis=-1, keepdims=True)), sk)
            e_loc = jnp.exp(s_loc - m)
            e_ctx = jnp.exp(s_ctx - m)
            denom = (jnp.sum(e_loc, axis=-1, keepdims=True) + jnp.sum(e_ctx, axis=-1, keepdims=True)
                     + jnp.exp(sk - m))
            o = _dot(e_loc.astype(BF16), v_loc) + _dot(e_ctx.astype(BF16), v_ctx)
            o_ref[:, cols] = (o / denom).astype(o_ref.dtype)


def _attention(p, cos_tab, sin_tab, sink16, *, n_ctx):
    R = p.shape[0]
    B = ATTN_BLOCK
    tq = _tile(math.gcd(n_ctx, R - n_ctx), 2 * B, B)
    per = tq // B
    n_qblocks = R // tq
    n_blocks = R // B
    ctx_blocks = n_ctx // B
    W = C_KV_WIDTH
    assert n_ctx % B == 0 and R % B == 0 and P_Q % W == 0 and P_K % W == 0 and P_V % W == 0
    assert C_GROUP * C_HEAD_DIM == W
    kcol, vcol = P_K // W, P_V // W

    def prev_rows(i):
        return jnp.clip(i * per - 1, ctx_blocks, n_blocks - 1)

    def next_rows(i):
        return jnp.clip((i + 1) * per, ctx_blocks, n_blocks - 1)

    def kv_specs(c):
        return [pl.BlockSpec((B, W), lambda i: (prev_rows(i), c)), pl.BlockSpec((tq, W), lambda i: (i, c)),
                pl.BlockSpec((B, W), lambda i: (next_rows(i), c)), pl.BlockSpec((n_ctx, W), lambda i: (0, c))]

    tab_specs = []
    for rows, fn in ((B, prev_rows), (tq, lambda i: i), (B, next_rows)):
        tab_specs += [pl.BlockSpec((rows, C_HEAD_DIM), lambda i, fn=fn: (fn(i), 0))] * 2
    q_specs = [pl.BlockSpec((tq, W), lambda i, h=h: (i, P_Q // W + h)) for h in range(C_KV_HEADS)]

    return pl.pallas_call(
        functools.partial(_attn_kernel, ctx_qblocks=n_ctx // tq, n_qblocks=n_qblocks),
        grid=(n_qblocks,),
        in_specs=q_specs + kv_specs(kcol) + kv_specs(vcol) + tab_specs
        + [pl.BlockSpec((C_Q_HEADS, LANES), lambda i: (0, 0))],
        out_specs=pl.BlockSpec((tq, C_WIDTH), lambda i: (i, 0)),
        out_shape=jax.ShapeDtypeStruct((R, C_WIDTH), BF16),
        compiler_params=_params("parallel"),
        name="window_attention",
    )(*([p] * 12), cos_tab, sin_tab, cos_tab, sin_tab, cos_tab, sin_tab, sink16)


def _pad_to(a, axis, size):
    pad = [(0, 0)] * a.ndim
    pad[axis] = (0, size - a.shape[axis])
    return jnp.pad(a, pad)


def _relayout_w_in_kernel(w_ref, o_ref):
    u = w_ref[...]
    rows = u.shape[0]
    quarter = C_HEAD_DIM // 4
    lane = lax.broadcasted_iota(jnp.int32, (1, C_HEAD_DIM), 1)
    second = jnp.logical_and(lane >= quarter, lane < 2 * quarter)
    third = jnp.logical_and(lane >= 2 * quarter, lane < 3 * quarter)

    def rope_order(x):
        heads = []
        for h in range(x.shape[1] // C_HEAD_DIM):
            t = x[:, h * C_HEAD_DIM:(h + 1) * C_HEAD_DIM]
            from_right = pltpu.roll(t, 3 * quarter, axis=1)
            from_left = pltpu.roll(t, quarter, axis=1)
            heads.append(jnp.where(second, from_right, jnp.where(third, from_left, t)))
        return jnp.concatenate(heads, axis=1)

    pieces = [u[:, 0:OFF_B], jnp.zeros((rows, A_PAD - OFF_B), F32), rope_order(u[:, OFF_K:OFF_V]),
              u[:, OFF_V:OFF_G], u[:, OFF_B:OFF_Q], rope_order(u[:, OFF_Q:OFF_K]), u[:, OFF_G:IN_COLS]]
    o_ref[...] = jnp.concatenate(pieces, axis=1).astype(o_ref.dtype)


def _relayout_w_in(w_in):
    depth, D, n = w_in.shape
    assert n == IN_COLS
    tr = _tile(D, 256, 16)
    return pl.pallas_call(
        _relayout_w_in_kernel,
        grid=(depth, D // tr),
        in_specs=[pl.BlockSpec((None, tr, n), lambda l, i: (l, i, 0))],
        out_specs=pl.BlockSpec((None, tr, P_COLS), lambda l, i: (l, i, 0)),
        out_shape=jax.ShapeDtypeStruct((depth, D, P_COLS), BF16),
        compiler_params=_params("parallel", "parallel"),
        name="w_in_relayout",
    )(w_in)


def _low_rank_pair(up):
    z = jnp.zeros_like(up[:, 0])
    return jnp.stack([jnp.concatenate([up[:, 0], z], axis=1), jnp.concatenate([z, up[:, 1]], axis=1)], axis=1).astype(BF16)


def kernel(x, c, ctx, c_ctx, mod_down, mod_up, mod_b, norm_g, w_in, shift_mu, rwkv_w0, rwkv_w_up, rwkv_a0,
           rwkv_a_up, rwkv_g_up, rwkv_k_k, rwkv_k_a, rwkv_r_k, rwkv_ln_g, rwkv_ln_b, pool_w, pool_scale,
           attn_sink, gate_up, w_branch_a, w_branch_b, w_branch_c, w_out, ffn_w1, ffn_w3, ffn_w2):
    assert x.shape[0] == 1 and ctx.shape[0] == 1 and c.shape[0] == 1
    depth = w_in.shape[0]
    T, D = x.shape[1], x.shape[2]
    L = ctx.shape[1]
    assert 2 * A_DECAY_RANK == LANES and 2 * A_ICLR_RANK == LANES

    w_in_p = _relayout_w_in(w_in)
    mu_p = _pad_to(shift_mu, -1, A_PAD)[:, None, :]
    wup2 = _low_rank_pair(rwkv_w_up)
    aup2 = _low_rank_pair(rwkv_a_up)
    gup_p = _pad_to(rwkv_g_up, 1, A_GD_PAD).astype(BF16)
    pool_w_b = pool_w.astype(BF16)
    gate_up_b = gate_up.astype(BF16)
    wa_b, wb_b, wc_b = (w.astype(BF16) for w in (w_branch_a, w_branch_b, w_branch_c))
    w2_b = ffn_w2.astype(BF16)
    sink16 = jnp.broadcast_to(attn_sink[..., None], (depth, C_Q_HEADS, LANES))
    cos_tab, sin_tab = _rope_tables(L, T)

    c8 = _pad_to(jnp.concatenate([c_ctx[None], c], axis=0), 0, SUBLANES)
    mod = _modulation(c8, mod_down, mod_up, mod_b).reshape(depth, SUBLANES, 6, D)

    def mod6(l, shift_i, scale_i, gate_i):
        m = mod[l]
        return jnp.stack([m[0, gate_i], m[1, gate_i], m[0, scale_i], m[1, scale_i], m[0, shift_i], m[1, shift_i]])

    xs, h = _join_norm(ctx[0], x[0], norm_g[0, 0][None], mod6(0, 0, 1, 2)[2:])
    for l in range(depth):
        p = _matmul(h, w_in_p, l, tn_cap=768, name="w_in")
        r, v, kap, lw0, lw1, k0, k1, b0, b1, gs = _rwkv_prep(
            p, mu_p[l], rwkv_w0[l], wup2[l], rwkv_a0[l], aup2[l], rwkv_k_k[l][None], rwkv_k_a[l][None], n_ctx=L)
        yf = _delta_scan(lw0, k0, b0, kap, v, r, n_ctx=L, reverse=False)
        yr = _delta_scan(lw1, k1, b1, kap, v, r, n_ctx=L, reverse=True)
        y_a = _rwkv_readout(yf, yr, r, v, k0, k1, gs, gup_p[l], rwkv_r_k[l][None], rwkv_ln_g[l][None],
                            rwkv_ln_b[l][None])
        y_b = _pool(p, pool_w_b[l], pool_scale[l][None], n_ctx=L)
        y_c = _attention(p, cos_tab, sin_tab, sink16[l], n_ctx=L)
        acc = _merge(p, y_a, y_b, y_c, gate_up_b, wa_b, wb_b, wc_b, l)
        mix = _matmul(acc, w_out, l, out_dtype=BF16, name="w_out")
        xs, h2 = _resid_norm(xs, mix, norm_g[l, 1:3], mod6(l, 3, 4, 2), n_ctx=L, emit_h=True)
        f = _matmul(_ffn_up(h2, ffn_w1, ffn_w3, l), w2_b, l, out_dtype=BF16, tk_cap=5504, name="ffn_down")
        if l + 1 < depth:
            g2 = jnp.stack([norm_g[l, 3], norm_g[l + 1, 0]])
            m6 = jnp.concatenate([mod6(l, 0, 1, 5)[:2], mod6(l + 1, 0, 1, 2)[2:]], axis=0)
            xs, h = _resid_norm(xs, f, g2, m6, n_ctx=L, emit_h=True)
        else:
            xs, _ = _resid_norm(xs, f, jnp.stack([norm_g[l, 3], norm_g[l, 3]]), mod6(l, 0, 1, 5), n_ctx=L,
                                emit_h=False, latent_only=True)
    return xs[None]
```

```python
import functools
import math

import jax
import jax.numpy as jnp
from jax import lax
from jax.experimental import pallas as pl
from jax.experimental.pallas import tpu as pltpu

F32 = jnp.float32
BF16 = jnp.bfloat16

LANES = 128
SUBLANES = 8
VMEM_LIMIT = 56 * 1024 * 1024

NORM_EPS = 1e-6
GRID_W = 64
ROPE_BASE = 10000.0
NEG_INF = -1e30

A_HEADS = 24
A_HEAD_DIM = 64
A_WIDTH = A_HEADS * A_HEAD_DIM
A_DECAY_RANK = 64
A_ICLR_RANK = 64
A_GATE_RANK = 224
A_GN_EPS = 64e-5
A_COLS = 3 * A_WIDTH + 2 * A_DECAY_RANK + 2 * A_ICLR_RANK + A_GATE_RANK
SCAN_CHUNK = 64
POOL_WINDOWS = (2, 4, 8, 16)
POOL_GROUP_W = 384
B_WIDTH = len(POOL_WINDOWS) * POOL_GROUP_W
POOL_HALO = 8
C_Q_HEADS = 16
C_KV_HEADS = 4
C_GROUP = C_Q_HEADS // C_KV_HEADS
C_HEAD_DIM = 128
C_WIDTH = C_Q_HEADS * C_HEAD_DIM
C_KV_WIDTH = C_KV_HEADS * C_HEAD_DIM
ATTN_BLOCK = 128
GATE_RANK = 256
N_BRANCH = 3

OFF_B = A_COLS
OFF_Q = OFF_B + B_WIDTH
OFF_K = OFF_Q + C_WIDTH
OFF_V = OFF_K + C_KV_WIDTH
OFF_G = OFF_V + C_KV_WIDTH
IN_COLS = OFF_G + GATE_RANK
A_PAD = 5120
A_GD_OFF = 3 * A_WIDTH + 2 * A_DECAY_RANK + 2 * A_ICLR_RANK
A_GD_PAD = A_PAD - A_GD_OFF
P_K = A_PAD
P_V = P_K + C_KV_WIDTH
P_B = P_V + C_KV_WIDTH
P_Q = P_B + B_WIDTH
P_G = P_Q + C_WIDTH
P_COLS = P_G + GATE_RANK


def _dot(a, b):
    return jnp.dot(a, b, preferred_element_type=F32)


def _dot_nt(a, b):
    return lax.dot_general(a, b, (((1,), (1,)), ((), ())), preferred_element_type=F32)


def _dot_tn(a, b):
    return lax.dot_general(a, b, (((0,), (0,)), ((), ())), preferred_element_type=F32)


def _tile(n, cap, mult):
    best = None
    for t in range(mult, min(n, cap) + 1, mult):
        if n % t == 0:
            best = t
    assert best is not None, (n, cap, mult)
    return best


def _params(*sem):
    return pltpu.CompilerParams(dimension_semantics=sem, vmem_limit_bytes=VMEM_LIMIT)


def _sigmoid(x):
    return 1.0 / (1.0 + jnp.exp(-x))


def _silu(x):
    return x * _sigmoid(x)


def _split3(x):
    hi = x.astype(BF16)
    rem = x - hi.astype(F32)
    mid = rem.astype(BF16)
    lo = (rem - mid.astype(F32)).astype(BF16)
    return hi, mid, lo


def _head_block_ones():
    r = lax.broadcasted_iota(jnp.int32, (3 * LANES, LANES), 0)
    c = lax.broadcasted_iota(jnp.int32, (3 * LANES, LANES), 1)
    return jnp.where(((r % LANES) // A_HEAD_DIM) == (c // A_HEAD_DIM), 1.0, 0.0).astype(BF16)


def _head_sums(x, ones_bd):
    out = []
    for j in range(0, x.shape[1], LANES):
        out.append(_dot(jnp.concatenate(_split3(x[:, j:j + LANES]), axis=1), ones_bd))
    return jnp.concatenate(out, axis=1)


def _mod_kernel(c_ref, down_ref, up_ref, b_ref, o_ref):
    s = _silu(c_ref[...]).astype(BF16)
    low = _dot(s, down_ref[0].astype(BF16)).astype(BF16)
    o_ref[0] = _dot(low, up_ref[0].astype(BF16)) + b_ref[0]


def _modulation(c8, down, up, bias):
    depth, d, rank = down.shape
    n = up.shape[2]
    tn = _tile(n, d, LANES)
    return pl.pallas_call(
        _mod_kernel,
        grid=(depth, n // tn),
        in_specs=[pl.BlockSpec((SUBLANES, d), lambda l, j: (0, 0)),
                  pl.BlockSpec((1, d, rank), lambda l, j: (l, 0, 0)),
                  pl.BlockSpec((1, rank, tn), lambda l, j: (l, 0, j)),
                  pl.BlockSpec((1, 1, tn), lambda l, j: (l, 0, j))],
        out_specs=pl.BlockSpec((1, SUBLANES, tn), lambda l, j: (l, 0, j)),
        out_shape=jax.ShapeDtypeStruct((depth, SUBLANES, n), F32),
        compiler_params=_params("arbitrary", "arbitrary"),
        name="modulation",
    )(c8, down, up, bias.reshape(depth, 1, n))


def _rms(x, g):
    return x * lax.rsqrt(jnp.mean(x * x, axis=-1, keepdims=True) + NORM_EPS) * g


def _resid_norm_kernel(*refs, ctx_tiles, has_m, emit_h):
    refs = list(refs)
    x_ref = refs.pop(0)
    m_ref = refs.pop(0) if has_m else None
    g_ref = refs.pop(0)
    mod_ref = refs.pop(0)
    is_ctx = pl.program_id(0) < ctx_tiles

    def pick(i):
        return jnp.where(is_ctx, mod_ref[i:i + 1, :], mod_ref[i + 1:i + 2, :])

    x = x_ref[...]
    if has_m:
        x = x + pick(0) * _rms(m_ref[...].astype(F32), g_ref[0:1, :])
        refs.pop(0)[...] = x
    if emit_h:
        h = _rms(x, g_ref[1:2, :]) * (1.0 + pick(2)) + pick(4)
        refs.pop(0)[...] = h.astype(BF16)


def _join_norm_kernel(ctx_ref, x_ref, g_ref, mod_ref, xs_ref, h_ref, *, ctx_tiles):
    is_ctx = pl.program_id(0) < ctx_tiles
    x = jnp.where(is_ctx, ctx_ref[...], x_ref[...])
    xs_ref[...] = x
    scale = jnp.where(is_ctx, mod_ref[0:1, :], mod_ref[1:2, :])
    shift = jnp.where(is_ctx, mod_ref[2:3, :], mod_ref[3:4, :])
    h_ref[...] = (_rms(x, g_ref[...]) * (1.0 + scale) + shift).astype(BF16)


def _join_norm(ctx, x, g, mod4):
    L, D = ctx.shape
    T = x.shape[0]
    te = _tile(math.gcd(L, T), 256, 16)
    ctx_tiles = L // te
    row = pl.BlockSpec((te, D), lambda i: (i, 0))
    return pl.pallas_call(
        functools.partial(_join_norm_kernel, ctx_tiles=ctx_tiles),
        grid=((L + T) // te,),
        in_specs=[pl.BlockSpec((te, D), lambda i: (jnp.minimum(i, ctx_tiles - 1), 0)),
                  pl.BlockSpec((te, D), lambda i: (jnp.maximum(i - ctx_tiles, 0), 0)),
                  pl.BlockSpec((1, D), lambda i: (0, 0)), pl.BlockSpec((4, D), lambda i: (0, 0))],
        out_specs=[row, row],
        out_shape=[jax.ShapeDtypeStruct((L + T, D), F32), jax.ShapeDtypeStruct((L + T, D), BF16)],
        compiler_params=_params("arbitrary"),
        name="join_norm",
    )(ctx, x, g, mod4)


def _resid_norm(x, m, g2, mod6, *, n_ctx, emit_h, latent_only=False):
    R, D = x.shape
    te = _tile(math.gcd(n_ctx, R - n_ctx), 256, 16)
    has_m = m is not None
    skip = n_ctx // te if latent_only else 0
    n_out = R - skip * te
    row_in = pl.BlockSpec((te, D), lambda i: (i + skip, 0))
    row = pl.BlockSpec((te, D), lambda i: (i, 0))
    ins = [x] + ([m] if has_m else []) + [g2, mod6]
    in_specs = [row_in] * (2 if has_m else 1) + [pl.BlockSpec((2, D), lambda i: (0, 0)),
                                                 pl.BlockSpec((6, D), lambda i: (0, 0))]
    out_shape, out_specs = [], []
    if has_m:
        out_shape.append(jax.ShapeDtypeStruct((n_out, D), F32))
        out_specs.append(row)
    if emit_h:
        out_shape.append(jax.ShapeDtypeStruct((n_out, D), BF16))
        out_specs.append(row)
    outs = pl.pallas_call(
        functools.partial(_resid_norm_kernel, ctx_tiles=n_ctx // te - skip, has_m=has_m, emit_h=emit_h),
        grid=(n_out // te,),
        in_specs=in_specs, out_specs=out_specs, out_shape=out_shape,
        compiler_params=_params("parallel"),
        name="resid_norm",
    )(*ins)
    outs = list(outs)
    x1 = outs.pop(0) if has_m else None
    h = outs.pop(0) if emit_h else None
    return x1, h


def _mm_kernel(x_ref, w_ref, o_ref, *scratch, nk, transposed_w):
    dot = _dot_nt if transposed_w else _dot
    part = dot(x_ref[...].astype(BF16), w_ref[...].astype(BF16))
    if nk == 1:
        o_ref[...] = part.astype(o_ref.dtype)
        return
    acc_ref, = scratch
    k = pl.program_id(2)

    @pl.when(k == 0)
    def _():
        acc_ref[...] = part

    @pl.when(k > 0)
    def _():
        acc_ref[...] += part

    @pl.when(k == nk - 1)
    def _():
        o_ref[...] = acc_ref[...].astype(o_ref.dtype)


def _matmul(x, w, layer, *, transposed_w=False, out_dtype=F32, tm_cap=1280, tn_cap=512, tk_cap=4096,
            name="matmul"):
    M, K = x.shape
    N = w.shape[1] if transposed_w else w.shape[2]
    tm = _tile(M, tm_cap, 16)
    tn = _tile(N, tn_cap, LANES)
    tk = _tile(K, tk_cap, LANES)
    nk = K // tk
    if transposed_w:
        w_spec = pl.BlockSpec((None, tn, tk), lambda i, j, k: (layer, j, k))
    else:
        w_spec = pl.BlockSpec((None, tk, tn), lambda i, j, k: (layer, k, j))
    return pl.pallas_call(
        functools.partial(_mm_kernel, nk=nk, transposed_w=transposed_w),
        grid=(M // tm, N // tn, nk),
        in_specs=[pl.BlockSpec((tm, tk), lambda i, j, k: (i, k)), w_spec],
        out_specs=pl.BlockSpec((tm, tn), lambda i, j, k: (i, j)),
        out_shape=jax.ShapeDtypeStruct((M, N), out_dtype),
        scratch_shapes=[pltpu.VMEM((tm, tn), F32)] if nk > 1 else [],
        compiler_params=_params("parallel", "parallel", "arbitrary"),
        name=name,
    )(x, w)


def _ffn_up_kernel(h_ref, w1_ref, w3_ref, o_ref):
    h = h_ref[...]
    o_ref[...] = (_silu(_dot(h, w1_ref[...].astype(BF16))) * _dot(h, w3_ref[...].astype(BF16))).astype(o_ref.dtype)


def _ffn_up(h, w1, w3, layer):
    M, K = h.shape
    N = w1.shape[2]
    tm = _tile(M, 1280, 16)
    tn = _tile(N, 256, LANES)
    wspec = pl.BlockSpec((None, K, tn), lambda i, j: (layer, 0, j))
    return pl.pallas_call(
        _ffn_up_kernel,
        grid=(M // tm, N // tn),
        in_specs=[pl.BlockSpec((tm, K), lambda i, j: (i, 0)), wspec, wspec],
        out_specs=pl.BlockSpec((tm, tn), lambda i, j: (i, j)),
        out_shape=jax.ShapeDtypeStruct((M, N), BF16),
        compiler_params=_params("parallel", "parallel"),
        name="ffn_up",
    )(h, w1, w3)


def _merge_kernel(pg_ref, ya_ref, yb_ref, yc_ref, ga_ref, gb_ref, gc_ref, wa_ref, wb_ref, wc_ref, o_ref):
    pg = pg_ref[...].astype(BF16)
    acc = _sigmoid(_dot(pg, ga_ref[...])) * _dot(ya_ref[...], wa_ref[...])
    acc += _sigmoid(_dot(pg, gb_ref[...])) * _dot(yb_ref[...], wb_ref[...])
    acc += _sigmoid(_dot(pg, gc_ref[...])) * _dot(yc_ref[...], wc_ref[...])
    o_ref[...] = acc.astype(o_ref.dtype)


def _merge(p, ya, yb, yc, gate_up, wa, wb, wc, layer):
    R = p.shape[0]
    D = wa.shape[2]
    tm = _tile(R, 1280, 16)
    tn = _tile(D, 512, LANES)
    nj = D // tn

    def rows(width):
        return pl.BlockSpec((tm, width), lambda i, j: (i, 0))

    def gate(branch):
        return pl.BlockSpec((None, GATE_RANK, tn), lambda i, j: (layer, 0, branch * nj + j))

    def wcol(width):
        return pl.BlockSpec((None, width, tn), lambda i, j: (layer, 0, j))

    return pl.pallas_call(
        _merge_kernel,
        grid=(R // tm, nj),
        in_specs=[pl.BlockSpec((tm, GATE_RANK), lambda i, j: (i, P_G // GATE_RANK)),
                  rows(A_WIDTH), rows(B_WIDTH), rows(C_WIDTH),
                  gate(0), gate(1), gate(2), wcol(A_WIDTH), wcol(B_WIDTH), wcol(C_WIDTH)],
        out_specs=pl.BlockSpec((tm, tn), lambda i, j: (i, j)),
        out_shape=jax.ShapeDtypeStruct((R, D), BF16),
        compiler_params=_params("parallel", "parallel"),
        name="merge",
    )(p, ya, yb, yc, gate_up, gate_up, gate_up, wa, wb, wc)


def _segment_flags(i, ctx_tiles, n_tiles):
    first = jnp.logical_or(i == 0, i == ctx_tiles)
    last = jnp.logical_or(i == ctx_tiles - 1, i == n_tiles - 1)
    return first, last


def _rwkv_prep_kernel(u_ref, prev_ref, next_ref, mu_ref, w0_ref, wup_ref, a0_ref, aup_ref, kk_ref, ka_ref,
                      r_out, v_out, kap_out, lw0_out, lw1_out, k0_out, k1_out, b0_out, b1_out, gs_out,
                      *, ctx_tiles, n_tiles):
    i = pl.program_id(0)
    first, last = _segment_flags(i, ctx_tiles, n_tiles)
    u = u_ref[...]
    tm = u.shape[0]
    rid = lax.broadcasted_iota(jnp.int32, (tm, 1), 0)
    prev_row = jnp.where(first, 0.0, prev_ref[SUBLANES - 1:SUBLANES, :])
    next_row = jnp.where(last, 0.0, next_ref[0:1, :])
    prev = jnp.where(rid == 0, prev_row, pltpu.roll(u, 1, axis=0))
    nxt = jnp.where(rid == tm - 1, next_row, pltpu.roll(u, tm - 1, axis=0))
    s = u + mu_ref[...] * (0.5 * (prev + nxt) - u)

    W = A_WIDTH
    r = s[:, 0:W]
    k = s[:, W:2 * W]
    v = s[:, 2 * W:3 * W]
    o = 3 * W
    wd = jnp.tanh(s[:, o:o + LANES]).astype(BF16)
    ad = s[:, o + LANES:o + 2 * LANES].astype(BF16)
    gd = s[:, A_GD_OFF:A_PAD]

    ones_bd = _head_block_ones()
    kk = k * kk_ref[...]
    nrm = jnp.sqrt(_head_sums(kk * kk, ones_bd))
    kk = kk / jnp.maximum(nrm, 1e-12)

    r_out[...] = r.astype(BF16)
    v_out[...] = v.astype(BF16)
    kap_out[...] = kk.astype(BF16)
    gs_out[...] = _sigmoid(gd).astype(BF16)
    for d, (lw_out, k_out, b_out) in enumerate(((lw0_out, k0_out, b0_out), (lw1_out, k1_out, b1_out))):
        z = w0_ref[d:d + 1, :] + _dot(wd, wup_ref[d])
        lw_out[...] = -math.exp(-0.5) * _sigmoid(z)
        a = _sigmoid(a0_ref[d:d + 1, :] + _dot(ad, aup_ref[d]))
        k_out[...] = (k * (1.0 + (a - 1.0) * ka_ref[...])).astype(BF16)
        b_out[...] = (kk * a).astype(BF16)


def _rwkv_prep(p, mu, w0, wup2, a0, aup2, k_k, k_a, *, n_ctx):
    R = p.shape[0]
    tm = _tile(math.gcd(n_ctx, R - n_ctx), 256, 16)
    n_tiles = R // tm
    hb = tm // SUBLANES
    n_hblocks = R // SUBLANES
    W = A_WIDTH

    def const(shape):
        return pl.BlockSpec(shape, lambda i: (0,) * len(shape))

    wide = pl.BlockSpec((tm, W), lambda i: (i, 0))
    f32w = jax.ShapeDtypeStruct((R, W), F32)
    bf16w = jax.ShapeDtypeStruct((R, W), BF16)
    return pl.pallas_call(
        functools.partial(_rwkv_prep_kernel, ctx_tiles=n_ctx // tm, n_tiles=n_tiles),
        grid=(n_tiles,),
        in_specs=[pl.BlockSpec((tm, A_PAD), lambda i: (i, 0)),
                  pl.BlockSpec((SUBLANES, A_PAD), lambda i: (jnp.maximum(i * hb - 1, 0), 0)),
                  pl.BlockSpec((SUBLANES, A_PAD), lambda i: (jnp.minimum((i + 1) * hb, n_hblocks - 1), 0)),
                  const((1, A_PAD)), const((2, W)), const((2, LANES, W)), const((2, W)), const((2, LANES, W)),
                  const((1, W)), const((1, W))],
        out_specs=[wide] * 9 + [pl.BlockSpec((tm, A_GD_PAD), lambda i: (i, 0))],
        out_shape=[bf16w] * 3 + [f32w] * 2 + [bf16w] * 4 + [jax.ShapeDtypeStruct((R, A_GD_PAD), BF16)],
        compiler_params=_params("parallel"),
        name="rwkv_prep",
    )(p, p, p, mu, w0, wup2, a0, aup2, k_k, k_a)


def _scan_kernel(lw_ref, k_ref, b_ref, kap_ref, v_ref, r_ref, y_ref, s_ref, *, reverse, npairs):
    C = SCAN_CHUNK
    N = A_HEAD_DIM

    @pl.when(pl.program_id(1) == 0)
    def _():
        s_ref[...] = jnp.zeros_like(s_ref)

    row = lax.broadcasted_iota(jnp.int32, (C, C), 0)
    col = lax.broadcasted_iota(jnp.int32, (C, C), 1)
    earlier = (col > row) if reverse else (col < row)
    diag = col == row
    incl = jnp.logical_or(earlier, diag)
    tri = jnp.where(incl, 1.0, 0.0).astype(BF16)
    tri3 = jnp.concatenate([tri, tri, tri], axis=1)
    eye = jnp.where(diag, 1.0, 0.0).astype(F32)

    def same_block(n):
        return (row // n) == (col // n)

    last = 0 if reverse else C - 1

    hs = (slice(0, N), slice(N, 2 * N))
    heads = range(2 * npairs)
    x, k_t, b_t, k_e, b_e, e_tot, v = ([] for _ in range(7))
    for p in range(npairs):
        sl = slice(p * LANES, (p + 1) * LANES)
        lw = lw_ref[:, sl]
        c = _dot(tri3, jnp.concatenate(_split3(lw), axis=0))
        ctot = c[last:last + 1, :]
        e_nc = jnp.exp(-c)
        e_tc = jnp.exp(ctot - c)
        kap_p = kap_ref[:, sl].astype(F32) * jnp.exp(c - lw)
        r_p = r_ref[:, sl].astype(F32) * jnp.exp(c)
        k_p = k_ref[:, sl].astype(F32)
        b_p = b_ref[:, sl].astype(F32)
        v_p = v_ref[:, sl]
        e_p = jnp.exp(ctot)
        for h in hs:
            x.append(jnp.concatenate([kap_p[:, h], r_p[:, h]], axis=0).astype(BF16))
            k_t.append((k_p * e_nc)[:, h].astype(BF16))
            b_t.append((b_p * e_nc)[:, h].astype(BF16))
            k_e.append((k_p * e_tc)[:, h].astype(BF16))
            b_e.append((b_p * e_tc)[:, h].astype(BF16))
            e_tot.append(e_p[:, h])
            v.append(v_p[:, h].astype(BF16))

    def bd(a, b):
        return _dot(a.astype(BF16), b.astype(BF16))

    g1 = [_dot_nt(x[i], k_t[i]) for i in heads]
    g2 = [_dot_nt(x[i], b_t[i]) for i in heads]
    a_kk = [jnp.where(earlier, g1[i][:C], 0.0).astype(BF16) for i in heads]
    a_rk = [jnp.where(incl, g1[i][C:], 0.0).astype(BF16) for i in heads]
    a_kb = [jnp.where(earlier, g2[i][:C], 0.0) for i in heads]
    a_rb = [jnp.where(incl, g2[i][C:], 0.0).astype(BF16) for i in heads]
    blk8 = same_block(8)
    a0 = [jnp.where(blk8, a_kb[i], 0.0) for i in heads]
    a2 = [bd(a0[i], a0[i]) for i in heads]
    a4 = [bd(a2[i], a2[i]) for i in heads]
    t = [bd(eye - a0[i], eye + a2[i]) for i in heads]
    t = [bd(t[i], eye + a4[i]) for i in heads]
    for n in (16, 32, 64):
        m = jnp.logical_and(same_block(n), jnp.logical_not(same_block(n // 2)))
        off = [jnp.where(m, a_kb[i], 0.0) for i in heads]
        ot = [bd(off[i], t[i]) for i in heads]
        t = [t[i] - bd(t[i], ot[i]) for i in heads]
    s0 = [s_ref[i] for i in heads]
    xs = [_dot_nt(x[i], s0[i].astype(BF16)) for i in heads]
    akv = [_dot(a_kk[i], v[i]) for i in heads]
    u = [bd(t[i], xs[i][:C] + akv[i]).astype(BF16) for i in heads]
    ys = [xs[i][C:] + _dot(a_rk[i], v[i]) - _dot(a_rb[i], u[i]) for i in heads]
    for i in heads:
        s_ref[i] = s0[i] * e_tot[i] + _dot_tn(v[i], k_e[i]) - _dot_tn(u[i], b_e[i])
    for p in range(npairs):
        y_ref[:, p * LANES:(p + 1) * LANES] = jnp.concatenate([ys[2 * p], ys[2 * p + 1]], axis=1)


def _delta_scan(lw, k, b, kap, v, r, *, n_ctx, reverse, pairs_per_block=None):
    R, W = lw.shape
    C = SCAN_CHUNK
    nchunks = R // C
    ctx_chunks = n_ctx // C
    npairs = W // LANES
    pb = npairs if pairs_per_block is None else pairs_per_block
    assert R % C == 0 and n_ctx % C == 0 and npairs % pb == 0

    if reverse:
        def rows(s):
            return jnp.where(s < ctx_chunks, ctx_chunks - 1 - s, nchunks - 1 - (s - ctx_chunks))
    else:
        def rows(s):
            return s

    spec = pl.BlockSpec((C, pb * LANES), lambda g, s: (rows(s), g))
    return pl.pallas_call(
        functools.partial(_scan_kernel, reverse=reverse, npairs=pb),
        grid=(npairs // pb, nchunks),
        in_specs=[spec] * 6,
        out_specs=spec,
        out_shape=jax.ShapeDtypeStruct((R, W), F32),
        scratch_shapes=[pltpu.VMEM((2 * pb, A_HEAD_DIM, A_HEAD_DIM), F32)],
        compiler_params=_params("parallel", "arbitrary"),
        name="delta_scan_rev" if reverse else "delta_scan_fwd",
    )(lw, k, b, kap, v, r)


def _rwkv_readout_kernel(yf_ref, yb_ref, r_ref, v_ref, k0_ref, k1_ref, gs_ref, gup_ref, rk_ref, lng_ref, lnb_ref,
                         o_ref):
    ones_bd = _head_block_ones()
    inv_n = 1.0 / A_HEAD_DIM
    ro = yf_ref[...] + yb_ref[...]
    mu = _head_sums(ro, ones_bd) * inv_n
    cen = ro - mu
    var = _head_sums(cen * cen, ones_bd) * inv_n
    yn = cen * lax.rsqrt(var + A_GN_EPS) * lng_ref[...] + lnb_ref[...]
    rk = r_ref[...].astype(F32) * (k0_ref[...].astype(F32) + k1_ref[...].astype(F32)) * rk_ref[...]
    bonus = _head_sums(rk, ones_bd) * v_ref[...].astype(F32)
    g = _dot(gs_ref[...], gup_ref[...])
    o_ref[...] = ((yn + bonus) * g).astype(o_ref.dtype)


def _rwkv_readout(yf, yb, r, v, k0, k1, gs, gup, r_k, ln_g, ln_b):
    R, W = yf.shape
    tm = _tile(R, 256, 16)
    wide = pl.BlockSpec((tm, W), lambda i: (i, 0))
    vec = pl.BlockSpec((1, W), lambda i: (0, 0))
    return pl.pallas_call(
        _rwkv_readout_kernel,
        grid=(R // tm,),
        in_specs=[wide] * 6 + [pl.BlockSpec((tm, A_GD_PAD), lambda i: (i, 0)),
                               pl.BlockSpec((A_GD_PAD, W), lambda i: (0, 0)), vec, vec, vec],
        out_specs=wide,
        out_shape=jax.ShapeDtypeStruct((R, W), BF16),
        compiler_params=_params("parallel"),
        name="rwkv_readout",
    )(yf, yb, r, v, k0, k1, gs, gup, r_k, ln_g, ln_b)


def _pool_kernel(u_ref, prev_ref, next_ref, w_ref, scale_ref, o_ref, ext_ref, *, ctx_tiles, n_tiles, n_ctx, n_lat):
    i = pl.program_id(0)
    first, last = _segment_flags(i, ctx_tiles, n_tiles)
    tm = u_ref.shape[0]
    H = POOL_HALO
    ext_ref[0:H, :] = jnp.where(first, 0.0, prev_ref[...])
    ext_ref[H:H + tm, :] = u_ref[...]
    ext_ref[H + tm:H + tm + H, :] = jnp.where(last, 0.0, next_ref[...])
    is_ctx = i < ctx_tiles
    seg_len = jnp.where(is_ctx, n_ctx, n_lat)
    t = lax.broadcasted_iota(jnp.int32, (tm, 1), 0) + i * tm - jnp.where(is_ctx, 0, n_ctx)
    for gi, win in enumerate(POOL_WINDOWS):
        cols = slice(gi * POOL_GROUP_W, (gi + 1) * POOL_GROUP_W)
        acc = None
        for o in range(-(win // 2), win // 2):
            term = ext_ref[H + o:H + o + tm, cols]
            acc = term if acc is None else acc + term
        lo = jnp.maximum(t - win // 2, 0)
        hi = jnp.minimum(t + win // 2 - 1, seg_len - 1)
        cnt = (hi - lo + 1).astype(F32)
        pooled = acc / cnt - u_ref[:, cols]
        y = _dot(pooled.astype(BF16), w_ref[gi]) * scale_ref[:, cols]
        o_ref[:, cols] = y.astype(o_ref.dtype)


def _pool(p, pool_w, pool_scale, *, n_ctx):
    R = p.shape[0]
    tm = _tile(math.gcd(n_ctx, R - n_ctx), 256, 16)
    n_tiles = R // tm
    hb = tm // POOL_HALO
    n_hblocks = R // POOL_HALO
    cb = P_B // B_WIDTH
    assert P_B % B_WIDTH == 0
    return pl.pallas_call(
        functools.partial(_pool_kernel, ctx_tiles=n_ctx // tm, n_tiles=n_tiles, n_ctx=n_ctx, n_lat=R - n_ctx),
        grid=(n_tiles,),
        in_specs=[pl.BlockSpec((tm, B_WIDTH), lambda i: (i, cb)),
                  pl.BlockSpec((POOL_HALO, B_WIDTH), lambda i: (jnp.maximum(i * hb - 1, 0), cb)),
                  pl.BlockSpec((POOL_HALO, B_WIDTH), lambda i: (jnp.minimum((i + 1) * hb, n_hblocks - 1), cb)),
                  pl.BlockSpec((len(POOL_WINDOWS), POOL_GROUP_W, POOL_GROUP_W), lambda i: (0, 0, 0)),
                  pl.BlockSpec((1, B_WIDTH), lambda i: (0, 0))],
        out_specs=pl.BlockSpec((tm, B_WIDTH), lambda i: (i, 0)),
        out_shape=jax.ShapeDtypeStruct((R, B_WIDTH), BF16),
        scratch_shapes=[pltpu.VMEM((tm + 2 * POOL_HALO, B_WIDTH), F32)],
        compiler_params=_params("parallel"),
        name="pool",
    )(p, p, p, pool_w, pool_scale)


def _rope(x, cos, sin):
    return x * cos + pltpu.roll(x, C_HEAD_DIM // 2, axis=1) * sin


def _rope_tables(n_ctx, n_lat):
    half = C_HEAD_DIM // 2
    t = jnp.arange(n_lat)
    row = (t // GRID_W).astype(F32)
    col = (t % GRID_W).astype(F32)
    inv = ROPE_BASE ** (-jnp.arange(0, half, 2, dtype=F32) / half)
    ar = row[:, None] * inv[None]
    ac = col[:, None] * inv[None]
    cos = jnp.concatenate([jnp.cos(ar), jnp.cos(ac), jnp.cos(ar), jnp.cos(ac)], axis=1)
    sin = jnp.concatenate([-jnp.sin(ar), -jnp.sin(ac), jnp.sin(ar), jnp.sin(ac)], axis=1)
    cos = jnp.concatenate([jnp.ones((n_ctx, C_HEAD_DIM), F32), cos], axis=0)
    sin = jnp.concatenate([jnp.zeros((n_ctx, C_HEAD_DIM), F32), sin], axis=0)
    return cos, sin


def _attn_kernel(q0_ref, q1_ref, q2_ref, q3_ref, kp_ref, kc_ref, kn_ref, kx_ref, vp_ref, vc_ref, vn_ref, vx_ref,
                 cp_ref, sp_ref, co_ref, so_ref, cn_ref, sn_ref, sink_ref, o_ref, *, ctx_qblocks, n_qblocks):
    i = pl.program_id(0)
    B = ATTN_BLOCK
    q_refs = (q0_ref, q1_ref, q2_ref, q3_ref)
    tq = q0_ref.shape[0]
    cos_o = co_ref[...]
    sin_o = so_ref[...]
    nloc = tq + 2 * B
    qrow = lax.broadcasted_iota(jnp.int32, (tq, nloc), 0)
    kcol = lax.broadcasted_iota(jnp.int32, (tq, nloc), 1)
    rel = kcol - B - qrow
    lo = jnp.where(i == ctx_qblocks, B, 0)
    hi = jnp.where(i < ctx_qblocks, 0, jnp.where(i == n_qblocks - 1, B + tq, nloc))
    bias = jnp.where(jnp.abs(rel) <= ATTN_BLOCK, 0.0, NEG_INF)
    bias = jnp.where(kcol >= lo, bias, NEG_INF)
    bias = jnp.where(kcol < hi, bias, NEG_INF)
    scale = C_HEAD_DIM ** -0.5
    for h in range(C_KV_HEADS):
        kc = slice(h * C_HEAD_DIM, (h + 1) * C_HEAD_DIM)
        k_loc = jnp.concatenate([_rope(kp_ref[:, kc], cp_ref[...], sp_ref[...]).astype(BF16),
                                 _rope(kc_ref[:, kc], cos_o, sin_o).astype(BF16),
                                 _rope(kn_ref[:, kc], cn_ref[...], sn_ref[...]).astype(BF16)], axis=0)
        v_loc = jnp.concatenate([vp_ref[:, kc].astype(BF16), vc_ref[:, kc].astype(BF16),
                                 vn_ref[:, kc].astype(BF16)], axis=0)
        k_ctx = kx_ref[:, kc].astype(BF16)
        v_ctx = vx_ref[:, kc].astype(BF16)
        for g in range(C_GROUP):
            j = h * C_GROUP + g
            cols = slice(j * C_HEAD_DIM, (j + 1) * C_HEAD_DIM)
            q = _rope(q_refs[h][:, g * C_HEAD_DIM:(g + 1) * C_HEAD_DIM], cos_o, sin_o).astype(BF16)
            s_loc = _dot_nt(q, k_loc) * scale + bias
            s_ctx = _dot_nt(q, k_ctx) * scale
            sk = sink_ref[j:j + 1, 0:1]
            m = jnp.maximum(jnp.maximum(jnp.max(s_loc, axis=-1, keepdims=True),
                                        jnp.max(s_ctx, axis=-1, keepdims=True)), sk)
            e_loc = jnp.exp(s_loc - m)
            e_ctx = jnp.exp(s_ctx - m)
            denom = (jnp.sum(e_loc, axis=-1, keepdims=True) + jnp.sum(e_ctx, axis=-1, keepdims=True)
                     + jnp.exp(sk - m))
            o = _dot(e_loc.astype(BF16), v_loc) + _dot(e_ctx.astype(BF16), v_ctx)
            o_ref[:, cols] = (o / denom).astype(o_ref.dtype)


def _attention(p, cos_tab, sin_tab, sink16, *, n_ctx):
    R = p.shape[0]
    B = ATTN_BLOCK
    tq = _tile(math.gcd(n_ctx, R - n_ctx), 2 * B, B)
    per = tq // B
    n_qblocks = R // tq
    n_blocks = R // B
    ctx_blocks = n_ctx // B
    W = C_KV_WIDTH
    assert n_ctx % B == 0 and R % B == 0 and P_Q % W == 0 and P_K % W == 0 and P_V % W == 0
    assert C_GROUP * C_HEAD_DIM == W
    kcol, vcol = P_K // W, P_V // W

    def prev_rows(i):
        return jnp.clip(i * per - 1, ctx_blocks, n_blocks - 1)

    def next_rows(i):
        return jnp.clip((i + 1) * per, ctx_blocks, n_blocks - 1)

    def kv_specs(c):
        return [pl.BlockSpec((B, W), lambda i: (prev_rows(i), c)), pl.BlockSpec((tq, W), lambda i: (i, c)),
                pl.BlockSpec((B, W), lambda i: (next_rows(i), c)), pl.BlockSpec((n_ctx, W), lambda i: (0, c))]

    tab_specs = []
    for rows, fn in ((B, prev_rows), (tq, lambda i: i), (B, next_rows)):
        tab_specs += [pl.BlockSpec((rows, C_HEAD_DIM), lambda i, fn=fn: (fn(i), 0))] * 2
    q_specs = [pl.BlockSpec((tq, W), lambda i, h=h: (i, P_Q // W + h)) for h in range(C_KV_HEADS)]

    return pl.pallas_call(
        functools.partial(_attn_kernel, ctx_qblocks=n_ctx // tq, n_qblocks=n_qblocks),
        grid=(n_qblocks,),
        in_specs=q_specs + kv_specs(kcol) + kv_specs(vcol) + tab_specs
        + [pl.BlockSpec((C_Q_HEADS, LANES), lambda i: (0, 0))],
        out_specs=pl.BlockSpec((tq, C_WIDTH), lambda i: (i, 0)),
        out_shape=jax.ShapeDtypeStruct((R, C_WIDTH), BF16),
        compiler_params=_params("parallel"),
        name="window_attention",
    )(*([p] * 12), cos_tab, sin_tab, cos_tab, sin_tab, cos_tab, sin_tab, sink16)


def _pad_to(a, axis, size):
    pad = [(0, 0)] * a.ndim
    pad[axis] = (0, size - a.shape[axis])
    return jnp.pad(a, pad)


def _relayout_w_in_kernel(w_ref, o_ref):
    quarter = C_HEAD_DIM // 4

    def copy_rows(dst, src, n):
        o_ref[dst:dst + n, :] = w_ref[src:src + n, :].astype(o_ref.dtype)

    def copy_heads(dst, src, width):
        for h in range(0, width, C_HEAD_DIM):
            for new, old in enumerate((0, 2, 1, 3)):
                copy_rows(dst + h + new * quarter, src + h + old * quarter, quarter)

    copy_rows(0, 0, OFF_B)
    o_ref[OFF_B:A_PAD, :] = jnp.zeros((A_PAD - OFF_B, o_ref.shape[1]), o_ref.dtype)
    copy_heads(P_K, OFF_K, C_KV_WIDTH)
    copy_rows(P_V, OFF_V, C_KV_WIDTH)
    copy_rows(P_B, OFF_B, B_WIDTH)
    copy_heads(P_Q, OFF_Q, C_WIDTH)
    copy_rows(P_G, OFF_G, GATE_RANK)


def _relayout_w_in(w_in):
    depth, D, n = w_in.shape
    assert n == IN_COLS
    w_t = jnp.swapaxes(w_in, 1, 2)
    tc = _tile(D, 256, LANES)
    return pl.pallas_call(
        _relayout_w_in_kernel,
        grid=(depth, D // tc),
        in_specs=[pl.BlockSpec((None, n, tc), lambda l, i: (l, 0, i))],
        out_specs=pl.BlockSpec((None, P_COLS, tc), lambda l, i: (l, 0, i)),
        out_shape=jax.ShapeDtypeStruct((depth, P_COLS, D), BF16),
        compiler_params=_params("parallel", "parallel"),
        name="w_in_relayout",
    )(w_t)


def _low_rank_pair(up):
    z = jnp.zeros_like(up[:, 0])
    return jnp.stack([jnp.concatenate([up[:, 0], z], axis=1), jnp.concatenate([z, up[:, 1]], axis=1)], axis=1).astype(BF16)


def kernel(x, c, ctx, c_ctx, mod_down, mod_up, mod_b, norm_g, w_in, shift_mu, rwkv_w0, rwkv_w_up, rwkv_a0,
           rwkv_a_up, rwkv_g_up, rwkv_k_k, rwkv_k_a, rwkv_r_k, rwkv_ln_g, rwkv_ln_b, pool_w, pool_scale,
           attn_sink, gate_up, w_branch_a, w_branch_b, w_branch_c, w_out, ffn_w1, ffn_w3, ffn_w2):
    assert x.shape[0] == 1 and ctx.shape[0] == 1 and c.shape[0] == 1
    depth = w_in.shape[0]
    T, D = x.shape[1], x.shape[2]
    L = ctx.shape[1]
    assert 2 * A_DECAY_RANK == LANES and 2 * A_ICLR_RANK == LANES

    w_in_p = _relayout_w_in(w_in)
    mu_p = _pad_to(shift_mu, -1, A_PAD)[:, None, :]
    wup2 = _low_rank_pair(rwkv_w_up)
    aup2 = _low_rank_pair(rwkv_a_up)
    gup_p = _pad_to(rwkv_g_up, 1, A_GD_PAD).astype(BF16)
    pool_w_b = pool_w.astype(BF16)
    gate_up_b = gate_up.astype(BF16)
    wa_b, wb_b, wc_b = (w.astype(BF16) for w in (w_branch_a, w_branch_b, w_branch_c))
    w2_b = ffn_w2.astype(BF16)
    sink16 = jnp.broadcast_to(attn_sink[..., None], (depth, C_Q_HEADS, LANES))
    cos_tab, sin_tab = _rope_tables(L, T)

    c8 = _pad_to(jnp.concatenate([c_ctx[None], c], axis=0), 0, SUBLANES)
    mod = _modulation(c8, mod_down, mod_up, mod_b).reshape(depth, SUBLANES, 6, D)

    def mod6(l, shift_i, scale_i, gate_i):
        m = mod[l]
        return jnp.stack([m[0, gate_i], m[1, gate_i], m[0, scale_i], m[1, scale_i], m[0, shift_i], m[1, shift_i]])

    xs, h = _join_norm(ctx[0], x[0], norm_g[0, 0][None], mod6(0, 0, 1, 2)[2:])
    for l in range(depth):
        p = _matmul(h, w_in_p, l, transposed_w=True, tn_cap=768, name="w_in")
        r, v, kap, lw0, lw1, k0, k1, b0, b1, gs = _rwkv_prep(
            p, mu_p[l], rwkv_w0[l], wup2[l], rwkv_a0[l], aup2[l], rwkv_k_k[l][None], rwkv_k_a[l][None], n_ctx=L)
        yf = _delta_scan(lw0, k0, b0, kap, v, r, n_ctx=L, reverse=False)
        yr = _delta_scan(lw1, k1, b1, kap, v, r, n_ctx=L, reverse=True)
        y_a = _rwkv_readout(yf, yr, r, v, k0, k1, gs, gup_p[l], rwkv_r_k[l][None], rwkv_ln_g[l][None],
                            rwkv_ln_b[l][None])
        y_b = _pool(p, pool_w_b[l], pool_scale[l][None], n_ctx=L)
        y_c = _attention(p, cos_tab, sin_tab, sink16[l], n_ctx=L)
        acc = _merge(p, y_a, y_b, y_c, gate_up_b, wa_b, wb_b, wc_b, l)
        mix = _matmul(acc, w_out, l, out_dtype=BF16, name="w_out")
        xs, h2 = _resid_norm(xs, mix, norm_g[l, 1:3], mod6(l, 3, 4, 2), n_ctx=L, emit_h=True)
        f = _matmul(_ffn_up(h2, ffn_w1, ffn_w3, l), w2_b, l, out_dtype=BF16, tk_cap=5504, name="ffn_down")
        if l + 1 < depth:
            g2 = jnp.stack([norm_g[l, 3], norm_g[l + 1, 0]])
            m6 = jnp.concatenate([mod6(l, 0, 1, 5)[:2], mod6(l + 1, 0, 1, 2)[2:]], axis=0)
            xs, h = _resid_norm(xs, f, g2, m6, n_ctx=L, emit_h=True)
        else:
            xs, _ = _resid_norm(xs, f, jnp.stack([norm_g[l, 3], norm_g[l, 3]]), mod6(l, 0, 1, 5), n_ctx=L,
                                emit_h=False, latent_only=True)
    return xs[None]
```

```python
import functools
import math

import jax
import jax.numpy as jnp
from jax import lax
from jax.experimental import pallas as pl
from jax.experimental.pallas import tpu as pltpu

F32 = jnp.float32
BF16 = jnp.bfloat16

LANES = 128
SUBLANES = 8
VMEM_LIMIT = 56 * 1024 * 1024

NORM_EPS = 1e-6
GRID_W = 64
ROPE_BASE = 10000.0
NEG_INF = -1e30

A_HEADS = 24
A_HEAD_DIM = 64
A_WIDTH = A_HEADS * A_HEAD_DIM
A_DECAY_RANK = 64
A_ICLR_RANK = 64
A_GATE_RANK = 224
A_GN_EPS = 64e-5
A_COLS = 3 * A_WIDTH + 2 * A_DECAY_RANK + 2 * A_ICLR_RANK + A_GATE_RANK
SCAN_CHUNK = 64
POOL_WINDOWS = (2, 4, 8, 16)
POOL_GROUP_W = 384
B_WIDTH = len(POOL_WINDOWS) * POOL_GROUP_W
POOL_HALO = 8
C_Q_HEADS = 16
C_KV_HEADS = 4
C_GROUP = C_Q_HEADS // C_KV_HEADS
C_HEAD_DIM = 128
C_WIDTH = C_Q_HEADS * C_HEAD_DIM
C_KV_WIDTH = C_KV_HEADS * C_HEAD_DIM
ATTN_BLOCK = 128
GATE_RANK = 256
N_BRANCH = 3

OFF_B = A_COLS
OFF_Q = OFF_B + B_WIDTH
OFF_K = OFF_Q + C_WIDTH
OFF_V = OFF_K + C_KV_WIDTH
OFF_G = OFF_V + C_KV_WIDTH
IN_COLS = OFF_G + GATE_RANK
A_PAD = 5120
A_GD_OFF = 3 * A_WIDTH + 2 * A_DECAY_RANK + 2 * A_ICLR_RANK
A_GD_PAD = A_PAD - A_GD_OFF
P_K = A_PAD
P_V = P_K + C_KV_WIDTH
P_B = P_V + C_KV_WIDTH
P_Q = P_B + B_WIDTH
P_G = P_Q + C_WIDTH
P_COLS = P_G + GATE_RANK


def _dot(a, b):
    return jnp.dot(a, b, preferred_element_type=F32)


def _dot_nt(a, b):
    return lax.dot_general(a, b, (((1,), (1,)), ((), ())), preferred_element_type=F32)


def _dot_tn(a, b):
    return lax.dot_general(a, b, (((0,), (0,)), ((), ())), preferred_element_type=F32)


def _tile(n, cap, mult):
    best = None
    for t in range(mult, min(n, cap) + 1, mult):
        if n % t == 0:
            best = t
    assert best is not None, (n, cap, mult)
    return best


def _params(*sem):
    return pltpu.CompilerParams(dimension_semantics=sem, vmem_limit_bytes=VMEM_LIMIT)


def _sigmoid(x):
    return 1.0 / (1.0 + jnp.exp(-x))


def _silu(x):
    return x * _sigmoid(x)


def _split3(x):
    hi = x.astype(BF16)
    rem = x - hi.astype(F32)
    mid = rem.astype(BF16)
    lo = (rem - mid.astype(F32)).astype(BF16)
    return hi, mid, lo


def _head_block_ones():
    r = lax.broadcasted_iota(jnp.int32, (3 * LANES, LANES), 0)
    c = lax.broadcasted_iota(jnp.int32, (3 * LANES, LANES), 1)
    return jnp.where(((r % LANES) // A_HEAD_DIM) == (c // A_HEAD_DIM), 1.0, 0.0).astype(BF16)


def _head_sums(x, ones_bd):
    out = []
    for j in range(0, x.shape[1], LANES):
        out.append(_dot(jnp.concatenate(_split3(x[:, j:j + LANES]), axis=1), ones_bd))
    return jnp.concatenate(out, axis=1)


def _mod_kernel(c_ref, down_ref, up_ref, b_ref, o_ref):
    s = _silu(c_ref[...]).astype(BF16)
    low = _dot(s, down_ref[0].astype(BF16)).astype(BF16)
    o_ref[0] = _dot(low, up_ref[0].astype(BF16)) + b_ref[0]


def _modulation(c8, down, up, bias):
    depth, d, rank = down.shape
    n = up.shape[2]
    tn = _tile(n, d, LANES)
    return pl.pallas_call(
        _mod_kernel,
        grid=(depth, n // tn),
        in_specs=[pl.BlockSpec((SUBLANES, d), lambda l, j: (0, 0)),
                  pl.BlockSpec((1, d, rank), lambda l, j: (l, 0, 0)),
                  pl.BlockSpec((1, rank, tn), lambda l, j: (l, 0, j)),
                  pl.BlockSpec((1, 1, tn), lambda l, j: (l, 0, j))],
        out_specs=pl.BlockSpec((1, SUBLANES, tn), lambda l, j: (l, 0, j)),
        out_shape=jax.ShapeDtypeStruct((depth, SUBLANES, n), F32),
        compiler_params=_params("arbitrary", "arbitrary"),
        name="modulation",
    )(c8, down, up, bias.reshape(depth, 1, n))


def _rms(x, g):
    return x * lax.rsqrt(jnp.mean(x * x, axis=-1, keepdims=True) + NORM_EPS) * g


def _resid_norm_kernel(*refs, ctx_tiles, has_m, emit_h):
    refs = list(refs)
    x_ref = refs.pop(0)
    m_ref = refs.pop(0) if has_m else None
    g_ref = refs.pop(0)
    mod_ref = refs.pop(0)
    is_ctx = pl.program_id(0) < ctx_tiles

    def pick(i):
        return jnp.where(is_ctx, mod_ref[i:i + 1, :], mod_ref[i + 1:i + 2, :])

    x = x_ref[...]
    if has_m:
        x = x + pick(0) * _rms(m_ref[...].astype(F32), g_ref[0:1, :])
        refs.pop(0)[...] = x
    if emit_h:
        h = _rms(x, g_ref[1:2, :]) * (1.0 + pick(2)) + pick(4)
        refs.pop(0)[...] = h.astype(BF16)


def _join_norm_kernel(ctx_ref, x_ref, g_ref, mod_ref, xs_ref, h_ref, *, ctx_tiles):
    is_ctx = pl.program_id(0) < ctx_tiles
    x = jnp.where(is_ctx, ctx_ref[...], x_ref[...])
    xs_ref[...] = x
    scale = jnp.where(is_ctx, mod_ref[0:1, :], mod_ref[1:2, :])
    shift = jnp.where(is_ctx, mod_ref[2:3, :], mod_ref[3:4, :])
    h_ref[...] = (_rms(x, g_ref[...]) * (1.0 + scale) + shift).astype(BF16)


def _join_norm(ctx, x, g, mod4):
    L, D = ctx.shape
    T = x.shape[0]
    te = _tile(math.gcd(L, T), 256, 16)
    ctx_tiles = L // te
    row = pl.BlockSpec((te, D), lambda i: (i, 0))
    return pl.pallas_call(
        functools.partial(_join_norm_kernel, ctx_tiles=ctx_tiles),
        grid=((L + T) // te,),
        in_specs=[pl.BlockSpec((te, D), lambda i: (jnp.minimum(i, ctx_tiles - 1), 0)),
                  pl.BlockSpec((te, D), lambda i: (jnp.maximum(i - ctx_tiles, 0), 0)),
                  pl.BlockSpec((1, D), lambda i: (0, 0)), pl.BlockSpec((4, D), lambda i: (0, 0))],
        out_specs=[row, row],
        out_shape=[jax.ShapeDtypeStruct((L + T, D), F32), jax.ShapeDtypeStruct((L + T, D), BF16)],
        compiler_params=_params("arbitrary"),
        name="join_norm",
    )(ctx, x, g, mod4)


def _resid_norm(x, m, g2, mod6, *, n_ctx, emit_h, latent_only=False):
    R, D = x.shape
    te = _tile(math.gcd(n_ctx, R - n_ctx), 256, 16)
    has_m = m is not None
    skip = n_ctx // te if latent_only else 0
    n_out = R - skip * te
    row_in = pl.BlockSpec((te, D), lambda i: (i + skip, 0))
    row = pl.BlockSpec((te, D), lambda i: (i, 0))
    ins = [x] + ([m] if has_m else []) + [g2, mod6]
    in_specs = [row_in] * (2 if has_m else 1) + [pl.BlockSpec((2, D), lambda i: (0, 0)),
                                                 pl.BlockSpec((6, D), lambda i: (0, 0))]
    out_shape, out_specs = [], []
    if has_m:
        out_shape.append(jax.ShapeDtypeStruct((n_out, D), F32))
        out_specs.append(row)
    if emit_h:
        out_shape.append(jax.ShapeDtypeStruct((n_out, D), BF16))
        out_specs.append(row)
    outs = pl.pallas_call(
        functools.partial(_resid_norm_kernel, ctx_tiles=n_ctx // te - skip, has_m=has_m, emit_h=emit_h),
        grid=(n_out // te,),
        in_specs=in_specs, out_specs=out_specs, out_shape=out_shape,
        compiler_params=_params("parallel"),
        name="resid_norm",
    )(*ins)
    outs = list(outs)
    x1 = outs.pop(0) if has_m else None
    h = outs.pop(0) if emit_h else None
    return x1, h


def _mm_kernel(x_ref, w_ref, o_ref, *scratch, nk, transposed_w):
    dot = _dot_nt if transposed_w else _dot
    part = dot(x_ref[...].astype(BF16), w_ref[...].astype(BF16))
    if nk == 1:
        o_ref[...] = part.astype(o_ref.dtype)
        return
    acc_ref, = scratch
    k = pl.program_id(2)

    @pl.when(k == 0)
    def _():
        acc_ref[...] = part

    @pl.when(k > 0)
    def _():
        acc_ref[...] += part

    @pl.when(k == nk - 1)
    def _():
        o_ref[...] = acc_ref[...].astype(o_ref.dtype)


def _matmul(x, w, layer, *, transposed_w=False, out_dtype=F32, tm_cap=1280, tn_cap=512, tk_cap=4096,
            name="matmul"):
    M, K = x.shape
    N = w.shape[1] if transposed_w else w.shape[2]
    tm = _tile(M, tm_cap, 16)
    tn = _tile(N, tn_cap, LANES)
    tk = _tile(K, tk_cap, LANES)
    nk = K // tk
    if transposed_w:
        w_spec = pl.BlockSpec((None, tn, tk), lambda i, j, k: (layer, j, k))
    else:
        w_spec = pl.BlockSpec((None, tk, tn), lambda i, j, k: (layer, k, j))
    return pl.pallas_call(
        functools.partial(_mm_kernel, nk=nk, transposed_w=transposed_w),
        grid=(M // tm, N // tn, nk),
        in_specs=[pl.BlockSpec((tm, tk), lambda i, j, k: (i, k)), w_spec],
        out_specs=pl.BlockSpec((tm, tn), lambda i, j, k: (i, j)),
        out_shape=jax.ShapeDtypeStruct((M, N), out_dtype),
        scratch_shapes=[pltpu.VMEM((tm, tn), F32)] if nk > 1 else [],
        compiler_params=_params("parallel", "parallel", "arbitrary"),
        name=name,
    )(x, w)


def _ffn_up_kernel(h_ref, w1_ref, w3_ref, o_ref):
    h = h_ref[...]
    o_ref[...] = (_silu(_dot(h, w1_ref[...].astype(BF16))) * _dot(h, w3_ref[...].astype(BF16))).astype(o_ref.dtype)


def _ffn_up(h, w1, w3, layer):
    M, K = h.shape
    N = w1.shape[2]
    tm = _tile(M, 1280, 16)
    tn = _tile(N, 256, LANES)
    wspec = pl.BlockSpec((None, K, tn), lambda i, j: (layer, 0, j))
    return pl.pallas_call(
        _ffn_up_kernel,
        grid=(M // tm, N // tn),
        in_specs=[pl.BlockSpec((tm, K), lambda i, j: (i, 0)), wspec, wspec],
        out_specs=pl.BlockSpec((tm, tn), lambda i, j: (i, j)),
        out_shape=jax.ShapeDtypeStruct((M, N), BF16),
        compiler_params=_params("parallel", "parallel"),
        name="ffn_up",
    )(h, w1, w3)


def _merge_kernel(pg_ref, ya_ref, yb_ref, yc_ref, ga_ref, gb_ref, gc_ref, wa_ref, wb_ref, wc_ref, o_ref):
    pg = pg_ref[...].astype(BF16)
    acc = _sigmoid(_dot(pg, ga_ref[...])) * _dot(ya_ref[...], wa_ref[...])
    acc += _sigmoid(_dot(pg, gb_ref[...])) * _dot(yb_ref[...], wb_ref[...])
    acc += _sigmoid(_dot(pg, gc_ref[...])) * _dot(yc_ref[...], wc_ref[...])
    o_ref[...] = acc.astype(o_ref.dtype)


def _merge(p, ya, yb, yc, gate_up, wa, wb, wc, layer):
    R = p.shape[0]
    D = wa.shape[2]
    tm = _tile(R, 1280, 16)
    tn = _tile(D, 512, LANES)
    nj = D // tn

    def rows(width):
        return pl.BlockSpec((tm, width), lambda i, j: (i, 0))

    def gate(branch):
        return pl.BlockSpec((None, GATE_RANK, tn), lambda i, j: (layer, 0, branch * nj + j))

    def wcol(width):
        return pl.BlockSpec((None, width, tn), lambda i, j: (layer, 0, j))

    return pl.pallas_call(
        _merge_kernel,
        grid=(R // tm, nj),
        in_specs=[pl.BlockSpec((tm, GATE_RANK), lambda i, j: (i, P_G // GATE_RANK)),
                  rows(A_WIDTH), rows(B_WIDTH), rows(C_WIDTH),
                  gate(0), gate(1), gate(2), wcol(A_WIDTH), wcol(B_WIDTH), wcol(C_WIDTH)],
        out_specs=pl.BlockSpec((tm, tn), lambda i, j: (i, j)),
        out_shape=jax.ShapeDtypeStruct((R, D), BF16),
        compiler_params=_params("parallel", "parallel"),
        name="merge",
    )(p, ya, yb, yc, gate_up, gate_up, gate_up, wa, wb, wc)


def _segment_flags(i, ctx_tiles, n_tiles):
    first = jnp.logical_or(i == 0, i == ctx_tiles)
    last = jnp.logical_or(i == ctx_tiles - 1, i == n_tiles - 1)
    return first, last


def _rwkv_prep_kernel(u_ref, prev_ref, next_ref, mu_ref, w0_ref, wup_ref, a0_ref, aup_ref, kk_ref, ka_ref,
                      r_out, v_out, kap_out, lw0_out, lw1_out, k0_out, k1_out, b0_out, b1_out, gs_out,
                      *, ctx_tiles, n_tiles):
    i = pl.program_id(0)
    first, last = _segment_flags(i, ctx_tiles, n_tiles)
    u = u_ref[...]
    tm = u.shape[0]
    rid = lax.broadcasted_iota(jnp.int32, (tm, 1), 0)
    prev_row = jnp.where(first, 0.0, prev_ref[SUBLANES - 1:SUBLANES, :])
    next_row = jnp.where(last, 0.0, next_ref[0:1, :])
    prev = jnp.where(rid == 0, prev_row, pltpu.roll(u, 1, axis=0))
    nxt = jnp.where(rid == tm - 1, next_row, pltpu.roll(u, tm - 1, axis=0))
    s = u + mu_ref[...] * (0.5 * (prev + nxt) - u)

    W = A_WIDTH
    r = s[:, 0:W]
    k = s[:, W:2 * W]
    v = s[:, 2 * W:3 * W]
    o = 3 * W
    wd = jnp.tanh(s[:, o:o + LANES]).astype(BF16)
    ad = s[:, o + LANES:o + 2 * LANES].astype(BF16)
    gd = s[:, A_GD_OFF:A_PAD]

    ones_bd = _head_block_ones()
    kk = k * kk_ref[...]
    nrm = jnp.sqrt(_head_sums(kk * kk, ones_bd))
    kk = kk / jnp.maximum(nrm, 1e-12)

    r_out[...] = r.astype(BF16)
    v_out[...] = v.astype(BF16)
    kap_out[...] = kk.astype(BF16)
    gs_out[...] = _sigmoid(gd).astype(BF16)
    for d, (lw_out, k_out, b_out) in enumerate(((lw0_out, k0_out, b0_out), (lw1_out, k1_out, b1_out))):
        z = w0_ref[d:d + 1, :] + _dot(wd, wup_ref[d])
        lw_out[...] = -math.exp(-0.5) * _sigmoid(z)
        a = _sigmoid(a0_ref[d:d + 1, :] + _dot(ad, aup_ref[d]))
        k_out[...] = (k * (1.0 + (a - 1.0) * ka_ref[...])).astype(BF16)
        b_out[...] = (kk * a).astype(BF16)


def _rwkv_prep(p, mu, w0, wup2, a0, aup2, k_k, k_a, *, n_ctx):
    R = p.shape[0]
    tm = _tile(math.gcd(n_ctx, R - n_ctx), 256, 16)
    n_tiles = R // tm
    hb = tm // SUBLANES
    n_hblocks = R // SUBLANES
    W = A_WIDTH

    def const(shape):
        return pl.BlockSpec(shape, lambda i: (0,) * len(shape))

    wide = pl.BlockSpec((tm, W), lambda i: (i, 0))
    f32w = jax.ShapeDtypeStruct((R, W), F32)
    bf16w = jax.ShapeDtypeStruct((R, W), BF16)
    return pl.pallas_call(
        functools.partial(_rwkv_prep_kernel, ctx_tiles=n_ctx // tm, n_tiles=n_tiles),
        grid=(n_tiles,),
        in_specs=[pl.BlockSpec((tm, A_PAD), lambda i: (i, 0)),
                  pl.BlockSpec((SUBLANES, A_PAD), lambda i: (jnp.maximum(i * hb - 1, 0), 0)),
                  pl.BlockSpec((SUBLANES, A_PAD), lambda i: (jnp.minimum((i + 1) * hb, n_hblocks - 1), 0)),
                  const((1, A_PAD)), const((2, W)), const((2, LANES, W)), const((2, W)), const((2, LANES, W)),
                  const((1, W)), const((1, W))],
        out_specs=[wide] * 9 + [pl.BlockSpec((tm, A_GD_PAD), lambda i: (i, 0))],
        out_shape=[bf16w] * 3 + [f32w] * 2 + [bf16w] * 4 + [jax.ShapeDtypeStruct((R, A_GD_PAD), BF16)],
        compiler_params=_params("parallel"),
        name="rwkv_prep",
    )(p, p, p, mu, w0, wup2, a0, aup2, k_k, k_a)


def _scan_kernel(lw_ref, k_ref, b_ref, kap_ref, v_ref, r_ref, y_ref, s_ref, *, reverse, npairs):
    C = SCAN_CHUNK
    N = A_HEAD_DIM

    @pl.when(pl.program_id(1) == 0)
    def _():
        s_ref[...] = jnp.zeros_like(s_ref)

    row = lax.broadcasted_iota(jnp.int32, (C, C), 0)
    col = lax.broadcasted_iota(jnp.int32, (C, C), 1)
    earlier = (col > row) if reverse else (col < row)
    diag = col == row
    incl = jnp.logical_or(earlier, diag)
    tri = jnp.where(incl, 1.0, 0.0).astype(BF16)
    tri3 = jnp.concatenate([tri, tri, tri], axis=1)
    eye = jnp.where(diag, 1.0, 0.0).astype(F32)

    def same_block(n):
        return (row // n) == (col // n)

    last = 0 if reverse else C - 1

    hs = (slice(0, N), slice(N, 2 * N))
    heads = range(2 * npairs)
    x, k_t, b_t, k_e, b_e, e_tot, v = ([] for _ in range(7))
    for p in range(npairs):
        sl = slice(p * LANES, (p + 1) * LANES)
        lw = lw_ref[:, sl]
        c = _dot(tri3, jnp.concatenate(_split3(lw), axis=0))
        ctot = c[last:last + 1, :]
        e_nc = jnp.exp(-c)
        e_tc = jnp.exp(ctot - c)
        kap_p = kap_ref[:, sl].astype(F32) * jnp.exp(c - lw)
        r_p = r_ref[:, sl].astype(F32) * jnp.exp(c)
        k_p = k_ref[:, sl].astype(F32)
        b_p = b_ref[:, sl].astype(F32)
        v_p = v_ref[:, sl]
        e_p = jnp.exp(ctot)
        for h in hs:
            x.append(jnp.concatenate([kap_p[:, h], r_p[:, h]], axis=0).astype(BF16))
            k_t.append((k_p * e_nc)[:, h].astype(BF16))
            b_t.append((b_p * e_nc)[:, h].astype(BF16))
            k_e.append((k_p * e_tc)[:, h].astype(BF16))
            b_e.append((b_p * e_tc)[:, h].astype(BF16))
            e_tot.append(e_p[:, h])
            v.append(v_p[:, h].astype(BF16))

    def bd(a, b):
        return _dot(a.astype(BF16), b.astype(BF16))

    g1 = [_dot_nt(x[i], k_t[i]) for i in heads]
    g2 = [_dot_nt(x[i], b_t[i]) for i in heads]
    a_kk = [jnp.where(earlier, g1[i][:C], 0.0).astype(BF16) for i in heads]
    a_rk = [jnp.where(incl, g1[i][C:], 0.0).astype(BF16) for i in heads]
    a_kb = [jnp.where(earlier, g2[i][:C], 0.0) for i in heads]
    a_rb = [jnp.where(incl, g2[i][C:], 0.0).astype(BF16) for i in heads]
    blk8 = same_block(8)
    a0 = [jnp.where(blk8, a_kb[i], 0.0) for i in heads]
    a2 = [bd(a0[i], a0[i]) for i in heads]
    a4 = [bd(a2[i], a2[i]) for i in heads]
    t = [bd(eye - a0[i], eye + a2[i]) for i in heads]
    t = [bd(t[i], eye + a4[i]) for i in heads]
    for n in (16, 32, 64):
        m = jnp.logical_and(same_block(n), jnp.logical_not(same_block(n // 2)))
        off = [jnp.where(m, a_kb[i], 0.0) for i in heads]
        ot = [bd(off[i], t[i]) for i in heads]
        t = [t[i] - bd(t[i], ot[i]) for i in heads]
    s0 = [s_ref[i] for i in heads]
    xs = [_dot_nt(x[i], s0[i].astype(BF16)) for i in heads]
    akv = [_dot(a_kk[i], v[i]) for i in heads]
    u = [bd(t[i], xs[i][:C] + akv[i]).astype(BF16) for i in heads]
    ys = [xs[i][C:] + _dot(a_rk[i], v[i]) - _dot(a_rb[i], u[i]) for i in heads]
    for i in heads:
        s_ref[i] = s0[i] * e_tot[i] + _dot_tn(v[i], k_e[i]) - _dot_tn(u[i], b_e[i])
    for p in range(npairs):
        y_ref[:, p * LANES:(p + 1) * LANES] = jnp.concatenate([ys[2 * p], ys[2 * p + 1]], axis=1)


def _delta_scan(lw, k, b, kap, v, r, *, n_ctx, reverse, pairs_per_block=None):
    R, W = lw.shape
    C = SCAN_CHUNK
    nchunks = R // C
    ctx_chunks = n_ctx // C
    npairs = W // LANES
    pb = npairs if pairs_per_block is None else pairs_per_block
    assert R % C == 0 and n_ctx % C == 0 and npairs % pb == 0

    if reverse:
        def rows(s):
            return jnp.where(s < ctx_chunks, ctx_chunks - 1 - s, nchunks - 1 - (s - ctx_chunks))
    else:
        def rows(s):
            return s

    spec = pl.BlockSpec((C, pb * LANES), lambda g, s: (rows(s), g))
    return pl.pallas_call(
        functools.partial(_scan_kernel, reverse=reverse, npairs=pb),
        grid=(npairs // pb, nchunks),
        in_specs=[spec] * 6,
        out_specs=spec,
        out_shape=jax.ShapeDtypeStruct((R, W), F32),
        scratch_shapes=[pltpu.VMEM((2 * pb, A_HEAD_DIM, A_HEAD_DIM), F32)],
        compiler_params=_params("parallel", "arbitrary"),
        name="delta_scan_rev" if reverse else "delta_scan_fwd",
    )(lw, k, b, kap, v, r)


def _rwkv_readout_kernel(yf_ref, yb_ref, r_ref, v_ref, k0_ref, k1_ref, gs_ref, gup_ref, rk_ref, lng_ref, lnb_ref,
                         o_ref):
    ones_bd = _head_block_ones()
    inv_n = 1.0 / A_HEAD_DIM
    ro = yf_ref[...] + yb_ref[...]
    mu = _head_sums(ro, ones_bd) * inv_n
    cen = ro - mu
    var = _head_sums(cen * cen, ones_bd) * inv_n
    yn = cen * lax.rsqrt(var + A_GN_EPS) * lng_ref[...] + lnb_ref[...]
    rk = r_ref[...].astype(F32) * (k0_ref[...].astype(F32) + k1_ref[...].astype(F32)) * rk_ref[...]
    bonus = _head_sums(rk, ones_bd) * v_ref[...].astype(F32)
    g = _dot(gs_ref[...], gup_ref[...])
    o_ref[...] = ((yn + bonus) * g).astype(o_ref.dtype)


def _rwkv_readout(yf, yb, r, v, k0, k1, gs, gup, r_k, ln_g, ln_b):
    R, W = yf.shape
    tm = _tile(R, 256, 16)
    wide = pl.BlockSpec((tm, W), lambda i: (i, 0))
    vec = pl.BlockSpec((1, W), lambda i: (0, 0))
    return pl.pallas_call(
        _rwkv_readout_kernel,
        grid=(R // tm,),
        in_specs=[wide] * 6 + [pl.BlockSpec((tm, A_GD_PAD), lambda i: (i, 0)),
                               pl.BlockSpec((A_GD_PAD, W), lambda i: (0, 0)), vec, vec, vec],
        out_specs=wide,
        out_shape=jax.ShapeDtypeStruct((R, W), BF16),
        compiler_params=_params("parallel"),
        name="rwkv_readout",
    )(yf, yb, r, v, k0, k1, gs, gup, r_k, ln_g, ln_b)


def _pool_kernel(u_ref, prev_ref, next_ref, w_ref, scale_ref, o_ref, ext_ref, *, ctx_tiles, n_tiles, n_ctx, n_lat):
    i = pl.program_id(0)
    first, last = _segment_flags(i, ctx_tiles, n_tiles)
    tm = u_ref.shape[0]
    H = POOL_HALO
    ext_ref[0:H, :] = jnp.where(first, 0.0, prev_ref[...])
    ext_ref[H:H + tm, :] = u_ref[...]
    ext_ref[H + tm:H + tm + H, :] = jnp.where(last, 0.0, next_ref[...])
    is_ctx = i < ctx_tiles
    seg_len = jnp.where(is_ctx, n_ctx, n_lat)
    t = lax.broadcasted_iota(jnp.int32, (tm, 1), 0) + i * tm - jnp.where(is_ctx, 0, n_ctx)
    for gi, win in enumerate(POOL_WINDOWS):
        cols = slice(gi * POOL_GROUP_W, (gi + 1) * POOL_GROUP_W)
        acc = None
        for o in range(-(win // 2), win // 2):
            term = ext_ref[H + o:H + o + tm, cols]
            acc = term if acc is None else acc + term
        lo = jnp.maximum(t - win // 2, 0)
        hi = jnp.minimum(t + win // 2 - 1, seg_len - 1)
        cnt = (hi - lo + 1).astype(F32)
        pooled = acc / cnt - u_ref[:, cols]
        y = _dot(pooled.astype(BF16), w_ref[gi]) * scale_ref[:, cols]
        o_ref[:, cols] = y.astype(o_ref.dtype)


def _pool(p, pool_w, pool_scale, *, n_ctx):
    R = p.shape[0]
    tm = _tile(math.gcd(n_ctx, R - n_ctx), 256, 16)
    n_tiles = R // tm
    hb = tm // POOL_HALO
    n_hblocks = R // POOL_HALO
    cb = P_B // B_WIDTH
    assert P_B % B_WIDTH == 0
    return pl.pallas_call(
        functools.partial(_pool_kernel, ctx_tiles=n_ctx // tm, n_tiles=n_tiles, n_ctx=n_ctx, n_lat=R - n_ctx),
        grid=(n_tiles,),
        in_specs=[pl.BlockSpec((tm, B_WIDTH), lambda i: (i, cb)),
                  pl.BlockSpec((POOL_HALO, B_WIDTH), lambda i: (jnp.maximum(i * hb - 1, 0), cb)),
                  pl.BlockSpec((POOL_HALO, B_WIDTH), lambda i: (jnp.minimum((i + 1) * hb, n_hblocks - 1), cb)),
                  pl.BlockSpec((len(POOL_WINDOWS), POOL_GROUP_W, POOL_GROUP_W), lambda i: (0, 0, 0)),
                  pl.BlockSpec((1, B_WIDTH), lambda i: (0, 0))],
        out_specs=pl.BlockSpec((tm, B_WIDTH), lambda i: (i, 0)),
        out_shape=jax.ShapeDtypeStruct((R, B_WIDTH), BF16),
        scratch_shapes=[pltpu.VMEM((tm + 2 * POOL_HALO, B_WIDTH), F32)],
        compiler_params=_params("parallel"),
        name="pool",
    )(p, p, p, pool_w, pool_scale)


def _rope(x, cos, sin):
    return x * cos + pltpu.roll(x, C_HEAD_DIM // 2, axis=1) * sin


def _rope_tables(n_ctx, n_lat):
    half = C_HEAD_DIM // 2
    t = jnp.arange(n_lat)
    row = (t // GRID_W).astype(F32)
    col = (t % GRID_W).astype(F32)
    inv = ROPE_BASE ** (-jnp.arange(0, half, 2, dtype=F32) / half)
    ar = row[:, None] * inv[None]
    ac = col[:, None] * inv[None]
    cos = jnp.concatenate([jnp.cos(ar), jnp.cos(ac), jnp.cos(ar), jnp.cos(ac)], axis=1)
    sin = jnp.concatenate([-jnp.sin(ar), -jnp.sin(ac), jnp.sin(ar), jnp.sin(ac)], axis=1)
    cos = jnp.concatenate([jnp.ones((n_ctx, C_HEAD_DIM), F32), cos], axis=0)
    sin = jnp.concatenate([jnp.zeros((n_ctx, C_HEAD_DIM), F32), sin], axis=0)
    return cos, sin


def _attn_kernel(q0_ref, q1_ref, q2_ref, q3_ref, kp_ref, kc_ref, kn_ref, kx_ref, vp_ref, vc_ref, vn_ref, vx_ref,
                 cp_ref, sp_ref, co_ref, so_ref, cn_ref, sn_ref, sink_ref, o_ref, *, ctx_qblocks, n_qblocks):
    i = pl.program_id(0)
    B = ATTN_BLOCK
    q_refs = (q0_ref, q1_ref, q2_ref, q3_ref)
    tq = q0_ref.shape[0]
    cos_o = co_ref[...]
    sin_o = so_ref[...]
    nloc = tq + 2 * B
    qrow = lax.broadcasted_iota(jnp.int32, (tq, nloc), 0)
    kcol = lax.broadcasted_iota(jnp.int32, (tq, nloc), 1)
    rel = kcol - B - qrow
    lo = jnp.where(i == ctx_qblocks, B, 0)
    hi = jnp.where(i < ctx_qblocks, 0, jnp.where(i == n_qblocks - 1, B + tq, nloc))
    bias = jnp.where(jnp.abs(rel) <= ATTN_BLOCK, 0.0, NEG_INF)
    bias = jnp.where(kcol >= lo, bias, NEG_INF)
    bias = jnp.where(kcol < hi, bias, NEG_INF)
    scale = C_HEAD_DIM ** -0.5
    for h in range(C_KV_HEADS):
        kc = slice(h * C_HEAD_DIM, (h + 1) * C_HEAD_DIM)
        k_loc = jnp.concatenate([_rope(kp_ref[:, kc], cp_ref[...], sp_ref[...]).astype(BF16),
                                 _rope(kc_ref[:, kc], cos_o, sin_o).astype(BF16),
                                 _rope(kn_ref[:, kc], cn_ref[...], sn_ref[...]).astype(BF16)], axis=0)
        v_loc = jnp.concatenate([vp_ref[:, kc].astype(BF16), vc_ref[:, kc].astype(BF16),
                                 vn_ref[:, kc].astype(BF16)], axis=0)
        k_ctx = kx_ref[:, kc].astype(BF16)
        v_ctx = vx_ref[:, kc].astype(BF16)
        for g in range(C_GROUP):
            j = h * C_GROUP + g
            cols = slice(j * C_HEAD_DIM, (j + 1) * C_HEAD_DIM)
            q = _rope(q_refs[h][:, g * C_HEAD_DIM:(g + 1) * C_HEAD_DIM], cos_o, sin_o).astype(BF16)
            s_loc = _dot_nt(q, k_loc) * scale + bias
            s_ctx = _dot_nt(q, k_ctx) * scale
            sk = sink_ref[j:j + 1, 0:1]
            m = jnp.maximum(jnp.maximum(jnp.max(s_loc, axis=-1, keepdims=True),
                                        jnp.max(s_ctx, axis=-1, keepdims=True)), sk)
            e_loc = jnp.exp(s_loc - m)
            e_ctx = jnp.exp(s_ctx - m)
            denom = (jnp.sum(e_loc, axis=-1, keepdims=True) + jnp.sum(e_ctx, axis=-1, keepdims=True)
                     + jnp.exp(sk - m))
            o = _dot(e_loc.astype(BF16), v_loc) + _dot(e_ctx.astype(BF16), v_ctx)
            o_ref[:, cols] = (o / denom).astype(o_ref.dtype)


def _attention(p, cos_tab, sin_tab, sink16, *, n_ctx):
    R = p.shape[0]
    B = ATTN_BLOCK
    tq = _tile(math.gcd(n_ctx, R - n_ctx), 2 * B, B)
    per = tq // B
    n_qblocks = R // tq
    n_blocks = R // B
    ctx_blocks = n_ctx // B
    W = C_KV_WIDTH
    assert n_ctx % B == 0 and R % B == 0 and P_Q % W == 0 and P_K % W == 0 and P_V % W == 0
    assert C_GROUP * C_HEAD_DIM == W
    kcol, vcol = P_K // W, P_V // W

    def prev_rows(i):
        return jnp.clip(i * per - 1, ctx_blocks, n_blocks - 1)

    def next_rows(i):
        return jnp.clip((i + 1) * per, ctx_blocks, n_blocks - 1)

    def kv_specs(c):
        return [pl.BlockSpec((B, W), lambda i: (prev_rows(i), c)), pl.BlockSpec((tq, W), lambda i: (i, c)),
                pl.BlockSpec((B, W), lambda i: (next_rows(i), c)), pl.BlockSpec((n_ctx, W), lambda i: (0, c))]

    tab_specs = []
    for rows, fn in ((B, prev_rows), (tq, lambda i: i), (B, next_rows)):
        tab_specs += [pl.BlockSpec((rows, C_HEAD_DIM), lambda i, fn=fn: (fn(i), 0))] * 2
    q_specs = [pl.BlockSpec((tq, W), lambda i, h=h: (i, P_Q // W + h)) for h in range(C_KV_HEADS)]

    return pl.pallas_call(
        functools.partial(_attn_kernel, ctx_qblocks=n_ctx // tq, n_qblocks=n_qblocks),
        grid=(n_qblocks,),
        in_specs=q_specs + kv_specs(kcol) + kv_specs(vcol) + tab_specs
        + [pl.BlockSpec((C_Q_HEADS, LANES), lambda i: (0, 0))],
        out_specs=pl.BlockSpec((tq, C_WIDTH), lambda i: (i, 0)),
        out_shape=jax.ShapeDtypeStruct((R, C_WIDTH), BF16),
        compiler_params=_params("parallel"),
        name="window_attention",
    )(*([p] * 12), cos_tab, sin_tab, cos_tab, sin_tab, cos_tab, sin_tab, sink16)


def _pad_to(a, axis, size):
    pad = [(0, 0)] * a.ndim
    pad[axis] = (0, size - a.shape[axis])
    return jnp.pad(a, pad)


def _relayout_w_in_kernel(w_ref, o_ref):
    quarter = C_HEAD_DIM // 4

    def copy_rows(dst, src, n):
        o_ref[dst:dst + n, :] = w_ref[src:src + n, :].astype(o_ref.dtype)

    def copy_heads(dst, src, width):
        for h in range(0, width, C_HEAD_DIM):
            for new, old in enumerate((0, 2, 1, 3)):
                copy_rows(dst + h + new * quarter, src + h + old * quarter, quarter)

    copy_rows(0, 0, OFF_B)
    o_ref[OFF_B:A_PAD, :] = jnp.zeros((A_PAD - OFF_B, o_ref.shape[1]), o_ref.dtype)
    copy_heads(P_K, OFF_K, C_KV_WIDTH)
    copy_rows(P_V, OFF_V, C_KV_WIDTH)
    copy_rows(P_B, OFF_B, B_WIDTH)
    copy_heads(P_Q, OFF_Q, C_WIDTH)
    copy_rows(P_G, OFF_G, GATE_RANK)


def _relayout_w_in(w_in):
    depth, D, n = w_in.shape
    assert n == IN_COLS
    w_t = jnp.swapaxes(w_in, 1, 2)
    tc = _tile(D, 256, LANES)
    return pl.pallas_call(
        _relayout_w_in_kernel,
        grid=(depth, D // tc),
        in_specs=[pl.BlockSpec((None, n, tc), lambda l, i: (l, 0, i))],
        out_specs=pl.BlockSpec((None, P_COLS, tc), lambda l, i: (l, 0, i)),
        out_shape=jax.ShapeDtypeStruct((depth, P_COLS, D), BF16),
        compiler_params=_params("parallel", "parallel"),
        name="w_in_relayout",
    )(w_t)


def _low_rank_pair(up):
    z = jnp.zeros_like(up[:, 0])
    return jnp.stack([jnp.concatenate([up[:, 0], z], axis=1), jnp.concatenate([z, up[:, 1]], axis=1)], axis=1).astype(BF16)


def kernel(x, c, ctx, c_ctx, mod_down, mod_up, mod_b, norm_g, w_in, shift_mu, rwkv_w0, rwkv_w_up, rwkv_a0,
           rwkv_a_up, rwkv_g_up, rwkv_k_k, rwkv_k_a, rwkv_r_k, rwkv_ln_g, rwkv_ln_b, pool_w, pool_scale,
           attn_sink, gate_up, w_branch_a, w_branch_b, w_branch_c, w_out, ffn_w1, ffn_w3, ffn_w2):
    assert x.shape[0] == 1 and ctx.shape[0] == 1 and c.shape[0] == 1
    depth = w_in.shape[0]
    T, D = x.shape[1], x.shape[2]
    L = ctx.shape[1]
    d_ff = ffn_w2.shape[1]
    assert 2 * A_DECAY_RANK == LANES and 2 * A_ICLR_RANK == LANES

    w_in_p = _relayout_w_in(w_in)
    mu_p = _pad_to(shift_mu, -1, A_PAD)[:, None, :]
    wup2 = _low_rank_pair(rwkv_w_up)
    aup2 = _low_rank_pair(rwkv_a_up)
    gup_p = _pad_to(rwkv_g_up, 1, A_GD_PAD).astype(BF16)
    pool_w_b = pool_w.astype(BF16)
    gate_up_b = gate_up.astype(BF16)
    wa_b, wb_b, wc_b = (w.astype(BF16) for w in (w_branch_a, w_branch_b, w_branch_c))
    w2_b = ffn_w2.astype(BF16)
    sink16 = jnp.broadcast_to(attn_sink[..., None], (depth, C_Q_HEADS, LANES))
    cos_tab, sin_tab = _rope_tables(L, T)

    c8 = _pad_to(jnp.concatenate([c_ctx[None], c], axis=0), 0, SUBLANES)
    mod = _modulation(c8, mod_down, mod_up, mod_b).reshape(depth, SUBLANES, 6, D)

    def mod6(l, shift_i, scale_i, gate_i):
        m = mod[l]
        return jnp.stack([m[0, gate_i], m[1, gate_i], m[0, scale_i], m[1, scale_i], m[0, shift_i], m[1, shift_i]])

    xs, h = _join_norm(ctx[0], x[0], norm_g[0, 0][None], mod6(0, 0, 1, 2)[2:])
    for l in range(depth):
        p = _matmul(h, w_in_p, l, transposed_w=True, tn_cap=768, name="w_in")
        r, v, kap, lw0, lw1, k0, k1, b0, b1, gs = _rwkv_prep(
            p, mu_p[l], rwkv_w0[l], wup2[l], rwkv_a0[l], aup2[l], rwkv_k_k[l][None], rwkv_k_a[l][None], n_ctx=L)
        yf = _delta_scan(lw0, k0, b0, kap, v, r, n_ctx=L, reverse=False)
        yr = _delta_scan(lw1, k1, b1, kap, v, r, n_ctx=L, reverse=True)
        y_a = _rwkv_readout(yf, yr, r, v, k0, k1, gs, gup_p[l], rwkv_r_k[l][None], rwkv_ln_g[l][None],
                            rwkv_ln_b[l][None])
        y_b = _pool(p, pool_w_b[l], pool_scale[l][None], n_ctx=L)
        y_c = _attention(p, cos_tab, sin_tab, sink16[l], n_ctx=L)
        acc = _merge(p, y_a, y_b, y_c, gate_up_b, wa_b, wb_b, wc_b, l)
        mix = _matmul(acc, w_out, l, out_dtype=BF16, name="w_out")
        xs, h2 = _resid_norm(xs, mix, norm_g[l, 1:3], mod6(l, 3, 4, 2), n_ctx=L, emit_h=True)
        f = _matmul(_ffn_up(h2, ffn_w1, ffn_w3, l), w2_b, l, out_dtype=BF16, tm_cap=640, tk_cap=d_ff, name="ffn_down")
        if l + 1 < depth:
            g2 = jnp.stack([norm_g[l, 3], norm_g[l + 1, 0]])
            m6 = jnp.concatenate([mod6(l, 0, 1, 5)[:2], mod6(l + 1, 0, 1, 2)[2:]], axis=0)
            xs, h = _resid_norm(xs, f, g2, m6, n_ctx=L, emit_h=True)
        else:
            xs, _ = _resid_norm(xs, f, jnp.stack([norm_g[l, 3], norm_g[l, 3]]), mod6(l, 0, 1, 5), n_ctx=L,
                                emit_h=False, latent_only=True)
    return xs[None]
```

```python
import functools
import math

import jax
import jax.numpy as jnp
from jax import lax
from jax.experimental import pallas as pl
from jax.experimental.pallas import tpu as pltpu

F32 = jnp.float32
BF16 = jnp.bfloat16

LANES = 128
SUBLANES = 8
VMEM_LIMIT = 56 * 1024 * 1024

NORM_EPS = 1e-6
GRID_W = 64
ROPE_BASE = 10000.0
NEG_INF = -1e30

A_HEADS = 24
A_HEAD_DIM = 64
A_WIDTH = A_HEADS * A_HEAD_DIM
A_DECAY_RANK = 64
A_ICLR_RANK = 64
A_GATE_RANK = 224
A_GN_EPS = 64e-5
A_COLS = 3 * A_WIDTH + 2 * A_DECAY_RANK + 2 * A_ICLR_RANK + A_GATE_RANK
SCAN_CHUNK = 64
POOL_WINDOWS = (2, 4, 8, 16)
POOL_GROUP_W = 384
B_WIDTH = len(POOL_WINDOWS) * POOL_GROUP_W
POOL_HALO = 8
C_Q_HEADS = 16
C_KV_HEADS = 4
C_GROUP = C_Q_HEADS // C_KV_HEADS
C_HEAD_DIM = 128
C_WIDTH = C_Q_HEADS * C_HEAD_DIM
C_KV_WIDTH = C_KV_HEADS * C_HEAD_DIM
ATTN_BLOCK = 128
GATE_RANK = 256
N_BRANCH = 3

OFF_B = A_COLS
OFF_Q = OFF_B + B_WIDTH
OFF_K = OFF_Q + C_WIDTH
OFF_V = OFF_K + C_KV_WIDTH
OFF_G = OFF_V + C_KV_WIDTH
IN_COLS = OFF_G + GATE_RANK
A_PAD = 5120
A_GD_OFF = 3 * A_WIDTH + 2 * A_DECAY_RANK + 2 * A_ICLR_RANK
A_GD_PAD = A_PAD - A_GD_OFF
P_K = A_PAD
P_V = P_K + C_KV_WIDTH
P_B = P_V + C_KV_WIDTH
P_Q = P_B + B_WIDTH
P_G = P_Q + C_WIDTH
P_COLS = P_G + GATE_RANK


def _dot(a, b):
    return jnp.dot(a, b, preferred_element_type=F32)


def _dot_nt(a, b):
    return lax.dot_general(a, b, (((1,), (1,)), ((), ())), preferred_element_type=F32)


def _dot_tn(a, b):
    return lax.dot_general(a, b, (((0,), (0,)), ((), ())), preferred_element_type=F32)


def _tile(n, cap, mult):
    best = None
    for t in range(mult, min(n, cap) + 1, mult):
        if n % t == 0:
            best = t
    assert best is not None, (n, cap, mult)
    return best


def _params(*sem):
    return pltpu.CompilerParams(dimension_semantics=sem, vmem_limit_bytes=VMEM_LIMIT)


def _sigmoid(x):
    return 1.0 / (1.0 + jnp.exp(-x))


def _silu(x):
    return x * _sigmoid(x)


def _split3(x):
    hi = x.astype(BF16)
    rem = x - hi.astype(F32)
    mid = rem.astype(BF16)
    lo = (rem - mid.astype(F32)).astype(BF16)
    return hi, mid, lo


def _head_block_ones():
    r = lax.broadcasted_iota(jnp.int32, (3 * LANES, LANES), 0)
    c = lax.broadcasted_iota(jnp.int32, (3 * LANES, LANES), 1)
    return jnp.where(((r % LANES) // A_HEAD_DIM) == (c // A_HEAD_DIM), 1.0, 0.0).astype(BF16)


def _head_sums(x, ones_bd):
    out = []
    for j in range(0, x.shape[1], LANES):
        out.append(_dot(jnp.concatenate(_split3(x[:, j:j + LANES]), axis=1), ones_bd))
    return jnp.concatenate(out, axis=1)


def _mod_kernel(c_ref, down_ref, up_ref, b_ref, o_ref):
    s = _silu(c_ref[...]).astype(BF16)
    low = _dot(s, down_ref[0].astype(BF16)).astype(BF16)
    o_ref[0] = _dot(low, up_ref[0].astype(BF16)) + b_ref[0]


def _modulation(c8, down, up, bias):
    depth, d, rank = down.shape
    n = up.shape[2]
    tn = _tile(n, d, LANES)
    return pl.pallas_call(
        _mod_kernel,
        grid=(depth, n // tn),
        in_specs=[pl.BlockSpec((SUBLANES, d), lambda l, j: (0, 0)),
                  pl.BlockSpec((1, d, rank), lambda l, j: (l, 0, 0)),
                  pl.BlockSpec((1, rank, tn), lambda l, j: (l, 0, j)),
                  pl.BlockSpec((1, 1, tn), lambda l, j: (l, 0, j))],
        out_specs=pl.BlockSpec((1, SUBLANES, tn), lambda l, j: (l, 0, j)),
        out_shape=jax.ShapeDtypeStruct((depth, SUBLANES, n), F32),
        compiler_params=_params("arbitrary", "arbitrary"),
        name="modulation",
    )(c8, down, up, bias.reshape(depth, 1, n))


def _rms(x, g):
    return x * lax.rsqrt(jnp.mean(x * x, axis=-1, keepdims=True) + NORM_EPS) * g


def _resid_norm_kernel(*refs, ctx_tiles, has_m, emit_h):
    refs = list(refs)
    x_ref = refs.pop(0)
    m_ref = refs.pop(0) if has_m else None
    g_ref = refs.pop(0)
    mod_ref = refs.pop(0)
    is_ctx = pl.program_id(0) < ctx_tiles

    def pick(i):
        return jnp.where(is_ctx, mod_ref[i:i + 1, :], mod_ref[i + 1:i + 2, :])

    x = x_ref[...]
    if has_m:
        x = x + pick(0) * _rms(m_ref[...].astype(F32), g_ref[0:1, :])
        refs.pop(0)[...] = x
    if emit_h:
        h = _rms(x, g_ref[1:2, :]) * (1.0 + pick(2)) + pick(4)
        refs.pop(0)[...] = h.astype(BF16)


def _join_norm_kernel(ctx_ref, x_ref, g_ref, mod_ref, xs_ref, h_ref, *, ctx_tiles):
    is_ctx = pl.program_id(0) < ctx_tiles
    x = jnp.where(is_ctx, ctx_ref[...], x_ref[...])
    xs_ref[...] = x
    scale = jnp.where(is_ctx, mod_ref[0:1, :], mod_ref[1:2, :])
    shift = jnp.where(is_ctx, mod_ref[2:3, :], mod_ref[3:4, :])
    h_ref[...] = (_rms(x, g_ref[...]) * (1.0 + scale) + shift).astype(BF16)


def _join_norm(ctx, x, g, mod4):
    L, D = ctx.shape
    T = x.shape[0]
    te = _tile(math.gcd(L, T), 256, 16)
    ctx_tiles = L // te
    row = pl.BlockSpec((te, D), lambda i: (i, 0))
    return pl.pallas_call(
        functools.partial(_join_norm_kernel, ctx_tiles=ctx_tiles),
        grid=((L + T) // te,),
        in_specs=[pl.BlockSpec((te, D), lambda i: (jnp.minimum(i, ctx_tiles - 1), 0)),
                  pl.BlockSpec((te, D), lambda i: (jnp.maximum(i - ctx_tiles, 0), 0)),
                  pl.BlockSpec((1, D), lambda i: (0, 0)), pl.BlockSpec((4, D), lambda i: (0, 0))],
        out_specs=[row, row],
        out_shape=[jax.ShapeDtypeStruct((L + T, D), F32), jax.ShapeDtypeStruct((L + T, D), BF16)],
        compiler_params=_params("arbitrary"),
        name="join_norm",
    )(ctx, x, g, mod4)


def _resid_norm(x, m, g2, mod6, *, n_ctx, emit_h, latent_only=False):
    R, D = x.shape
    te = _tile(math.gcd(n_ctx, R - n_ctx), 256, 16)
    has_m = m is not None
    skip = n_ctx // te if latent_only else 0
    n_out = R - skip * te
    row_in = pl.BlockSpec((te, D), lambda i: (i + skip, 0))
    row = pl.BlockSpec((te, D), lambda i: (i, 0))
    ins = [x] + ([m] if has_m else []) + [g2, mod6]
    in_specs = [row_in] * (2 if has_m else 1) + [pl.BlockSpec((2, D), lambda i: (0, 0)),
                                                 pl.BlockSpec((6, D), lambda i: (0, 0))]
    out_shape, out_specs = [], []
    if has_m:
        out_shape.append(jax.ShapeDtypeStruct((n_out, D), F32))
        out_specs.append(row)
    if emit_h:
        out_shape.append(jax.ShapeDtypeStruct((n_out, D), BF16))
        out_specs.append(row)
    outs = pl.pallas_call(
        functools.partial(_resid_norm_kernel, ctx_tiles=n_ctx // te - skip, has_m=has_m, emit_h=emit_h),
        grid=(n_out // te,),
        in_specs=in_specs, out_specs=out_specs, out_shape=out_shape,
        compiler_params=_params("parallel"),
        name="resid_norm",
    )(*ins)
    outs = list(outs)
    x1 = outs.pop(0) if has_m else None
    h = outs.pop(0) if emit_h else None
    return x1, h


def _mm_kernel(x_ref, w_ref, o_ref, *scratch, nk, transposed_w):
    dot = _dot_nt if transposed_w else _dot
    part = dot(x_ref[...].astype(BF16), w_ref[...].astype(BF16))
    if nk == 1:
        o_ref[...] = part.astype(o_ref.dtype)
        return
    acc_ref, = scratch
    k = pl.program_id(2)

    @pl.when(k == 0)
    def _():
        acc_ref[...] = part

    @pl.when(k > 0)
    def _():
        acc_ref[...] += part

    @pl.when(k == nk - 1)
    def _():
        o_ref[...] = acc_ref[...].astype(o_ref.dtype)


def _matmul(x, w, layer, *, transposed_w=False, out_dtype=F32, tm_cap=1280, tn_cap=512, tk_cap=4096,
            name="matmul"):
    M, K = x.shape
    N = w.shape[1] if transposed_w else w.shape[2]
    tm = _tile(M, tm_cap, 16)
    tn = _tile(N, tn_cap, LANES)
    tk = _tile(K, tk_cap, LANES)
    nk = K // tk
    if transposed_w:
        w_spec = pl.BlockSpec((None, tn, tk), lambda i, j, k: (layer, j, k))
    else:
        w_spec = pl.BlockSpec((None, tk, tn), lambda i, j, k: (layer, k, j))
    return pl.pallas_call(
        functools.partial(_mm_kernel, nk=nk, transposed_w=transposed_w),
        grid=(M // tm, N // tn, nk),
        in_specs=[pl.BlockSpec((tm, tk), lambda i, j, k: (i, k)), w_spec],
        out_specs=pl.BlockSpec((tm, tn), lambda i, j, k: (i, j)),
        out_shape=jax.ShapeDtypeStruct((M, N), out_dtype),
        scratch_shapes=[pltpu.VMEM((tm, tn), F32)] if nk > 1 else [],
        compiler_params=_params("parallel", "parallel", "arbitrary"),
        name=name,
    )(x, w)


def _ffn_up_kernel(h_ref, w1_ref, w3_ref, o_ref):
    h = h_ref[...]
    o_ref[...] = (_silu(_dot(h, w1_ref[...].astype(BF16))) * _dot(h, w3_ref[...].astype(BF16))).astype(o_ref.dtype)


def _ffn_up(h, w1, w3, layer):
    M, K = h.shape
    N = w1.shape[2]
    tm = _tile(M, 1664, 16)
    tn = _tile(N, 256, LANES)
    wspec = pl.BlockSpec((None, K, tn), lambda i, j: (layer, 0, j))
    return pl.pallas_call(
        _ffn_up_kernel,
        grid=(M // tm, N // tn),
        in_specs=[pl.BlockSpec((tm, K), lambda i, j: (i, 0)), wspec, wspec],
        out_specs=pl.BlockSpec((tm, tn), lambda i, j: (i, j)),
        out_shape=jax.ShapeDtypeStruct((M, N), BF16),
        compiler_params=_params("parallel", "parallel"),
        name="ffn_up",
    )(h, w1, w3)


def _merge_kernel(pg_ref, ya_ref, yb_ref, yc_ref, ga_ref, gb_ref, gc_ref, wa_ref, wb_ref, wc_ref, o_ref):
    pg = pg_ref[...].astype(BF16)
    acc = _sigmoid(_dot(pg, ga_ref[...])) * _dot(ya_ref[...], wa_ref[...])
    acc += _sigmoid(_dot(pg, gb_ref[...])) * _dot(yb_ref[...], wb_ref[...])
    acc += _sigmoid(_dot(pg, gc_ref[...])) * _dot(yc_ref[...], wc_ref[...])
    o_ref[...] = acc.astype(o_ref.dtype)


def _merge(p, ya, yb, yc, gate_up, wa, wb, wc, layer):
    R = p.shape[0]
    D = wa.shape[2]
    tm = _tile(R, 1280, 16)
    tn = _tile(D, 512, LANES)
    nj = D // tn

    def rows(width):
        return pl.BlockSpec((tm, width), lambda i, j: (i, 0))

    def gate(branch):
        return pl.BlockSpec((None, GATE_RANK, tn), lambda i, j: (layer, 0, branch * nj + j))

    def wcol(width):
        return pl.BlockSpec((None, width, tn), lambda i, j: (layer, 0, j))

    return pl.pallas_call(
        _merge_kernel,
        grid=(R // tm, nj),
        in_specs=[pl.BlockSpec((tm, GATE_RANK), lambda i, j: (i, P_G // GATE_RANK)),
                  rows(A_WIDTH), rows(B_WIDTH), rows(C_WIDTH),
                  gate(0), gate(1), gate(2), wcol(A_WIDTH), wcol(B_WIDTH), wcol(C_WIDTH)],
        out_specs=pl.BlockSpec((tm, tn), lambda i, j: (i, j)),
        out_shape=jax.ShapeDtypeStruct((R, D), BF16),
        compiler_params=_params("parallel", "parallel"),
        name="merge",
    )(p, ya, yb, yc, gate_up, gate_up, gate_up, wa, wb, wc)


def _segment_flags(i, ctx_tiles, n_tiles):
    first = jnp.logical_or(i == 0, i == ctx_tiles)
    last = jnp.logical_or(i == ctx_tiles - 1, i == n_tiles - 1)
    return first, last


def _rwkv_prep_kernel(u_ref, prev_ref, next_ref, mu_ref, w0_ref, wup_ref, a0_ref, aup_ref, kk_ref, ka_ref,
                      pu_ref, pprev_ref, pnext_ref, pw_ref, pscale_ref,
                      r_out, v_out, kap_out, lw0_out, lw1_out, k0_out, k1_out, b0_out, b1_out, gs_out, pool_out,
                      ext_ref, *, ctx_tiles, n_tiles, n_ctx, n_lat):
    i = pl.program_id(0)
    first, last = _segment_flags(i, ctx_tiles, n_tiles)
    _pool_tile(i, first, last, pu_ref, pprev_ref, pnext_ref, pw_ref, pscale_ref, pool_out, ext_ref,
               ctx_tiles=ctx_tiles, n_ctx=n_ctx, n_lat=n_lat)
    u = u_ref[...]
    tm = u.shape[0]
    rid = lax.broadcasted_iota(jnp.int32, (tm, 1), 0)
    prev_row = jnp.where(first, 0.0, prev_ref[SUBLANES - 1:SUBLANES, :])
    next_row = jnp.where(last, 0.0, next_ref[0:1, :])
    prev = jnp.where(rid == 0, prev_row, pltpu.roll(u, 1, axis=0))
    nxt = jnp.where(rid == tm - 1, next_row, pltpu.roll(u, tm - 1, axis=0))
    s = u + mu_ref[...] * (0.5 * (prev + nxt) - u)

    W = A_WIDTH
    r = s[:, 0:W]
    k = s[:, W:2 * W]
    v = s[:, 2 * W:3 * W]
    o = 3 * W
    wd = jnp.tanh(s[:, o:o + LANES]).astype(BF16)
    ad = s[:, o + LANES:o + 2 * LANES].astype(BF16)
    gd = s[:, A_GD_OFF:A_PAD]

    ones_bd = _head_block_ones()
    kk = k * kk_ref[...]
    nrm = jnp.sqrt(_head_sums(kk * kk, ones_bd))
    kk = kk / jnp.maximum(nrm, 1e-12)

    r_out[...] = r.astype(BF16)
    v_out[...] = v.astype(BF16)
    kap_out[...] = kk.astype(BF16)
    gs_out[...] = _sigmoid(gd).astype(BF16)
    for d, (lw_out, k_out, b_out) in enumerate(((lw0_out, k0_out, b0_out), (lw1_out, k1_out, b1_out))):
        z = w0_ref[d:d + 1, :] + _dot(wd, wup_ref[d])
        lw_out[...] = -math.exp(-0.5) * _sigmoid(z)
        a = _sigmoid(a0_ref[d:d + 1, :] + _dot(ad, aup_ref[d]))
        k_out[...] = (k * (1.0 + (a - 1.0) * ka_ref[...])).astype(BF16)
        b_out[...] = (kk * a).astype(BF16)


def _rwkv_prep_pool(p, mu, w0, wup2, a0, aup2, k_k, k_a, pool_w, pool_scale, *, n_ctx):
    R = p.shape[0]
    tm = _tile(math.gcd(n_ctx, R - n_ctx), 256, 16)
    n_tiles = R // tm
    assert SUBLANES == POOL_HALO and P_B % B_WIDTH == 0
    hb = tm // SUBLANES
    n_hblocks = R // SUBLANES
    W = A_WIDTH
    cb = P_B // B_WIDTH

    def const(shape):
        return pl.BlockSpec(shape, lambda i: (0,) * len(shape))

    def halo_specs(width, col):
        return [pl.BlockSpec((tm, width), lambda i: (i, col)),
                pl.BlockSpec((SUBLANES, width), lambda i: (jnp.maximum(i * hb - 1, 0), col)),
                pl.BlockSpec((SUBLANES, width), lambda i: (jnp.minimum((i + 1) * hb, n_hblocks - 1), col))]

    wide = pl.BlockSpec((tm, W), lambda i: (i, 0))
    f32w = jax.ShapeDtypeStruct((R, W), F32)
    bf16w = jax.ShapeDtypeStruct((R, W), BF16)
    return pl.pallas_call(
        functools.partial(_rwkv_prep_kernel, ctx_tiles=n_ctx // tm, n_tiles=n_tiles, n_ctx=n_ctx, n_lat=R - n_ctx),
        grid=(n_tiles,),
        in_specs=halo_specs(A_PAD, 0)
        + [const((1, A_PAD)), const((2, W)), const((2, LANES, W)), const((2, W)), const((2, LANES, W)),
           const((1, W)), const((1, W))]
        + halo_specs(B_WIDTH, cb)
        + [const((len(POOL_WINDOWS), POOL_GROUP_W, POOL_GROUP_W)), const((1, B_WIDTH))],
        out_specs=[wide] * 9 + [pl.BlockSpec((tm, A_GD_PAD), lambda i: (i, 0)),
                                pl.BlockSpec((tm, B_WIDTH), lambda i: (i, 0))],
        out_shape=[bf16w] * 3 + [f32w] * 2 + [bf16w] * 4 + [jax.ShapeDtypeStruct((R, A_GD_PAD), BF16),
                                                            jax.ShapeDtypeStruct((R, B_WIDTH), BF16)],
        scratch_shapes=[pltpu.VMEM((tm + 2 * POOL_HALO, B_WIDTH), F32)],
        compiler_params=_params("parallel"),
        name="rwkv_prep_pool",
    )(p, p, p, mu, w0, wup2, a0, aup2, k_k, k_a, p, p, p, pool_w, pool_scale)


def _scan_kernel(lw_ref, k_ref, b_ref, kap_ref, v_ref, r_ref, y_ref, s_ref, *, reverse, npairs):
    C = SCAN_CHUNK
    N = A_HEAD_DIM

    @pl.when(pl.program_id(1) == 0)
    def _():
        s_ref[...] = jnp.zeros_like(s_ref)

    row = lax.broadcasted_iota(jnp.int32, (C, C), 0)
    col = lax.broadcasted_iota(jnp.int32, (C, C), 1)
    earlier = (col > row) if reverse else (col < row)
    diag = col == row
    incl = jnp.logical_or(earlier, diag)
    tri = jnp.where(incl, 1.0, 0.0).astype(BF16)
    tri3 = jnp.concatenate([tri, tri, tri], axis=1)
    eye = jnp.where(diag, 1.0, 0.0).astype(F32)

    def same_block(n):
        return (row // n) == (col // n)

    last = 0 if reverse else C - 1

    hs = (slice(0, N), slice(N, 2 * N))
    heads = range(2 * npairs)
    x, k_t, b_t, k_e, b_e, e_tot, v = ([] for _ in range(7))
    for p in range(npairs):
        sl = slice(p * LANES, (p + 1) * LANES)
        lw = lw_ref[:, sl]
        c = _dot(tri3, jnp.concatenate(_split3(lw), axis=0))
        ctot = c[last:last + 1, :]
        e_nc = jnp.exp(-c)
        e_tc = jnp.exp(ctot - c)
        kap_p = kap_ref[:, sl].astype(F32) * jnp.exp(c - lw)
        r_p = r_ref[:, sl].astype(F32) * jnp.exp(c)
        k_p = k_ref[:, sl].astype(F32)
        b_p = b_ref[:, sl].astype(F32)
        v_p = v_ref[:, sl]
        e_p = jnp.exp(ctot)
        for h in hs:
            x.append(jnp.concatenate([kap_p[:, h], r_p[:, h]], axis=0).astype(BF16))
            k_t.append((k_p * e_nc)[:, h].astype(BF16))
            b_t.append((b_p * e_nc)[:, h].astype(BF16))
            k_e.append((k_p * e_tc)[:, h].astype(BF16))
            b_e.append((b_p * e_tc)[:, h].astype(BF16))
            e_tot.append(e_p[:, h])
            v.append(v_p[:, h].astype(BF16))

    def bd(a, b):
        return _dot(a.astype(BF16), b.astype(BF16))

    g1 = [_dot_nt(x[i], k_t[i]) for i in heads]
    g2 = [_dot_nt(x[i], b_t[i]) for i in heads]
    a_kk = [jnp.where(earlier, g1[i][:C], 0.0).astype(BF16) for i in heads]
    a_rk = [jnp.where(incl, g1[i][C:], 0.0).astype(BF16) for i in heads]
    a_kb = [jnp.where(earlier, g2[i][:C], 0.0) for i in heads]
    a_rb = [jnp.where(incl, g2[i][C:], 0.0).astype(BF16) for i in heads]
    blk8 = same_block(8)
    a0 = [jnp.where(blk8, a_kb[i], 0.0) for i in heads]
    a2 = [bd(a0[i], a0[i]) for i in heads]
    a4 = [bd(a2[i], a2[i]) for i in heads]
    t = [bd(eye - a0[i], eye + a2[i]) for i in heads]
    t = [bd(t[i], eye + a4[i]) for i in heads]
    for n in (16, 32, 64):
        m = jnp.logical_and(same_block(n), jnp.logical_not(same_block(n // 2)))
        off = [jnp.where(m, a_kb[i], 0.0) for i in heads]
        ot = [bd(off[i], t[i]) for i in heads]
        t = [t[i] - bd(t[i], ot[i]) for i in heads]
    s0 = [s_ref[i] for i in heads]
    xs = [_dot_nt(x[i], s0[i].astype(BF16)) for i in heads]
    akv = [_dot(a_kk[i], v[i]) for i in heads]
    u = [bd(t[i], xs[i][:C] + akv[i]).astype(BF16) for i in heads]
    ys = [xs[i][C:] + _dot(a_rk[i], v[i]) - _dot(a_rb[i], u[i]) for i in heads]
    for i in heads:
        s_ref[i] = s0[i] * e_tot[i] + _dot_tn(v[i], k_e[i]) - _dot_tn(u[i], b_e[i])
    for p in range(npairs):
        y_ref[:, p * LANES:(p + 1) * LANES] = jnp.concatenate([ys[2 * p], ys[2 * p + 1]], axis=1)


def _delta_scan(lw, k, b, kap, v, r, *, n_ctx, reverse, pairs_per_block=None):
    R, W = lw.shape
    C = SCAN_CHUNK
    nchunks = R // C
    ctx_chunks = n_ctx // C
    npairs = W // LANES
    pb = npairs if pairs_per_block is None else pairs_per_block
    assert R % C == 0 and n_ctx % C == 0 and npairs % pb == 0

    if reverse:
        def rows(s):
            return jnp.where(s < ctx_chunks, ctx_chunks - 1 - s, nchunks - 1 - (s - ctx_chunks))
    else:
        def rows(s):
            return s

    spec = pl.BlockSpec((C, pb * LANES), lambda g, s: (rows(s), g))
    return pl.pallas_call(
        functools.partial(_scan_kernel, reverse=reverse, npairs=pb),
        grid=(npairs // pb, nchunks),
        in_specs=[spec] * 6,
        out_specs=spec,
        out_shape=jax.ShapeDtypeStruct((R, W), F32),
        scratch_shapes=[pltpu.VMEM((2 * pb, A_HEAD_DIM, A_HEAD_DIM), F32)],
        compiler_params=_params("parallel", "arbitrary"),
        name="delta_scan_rev" if reverse else "delta_scan_fwd",
    )(lw, k, b, kap, v, r)


def _rwkv_readout_kernel(yf_ref, yb_ref, r_ref, v_ref, k0_ref, k1_ref, gs_ref, gup_ref, rk_ref, lng_ref, lnb_ref,
                         o_ref):
    ones_bd = _head_block_ones()
    inv_n = 1.0 / A_HEAD_DIM
    ro = yf_ref[...] + yb_ref[...]
    mu = _head_sums(ro, ones_bd) * inv_n
    cen = ro - mu
    var = _head_sums(cen * cen, ones_bd) * inv_n
    yn = cen * lax.rsqrt(var + A_GN_EPS) * lng_ref[...] + lnb_ref[...]
    rk = r_ref[...].astype(F32) * (k0_ref[...].astype(F32) + k1_ref[...].astype(F32)) * rk_ref[...]
    bonus = _head_sums(rk, ones_bd) * v_ref[...].astype(F32)
    g = _dot(gs_ref[...], gup_ref[...])
    o_ref[...] = ((yn + bonus) * g).astype(o_ref.dtype)


def _rwkv_readout(yf, yb, r, v, k0, k1, gs, gup, r_k, ln_g, ln_b):
    R, W = yf.shape
    tm = _tile(R, 256, 16)
    wide = pl.BlockSpec((tm, W), lambda i: (i, 0))
    vec = pl.BlockSpec((1, W), lambda i: (0, 0))
    return pl.pallas_call(
        _rwkv_readout_kernel,
        grid=(R // tm,),
        in_specs=[wide] * 6 + [pl.BlockSpec((tm, A_GD_PAD), lambda i: (i, 0)),
                               pl.BlockSpec((A_GD_PAD, W), lambda i: (0, 0)), vec, vec, vec],
        out_specs=wide,
        out_shape=jax.ShapeDtypeStruct((R, W), BF16),
        compiler_params=_params("parallel"),
        name="rwkv_readout",
    )(yf, yb, r, v, k0, k1, gs, gup, r_k, ln_g, ln_b)


def _pool_tile(i, first, last, u_ref, prev_ref, next_ref, w_ref, scale_ref, o_ref, ext_ref, *, ctx_tiles, n_ctx, n_lat):
    tm = u_ref.shape[0]
    H = POOL_HALO
    ext_ref[0:H, :] = jnp.where(first, 0.0, prev_ref[...])
    ext_ref[H:H + tm, :] = u_ref[...]
    ext_ref[H + tm:H + tm + H, :] = jnp.where(last, 0.0, next_ref[...])
    is_ctx = i < ctx_tiles
    seg_len = jnp.where(is_ctx, n_ctx, n_lat)
    t = lax.broadcasted_iota(jnp.int32, (tm, 1), 0) + i * tm - jnp.where(is_ctx, 0, n_ctx)
    for gi, win in enumerate(POOL_WINDOWS):
        cols = slice(gi * POOL_GROUP_W, (gi + 1) * POOL_GROUP_W)
        acc = None
        for o in range(-(win // 2), win // 2):
            term = ext_ref[H + o:H + o + tm, cols]
            acc = term if acc is None else acc + term
        lo = jnp.maximum(t - win // 2, 0)
        hi = jnp.minimum(t + win // 2 - 1, seg_len - 1)
        cnt = (hi - lo + 1).astype(F32)
        pooled = acc / cnt - u_ref[:, cols]
        y = _dot(pooled.astype(BF16), w_ref[gi]) * scale_ref[:, cols]
        o_ref[:, cols] = y.astype(o_ref.dtype)


def _rope(x, cos, sin):
    return x * cos + pltpu.roll(x, C_HEAD_DIM // 2, axis=1) * sin


def _rope_tables(n_ctx, n_lat):
    half = C_HEAD_DIM // 2
    t = jnp.arange(n_lat)
    row = (t // GRID_W).astype(F32)
    col = (t % GRID_W).astype(F32)
    inv = ROPE_BASE ** (-jnp.arange(0, half, 2, dtype=F32) / half)
    ar = row[:, None] * inv[None]
    ac = col[:, None] * inv[None]
    cos = jnp.concatenate([jnp.cos(ar), jnp.cos(ac), jnp.cos(ar), jnp.cos(ac)], axis=1)
    sin = jnp.concatenate([-jnp.sin(ar), -jnp.sin(ac), jnp.sin(ar), jnp.sin(ac)], axis=1)
    cos = jnp.concatenate([jnp.ones((n_ctx, C_HEAD_DIM), F32), cos], axis=0)
    sin = jnp.concatenate([jnp.zeros((n_ctx, C_HEAD_DIM), F32), sin], axis=0)
    return cos, sin


def _attn_kernel(q0_ref, q1_ref, q2_ref, q3_ref, kp_ref, kc_ref, kn_ref, kx_ref, vp_ref, vc_ref, vn_ref, vx_ref,
                 cp_ref, sp_ref, co_ref, so_ref, cn_ref, sn_ref, sink_ref, o_ref, *, ctx_qblocks, n_qblocks):
    i = pl.program_id(0)
    B = ATTN_BLOCK
    q_refs = (q0_ref, q1_ref, q2_ref, q3_ref)
    tq = q0_ref.shape[0]
    cos_o = co_ref[...]
    sin_o = so_ref[...]
    nloc = tq + 2 * B
    qrow = lax.broadcasted_iota(jnp.int32, (tq, nloc), 0)
    kcol = lax.broadcasted_iota(jnp.int32, (tq, nloc), 1)
    rel = kcol - B - qrow
    lo = jnp.where(i == ctx_qblocks, B, 0)
    hi = jnp.where(i < ctx_qblocks, 0, jnp.where(i == n_qblocks - 1, B + tq, nloc))
    bias = jnp.where(jnp.abs(rel) <= ATTN_BLOCK, 0.0, NEG_INF)
    bias = jnp.where(kcol >= lo, bias, NEG_INF)
    bias = jnp.where(kcol < hi, bias, NEG_INF)
    scale = C_HEAD_DIM ** -0.5
    for h in range(C_KV_HEADS):
        kc = slice(h * C_HEAD_DIM, (h + 1) * C_HEAD_DIM)
        k_loc = jnp.concatenate([_rope(kp_ref[:, kc], cp_ref[...], sp_ref[...]).astype(BF16),
                                 _rope(kc_ref[:, kc], cos_o, sin_o).astype(BF16),
                                 _rope(kn_ref[:, kc], cn_ref[...], sn_ref[...]).astype(BF16)], axis=0)
        v_loc = jnp.concatenate([vp_ref[:, kc].astype(BF16), vc_ref[:, kc].astype(BF16),
                                 vn_ref[:, kc].astype(BF16)], axis=0)
        k_ctx = kx_ref[:, kc].astype(BF16)
        v_ctx = vx_ref[:, kc].astype(BF16)
        for g in range(C_GROUP):
            j = h * C_GROUP + g
            cols = slice(j * C_HEAD_DIM, (j + 1) * C_HEAD_DIM)
            q = _rope(q_refs[h][:, g * C_HEAD_DIM:(g + 1) * C_HEAD_DIM], cos_o, sin_o).astype(BF16)
            s_loc = _dot_nt(q, k_loc) * scale + bias
            s_ctx = _dot_nt(q, k_ctx) * scale
            sk = sink_ref[j:j + 1, 0:1]
            m = jnp.maximum(jnp.maximum(jnp.max(s_loc, axis=-1, keepdims=True),
                                        jnp.max(s_ctx, axis=-1, keepdims=True)), sk)
            e_loc = jnp.exp(s_loc - m)
            e_ctx = jnp.exp(s_ctx - m)
            denom = (jnp.sum(e_loc, axis=-1, keepdims=True) + jnp.sum(e_ctx, axis=-1, keepdims=True)
                     + jnp.exp(sk - m))
            o = _dot(e_loc.astype(BF16), v_loc) + _dot(e_ctx.astype(BF16), v_ctx)
            o_ref[:, cols] = (o / denom).astype(o_ref.dtype)


def _attention(p, cos_tab, sin_tab, sink16, *, n_ctx):
    R = p.shape[0]
    B = ATTN_BLOCK
    tq = _tile(math.gcd(n_ctx, R - n_ctx), 2 * B, B)
    per = tq // B
    n_qblocks = R // tq
    n_blocks = R // B
    ctx_blocks = n_ctx // B
    W = C_KV_WIDTH
    assert n_ctx % B == 0 and R % B == 0 and P_Q % W == 0 and P_K % W == 0 and P_V % W == 0
    assert C_GROUP * C_HEAD_DIM == W
    kcol, vcol = P_K // W, P_V // W

    def prev_rows(i):
        return jnp.clip(i * per - 1, ctx_blocks, n_blocks - 1)

    def next_rows(i):
        return jnp.clip((i + 1) * per, ctx_blocks, n_blocks - 1)

    def kv_specs(c):
        return [pl.BlockSpec((B, W), lambda i: (prev_rows(i), c)), pl.BlockSpec((tq, W), lambda i: (i, c)),
                pl.BlockSpec((B, W), lambda i: (next_rows(i), c)), pl.BlockSpec((n_ctx, W), lambda i: (0, c))]

    tab_specs = []
    for rows, fn in ((B, prev_rows), (tq, lambda i: i), (B, next_rows)):
        tab_specs += [pl.BlockSpec((rows, C_HEAD_DIM), lambda i, fn=fn: (fn(i), 0))] * 2
    q_specs = [pl.BlockSpec((tq, W), lambda i, h=h: (i, P_Q // W + h)) for h in range(C_KV_HEADS)]

    return pl.pallas_call(
        functools.partial(_attn_kernel, ctx_qblocks=n_ctx // tq, n_qblocks=n_qblocks),
        grid=(n_qblocks,),
        in_specs=q_specs + kv_specs(kcol) + kv_specs(vcol) + tab_specs
        + [pl.BlockSpec((C_Q_HEADS, LANES), lambda i: (0, 0))],
        out_specs=pl.BlockSpec((tq, C_WIDTH), lambda i: (i, 0)),
        out_shape=jax.ShapeDtypeStruct((R, C_WIDTH), BF16),
        compiler_params=_params("parallel"),
        name="window_attention",
    )(*([p] * 12), cos_tab, sin_tab, cos_tab, sin_tab, cos_tab, sin_tab, sink16)


def _pad_to(a, axis, size):
    pad = [(0, 0)] * a.ndim
    pad[axis] = (0, size - a.shape[axis])
    return jnp.pad(a, pad)


def _relayout_w_in_kernel(w_ref, o_ref):
    quarter = C_HEAD_DIM // 4

    def copy_rows(dst, src, n):
        o_ref[dst:dst + n, :] = w_ref[src:src + n, :].astype(o_ref.dtype)

    def copy_heads(dst, src, width):
        for h in range(0, width, C_HEAD_DIM):
            for new, old in enumerate((0, 2, 1, 3)):
                copy_rows(dst + h + new * quarter, src + h + old * quarter, quarter)

    copy_rows(0, 0, OFF_B)
    o_ref[OFF_B:A_PAD, :] = jnp.zeros((A_PAD - OFF_B, o_ref.shape[1]), o_ref.dtype)
    copy_heads(P_K, OFF_K, C_KV_WIDTH)
    copy_rows(P_V, OFF_V, C_KV_WIDTH)
    copy_rows(P_B, OFF_B, B_WIDTH)
    copy_heads(P_Q, OFF_Q, C_WIDTH)
    copy_rows(P_G, OFF_G, GATE_RANK)


def _relayout_w_in(w_in):
    depth, D, n = w_in.shape
    assert n == IN_COLS
    w_t = jnp.swapaxes(w_in, 1, 2)
    tc = _tile(D, 256, LANES)
    return pl.pallas_call(
        _relayout_w_in_kernel,
        grid=(depth, D // tc),
        in_specs=[pl.BlockSpec((None, n, tc), lambda l, i: (l, 0, i))],
        out_specs=pl.BlockSpec((None, P_COLS, tc), lambda l, i: (l, 0, i)),
        out_shape=jax.ShapeDtypeStruct((depth, P_COLS, D), BF16),
        compiler_params=_params("parallel", "parallel"),
        name="w_in_relayout",
    )(w_t)


def _low_rank_pair(up):
    z = jnp.zeros_like(up[:, 0])
    return jnp.stack([jnp.concatenate([up[:, 0], z], axis=1), jnp.concatenate([z, up[:, 1]], axis=1)], axis=1).astype(BF16)


def kernel(x, c, ctx, c_ctx, mod_down, mod_up, mod_b, norm_g, w_in, shift_mu, rwkv_w0, rwkv_w_up, rwkv_a0,
           rwkv_a_up, rwkv_g_up, rwkv_k_k, rwkv_k_a, rwkv_r_k, rwkv_ln_g, rwkv_ln_b, pool_w, pool_scale,
           attn_sink, gate_up, w_branch_a, w_branch_b, w_branch_c, w_out, ffn_w1, ffn_w3, ffn_w2):
    assert x.shape[0] == 1 and ctx.shape[0] == 1 and c.shape[0] == 1
    depth = w_in.shape[0]
    T, D = x.shape[1], x.shape[2]
    L = ctx.shape[1]
    d_ff = ffn_w2.shape[1]
    assert 2 * A_DECAY_RANK == LANES and 2 * A_ICLR_RANK == LANES

    w_in_p = _relayout_w_in(w_in)
    mu_p = _pad_to(shift_mu, -1, A_PAD)[:, None, :]
    wup2 = _low_rank_pair(rwkv_w_up)
    aup2 = _low_rank_pair(rwkv_a_up)
    gup_p = _pad_to(rwkv_g_up, 1, A_GD_PAD).astype(BF16)
    pool_w_b = pool_w.astype(BF16)
    gate_up_b = gate_up.astype(BF16)
    wa_b, wb_b, wc_b = (w.astype(BF16) for w in (w_branch_a, w_branch_b, w_branch_c))
    w2_b = ffn_w2.astype(BF16)
    sink16 = jnp.broadcast_to(attn_sink[..., None], (depth, C_Q_HEADS, LANES))
    cos_tab, sin_tab = _rope_tables(L, T)

    c8 = _pad_to(jnp.concatenate([c_ctx[None], c], axis=0), 0, SUBLANES)
    mod = _modulation(c8, mod_down, mod_up, mod_b).reshape(depth, SUBLANES, 6, D)

    def mod6(l, shift_i, scale_i, gate_i):
        m = mod[l]
        return jnp.stack([m[0, gate_i], m[1, gate_i], m[0, scale_i], m[1, scale_i], m[0, shift_i], m[1, shift_i]])

    xs, h = _join_norm(ctx[0], x[0], norm_g[0, 0][None], mod6(0, 0, 1, 2)[2:])
    for l in range(depth):
        p = _matmul(h, w_in_p, l, transposed_w=True, tn_cap=768, name="w_in")
        r, v, kap, lw0, lw1, k0, k1, b0, b1, gs, y_b = _rwkv_prep_pool(
            p, mu_p[l], rwkv_w0[l], wup2[l], rwkv_a0[l], aup2[l], rwkv_k_k[l][None], rwkv_k_a[l][None],
            pool_w_b[l], pool_scale[l][None], n_ctx=L)
        yf = _delta_scan(lw0, k0, b0, kap, v, r, n_ctx=L, reverse=False)
        yr = _delta_scan(lw1, k1, b1, kap, v, r, n_ctx=L, reverse=True)
        y_a = _rwkv_readout(yf, yr, r, v, k0, k1, gs, gup_p[l], rwkv_r_k[l][None], rwkv_ln_g[l][None],
                            rwkv_ln_b[l][None])
        y_c = _attention(p, cos_tab, sin_tab, sink16[l], n_ctx=L)
        acc = _merge(p, y_a, y_b, y_c, gate_up_b, wa_b, wb_b, wc_b, l)
        mix = _matmul(acc, w_out, l, out_dtype=BF16, tm_cap=1664, name="w_out")
        xs, h2 = _resid_norm(xs, mix, norm_g[l, 1:3], mod6(l, 3, 4, 2), n_ctx=L, emit_h=True)
        f = _matmul(_ffn_up(h2, ffn_w1, ffn_w3, l), w2_b, l, out_dtype=BF16, tm_cap=640, tk_cap=d_ff, name="ffn_down")
        if l + 1 < depth:
            g2 = jnp.stack([norm_g[l, 3], norm_g[l + 1, 0]])
            m6 = jnp.concatenate([mod6(l, 0, 1, 5)[:2], mod6(l + 1, 0, 1, 2)[2:]], axis=0)
            xs, h = _resid_norm(xs, f, g2, m6, n_ctx=L, emit_h=True)
        else:
            xs, _ = _resid_norm(xs, f, jnp.stack([norm_g[l, 3], norm_g[l, 3]]), mod6(l, 0, 1, 5), n_ctx=L,
                                emit_h=False, latent_only=True)
    return xs[None]
```

```python
import functools
import math

import jax
import jax.numpy as jnp
from jax import lax
from jax.experimental import pallas as pl
from jax.experimental.pallas import tpu as pltpu

F32 = jnp.float32
BF16 = jnp.bfloat16

LANES = 128
SUBLANES = 8
VMEM_LIMIT = 56 * 1024 * 1024

NORM_EPS = 1e-6
GRID_W = 64
ROPE_BASE = 10000.0
NEG_INF = -1e30

A_HEADS = 24
A_HEAD_DIM = 64
A_WIDTH = A_HEADS * A_HEAD_DIM
A_DECAY_RANK = 64
A_ICLR_RANK = 64
A_GATE_RANK = 224
A_GN_EPS = 64e-5
A_COLS = 3 * A_WIDTH + 2 * A_DECAY_RANK + 2 * A_ICLR_RANK + A_GATE_RANK
SCAN_CHUNK = 64
POOL_WINDOWS = (2, 4, 8, 16)
POOL_GROUP_W = 384
B_WIDTH = len(POOL_WINDOWS) * POOL_GROUP_W
POOL_HALO = 8
C_Q_HEADS = 16
C_KV_HEADS = 4
C_GROUP = C_Q_HEADS // C_KV_HEADS
C_HEAD_DIM = 128
C_WIDTH = C_Q_HEADS * C_HEAD_DIM
C_KV_WIDTH = C_KV_HEADS * C_HEAD_DIM
ATTN_BLOCK = 128
GATE_RANK = 256
N_BRANCH = 3

OFF_B = A_COLS
OFF_Q = OFF_B + B_WIDTH
OFF_K = OFF_Q + C_WIDTH
OFF_V = OFF_K + C_KV_WIDTH
OFF_G = OFF_V + C_KV_WIDTH
IN_COLS = OFF_G + GATE_RANK
A_PAD = 5120
A_GD_OFF = 3 * A_WIDTH + 2 * A_DECAY_RANK + 2 * A_ICLR_RANK
A_GD_PAD = A_PAD - A_GD_OFF
P_K = A_PAD
P_V = P_K + C_KV_WIDTH
P_B = P_V + C_KV_WIDTH
P_Q = P_B + B_WIDTH
P_G = P_Q + C_WIDTH
P_COLS = P_G + GATE_RANK


def _dot(a, b):
    return jnp.dot(a, b, preferred_element_type=F32)


def _dot_nt(a, b):
    return lax.dot_general(a, b, (((1,), (1,)), ((), ())), preferred_element_type=F32)


def _dot_tn(a, b):
    return lax.dot_general(a, b, (((0,), (0,)), ((), ())), preferred_element_type=F32)


def _tile(n, cap, mult):
    best = None
    for t in range(mult, min(n, cap) + 1, mult):
        if n % t == 0:
            best = t
    assert best is not None, (n, cap, mult)
    return best


def _params(*sem):
    return pltpu.CompilerParams(dimension_semantics=sem, vmem_limit_bytes=VMEM_LIMIT)


def _sigmoid(x):
    return 1.0 / (1.0 + jnp.exp(-x))


def _silu(x):
    return x * _sigmoid(x)


def _split3(x):
    hi = x.astype(BF16)
    rem = x - hi.astype(F32)
    mid = rem.astype(BF16)
    lo = (rem - mid.astype(F32)).astype(BF16)
    return hi, mid, lo


def _head_block_ones():
    r = lax.broadcasted_iota(jnp.int32, (3 * LANES, LANES), 0)
    c = lax.broadcasted_iota(jnp.int32, (3 * LANES, LANES), 1)
    return jnp.where(((r % LANES) // A_HEAD_DIM) == (c // A_HEAD_DIM), 1.0, 0.0).astype(BF16)


def _head_sums(x, ones_bd):
    out = []
    for j in range(0, x.shape[1], LANES):
        out.append(_dot(jnp.concatenate(_split3(x[:, j:j + LANES]), axis=1), ones_bd))
    return jnp.concatenate(out, axis=1)


def _mod_kernel(c_ref, down_ref, up_ref, b_ref, o_ref):
    s = _silu(c_ref[...]).astype(BF16)
    low = _dot(s, down_ref[0].astype(BF16)).astype(BF16)
    o_ref[0] = _dot(low, up_ref[0].astype(BF16)) + b_ref[0]


def _modulation(c8, down, up, bias):
    depth, d, rank = down.shape
    n = up.shape[2]
    tn = _tile(n, d, LANES)
    return pl.pallas_call(
        _mod_kernel,
        grid=(depth, n // tn),
        in_specs=[pl.BlockSpec((SUBLANES, d), lambda l, j: (0, 0)),
                  pl.BlockSpec((1, d, rank), lambda l, j: (l, 0, 0)),
                  pl.BlockSpec((1, rank, tn), lambda l, j: (l, 0, j)),
                  pl.BlockSpec((1, 1, tn), lambda l, j: (l, 0, j))],
        out_specs=pl.BlockSpec((1, SUBLANES, tn), lambda l, j: (l, 0, j)),
        out_shape=jax.ShapeDtypeStruct((depth, SUBLANES, n), F32),
        compiler_params=_params("arbitrary", "arbitrary"),
        name="modulation",
    )(c8, down, up, bias.reshape(depth, 1, n))


def _rms(x, g):
    return x * lax.rsqrt(jnp.mean(x * x, axis=-1, keepdims=True) + NORM_EPS) * g


def _resid_norm_kernel(*refs, ctx_tiles, has_m, emit_h):
    refs = list(refs)
    x_ref = refs.pop(0)
    m_ref = refs.pop(0) if has_m else None
    g_ref = refs.pop(0)
    mod_ref = refs.pop(0)
    is_ctx = pl.program_id(0) < ctx_tiles

    def pick(i):
        return jnp.where(is_ctx, mod_ref[i:i + 1, :], mod_ref[i + 1:i + 2, :])

    x = x_ref[...]
    if has_m:
        x = x + pick(0) * _rms(m_ref[...].astype(F32), g_ref[0:1, :])
        refs.pop(0)[...] = x
    if emit_h:
        h = _rms(x, g_ref[1:2, :]) * (1.0 + pick(2)) + pick(4)
        refs.pop(0)[...] = h.astype(BF16)


def _join_norm_kernel(ctx_ref, x_ref, g_ref, mod_ref, xs_ref, h_ref, *, ctx_tiles):
    is_ctx = pl.program_id(0) < ctx_tiles
    x = jnp.where(is_ctx, ctx_ref[...], x_ref[...])
    xs_ref[...] = x
    scale = jnp.where(is_ctx, mod_ref[0:1, :], mod_ref[1:2, :])
    shift = jnp.where(is_ctx, mod_ref[2:3, :], mod_ref[3:4, :])
    h_ref[...] = (_rms(x, g_ref[...]) * (1.0 + scale) + shift).astype(BF16)


def _join_norm(ctx, x, g, mod4):
    L, D = ctx.shape
    T = x.shape[0]
    te = _tile(math.gcd(L, T), 256, 16)
    ctx_tiles = L // te
    row = pl.BlockSpec((te, D), lambda i: (i, 0))
    return pl.pallas_call(
        functools.partial(_join_norm_kernel, ctx_tiles=ctx_tiles),
        grid=((L + T) // te,),
        in_specs=[pl.BlockSpec((te, D), lambda i: (jnp.minimum(i, ctx_tiles - 1), 0)),
                  pl.BlockSpec((te, D), lambda i: (jnp.maximum(i - ctx_tiles, 0), 0)),
                  pl.BlockSpec((1, D), lambda i: (0, 0)), pl.BlockSpec((4, D), lambda i: (0, 0))],
        out_specs=[row, row],
        out_shape=[jax.ShapeDtypeStruct((L + T, D), F32), jax.ShapeDtypeStruct((L + T, D), BF16)],
        compiler_params=_params("arbitrary"),
        name="join_norm",
    )(ctx, x, g, mod4)


def _resid_norm(x, m, g2, mod6, *, n_ctx, emit_h, latent_only=False):
    R, D = x.shape
    te = _tile(math.gcd(n_ctx, R - n_ctx), 256, 16)
    has_m = m is not None
    skip = n_ctx // te if latent_only else 0
    n_out = R - skip * te
    row_in = pl.BlockSpec((te, D), lambda i: (i + skip, 0))
    row = pl.BlockSpec((te, D), lambda i: (i, 0))
    ins = [x] + ([m] if has_m else []) + [g2, mod6]
    in_specs = [row_in] * (2 if has_m else 1) + [pl.BlockSpec((2, D), lambda i: (0, 0)),
                                                 pl.BlockSpec((6, D), lambda i: (0, 0))]
    out_shape, out_specs = [], []
    if has_m:
        out_shape.append(jax.ShapeDtypeStruct((n_out, D), F32))
        out_specs.append(row)
    if emit_h:
        out_shape.append(jax.ShapeDtypeStruct((n_out, D), BF16))
        out_specs.append(row)
    outs = pl.pallas_call(
        functools.partial(_resid_norm_kernel, ctx_tiles=n_ctx // te - skip, has_m=has_m, emit_h=emit_h),
        grid=(n_out // te,),
        in_specs=in_specs, out_specs=out_specs, out_shape=out_shape,
        compiler_params=_params("parallel"),
        name="resid_norm",
    )(*ins)
    outs = list(outs)
    x1 = outs.pop(0) if has_m else None
    h = outs.pop(0) if emit_h else None
    return x1, h


def _mm_kernel(x_ref, w_ref, o_ref, *scratch, nk, transposed_w):
    dot = _dot_nt if transposed_w else _dot
    part = dot(x_ref[...].astype(BF16), w_ref[...].astype(BF16))
    if nk == 1:
        o_ref[...] = part.astype(o_ref.dtype)
        return
    acc_ref, = scratch
    k = pl.program_id(2)

    @pl.when(k == 0)
    def _():
        acc_ref[...] = part

    @pl.when(k > 0)
    def _():
        acc_ref[...] += part

    @pl.when(k == nk - 1)
    def _():
        o_ref[...] = acc_ref[...].astype(o_ref.dtype)


def _matmul(x, w, layer, *, transposed_w=False, out_dtype=F32, tm_cap=1280, tn_cap=512, tk_cap=4096,
            name="matmul"):
    M, K = x.shape
    N = w.shape[1] if transposed_w else w.shape[2]
    tm = _tile(M, tm_cap, 16)
    tn = _tile(N, tn_cap, LANES)
    tk = _tile(K, tk_cap, LANES)
    nk = K // tk
    if transposed_w:
        w_spec = pl.BlockSpec((None, tn, tk), lambda i, j, k: (layer, j, k))
    else:
        w_spec = pl.BlockSpec((None, tk, tn), lambda i, j, k: (layer, k, j))
    return pl.pallas_call(
        functools.partial(_mm_kernel, nk=nk, transposed_w=transposed_w),
        grid=(M // tm, N // tn, nk),
        in_specs=[pl.BlockSpec((tm, tk), lambda i, j, k: (i, k)), w_spec],
        out_specs=pl.BlockSpec((tm, tn), lambda i, j, k: (i, j)),
        out_shape=jax.ShapeDtypeStruct((M, N), out_dtype),
        scratch_shapes=[pltpu.VMEM((tm, tn), F32)] if nk > 1 else [],
        compiler_params=_params("parallel", "parallel", "arbitrary"),
        name=name,
    )(x, w)


def _ffn_up_kernel(h_ref, w1_ref, w3_ref, o_ref):
    h = h_ref[...]
    o_ref[...] = (_silu(_dot(h, w1_ref[...].astype(BF16))) * _dot(h, w3_ref[...].astype(BF16))).astype(o_ref.dtype)


def _ffn_up(h, w1, w3, layer):
    M, K = h.shape
    N = w1.shape[2]
    tm = _tile(M, 1664, 16)
    tn = _tile(N, 256, LANES)
    wspec = pl.BlockSpec((None, K, tn), lambda i, j: (layer, 0, j))
    return pl.pallas_call(
        _ffn_up_kernel,
        grid=(M // tm, N // tn),
        in_specs=[pl.BlockSpec((tm, K), lambda i, j: (i, 0)), wspec, wspec],
        out_specs=pl.BlockSpec((tm, tn), lambda i, j: (i, j)),
        out_shape=jax.ShapeDtypeStruct((M, N), BF16),
        compiler_params=_params("parallel", "parallel"),
        name="ffn_up",
    )(h, w1, w3)


def _merge_kernel(pg_ref, ya_ref, yb_ref, yc_ref, ga_ref, gb_ref, gc_ref, wa_ref, wb_ref, wc_ref, o_ref):
    pg = pg_ref[...].astype(BF16)
    acc = _sigmoid(_dot(pg, ga_ref[...])) * _dot(ya_ref[...], wa_ref[...])
    acc += _sigmoid(_dot(pg, gb_ref[...])) * _dot(yb_ref[...], wb_ref[...])
    acc += _sigmoid(_dot(pg, gc_ref[...])) * _dot(yc_ref[...], wc_ref[...])
    o_ref[...] = acc.astype(o_ref.dtype)


def _merge(p, ya, yb, yc, gate_up, wa, wb, wc, layer):
    R = p.shape[0]
    D = wa.shape[2]
    tm = _tile(R, 1280, 16)
    tn = _tile(D, 512, LANES)
    nj = D // tn

    def rows(width):
        return pl.BlockSpec((tm, width), lambda i, j: (i, 0))

    def gate(branch):
        return pl.BlockSpec((None, GATE_RANK, tn), lambda i, j: (layer, 0, branch * nj + j))

    def wcol(width):
        return pl.BlockSpec((None, width, tn), lambda i, j: (layer, 0, j))

    return pl.pallas_call(
        _merge_kernel,
        grid=(R // tm, nj),
        in_specs=[pl.BlockSpec((tm, GATE_RANK), lambda i, j: (i, P_G // GATE_RANK)),
                  rows(A_WIDTH), rows(B_WIDTH), rows(C_WIDTH),
                  gate(0), gate(1), gate(2), wcol(A_WIDTH), wcol(B_WIDTH), wcol(C_WIDTH)],
        out_specs=pl.BlockSpec((tm, tn), lambda i, j: (i, j)),
        out_shape=jax.ShapeDtypeStruct((R, D), BF16),
        compiler_params=_params("parallel", "parallel"),
        name="merge",
    )(p, ya, yb, yc, gate_up, gate_up, gate_up, wa, wb, wc)


def _segment_flags(i, ctx_tiles, n_tiles):
    first = jnp.logical_or(i == 0, i == ctx_tiles)
    last = jnp.logical_or(i == ctx_tiles - 1, i == n_tiles - 1)
    return first, last


def _rwkv_prep_kernel(u_ref, prev_ref, next_ref, mu_ref, w0_ref, wup_ref, a0_ref, aup_ref, kk_ref, ka_ref,
                      pu_ref, pprev_ref, pnext_ref, pw_ref, pscale_ref,
                      r_out, v_out, kap_out, lw0_out, lw1_out, k0_out, k1_out, b0_out, b1_out, gs_out, pool_out,
                      ext_ref, *, ctx_tiles, n_tiles, n_ctx, n_lat):
    i = pl.program_id(0)
    first, last = _segment_flags(i, ctx_tiles, n_tiles)
    _pool_tile(i, first, last, pu_ref, pprev_ref, pnext_ref, pw_ref, pscale_ref, pool_out, ext_ref,
               ctx_tiles=ctx_tiles, n_ctx=n_ctx, n_lat=n_lat)
    u = u_ref[...]
    tm = u.shape[0]
    rid = lax.broadcasted_iota(jnp.int32, (tm, 1), 0)
    prev_row = jnp.where(first, 0.0, prev_ref[SUBLANES - 1:SUBLANES, :])
    next_row = jnp.where(last, 0.0, next_ref[0:1, :])
    prev = jnp.where(rid == 0, prev_row, pltpu.roll(u, 1, axis=0))
    nxt = jnp.where(rid == tm - 1, next_row, pltpu.roll(u, tm - 1, axis=0))
    s = u + mu_ref[...] * (0.5 * (prev + nxt) - u)

    W = A_WIDTH
    r = s[:, 0:W]
    k = s[:, W:2 * W]
    v = s[:, 2 * W:3 * W]
    o = 3 * W
    wd = jnp.tanh(s[:, o:o + LANES]).astype(BF16)
    ad = s[:, o + LANES:o + 2 * LANES].astype(BF16)
    gd = s[:, A_GD_OFF:A_PAD]

    ones_bd = _head_block_ones()
    kk = k * kk_ref[...]
    nrm = jnp.sqrt(_head_sums(kk * kk, ones_bd))
    kk = kk / jnp.maximum(nrm, 1e-12)

    r_out[...] = r.astype(BF16)
    v_out[...] = v.astype(BF16)
    kap_out[...] = kk.astype(BF16)
    gs_out[...] = _sigmoid(gd).astype(BF16)
    for d, (lw_out, k_out, b_out) in enumerate(((lw0_out, k0_out, b0_out), (lw1_out, k1_out, b1_out))):
        z = w0_ref[d:d + 1, :] + _dot(wd, wup_ref[d])
        lw_out[...] = -math.exp(-0.5) * _sigmoid(z)
        a = _sigmoid(a0_ref[d:d + 1, :] + _dot(ad, aup_ref[d]))
        k_out[...] = (k * (1.0 + (a - 1.0) * ka_ref[...])).astype(BF16)
        b_out[...] = (kk * a).astype(BF16)


def _rwkv_prep_pool(p, mu, w0, wup2, a0, aup2, k_k, k_a, pool_w, pool_scale, *, n_ctx):
    R = p.shape[0]
    tm = _tile(math.gcd(n_ctx, R - n_ctx), 256, 16)
    n_tiles = R // tm
    assert SUBLANES == POOL_HALO and P_B % B_WIDTH == 0
    hb = tm // SUBLANES
    n_hblocks = R // SUBLANES
    W = A_WIDTH
    cb = P_B // B_WIDTH

    def const(shape):
        return pl.BlockSpec(shape, lambda i: (0,) * len(shape))

    def halo_specs(width, col):
        return [pl.BlockSpec((tm, width), lambda i: (i, col)),
                pl.BlockSpec((SUBLANES, width), lambda i: (jnp.maximum(i * hb - 1, 0), col)),
                pl.BlockSpec((SUBLANES, width), lambda i: (jnp.minimum((i + 1) * hb, n_hblocks - 1), col))]

    wide = pl.BlockSpec((tm, W), lambda i: (i, 0))
    f32w = jax.ShapeDtypeStruct((R, W), F32)
    bf16w = jax.ShapeDtypeStruct((R, W), BF16)
    return pl.pallas_call(
        functools.partial(_rwkv_prep_kernel, ctx_tiles=n_ctx // tm, n_tiles=n_tiles, n_ctx=n_ctx, n_lat=R - n_ctx),
        grid=(n_tiles,),
        in_specs=halo_specs(A_PAD, 0)
        + [const((1, A_PAD)), const((2, W)), const((2, LANES, W)), const((2, W)), const((2, LANES, W)),
           const((1, W)), const((1, W))]
        + halo_specs(B_WIDTH, cb)
        + [const((len(POOL_WINDOWS), POOL_GROUP_W, POOL_GROUP_W)), const((1, B_WIDTH))],
        out_specs=[wide] * 9 + [pl.BlockSpec((tm, A_GD_PAD), lambda i: (i, 0)),
                                pl.BlockSpec((tm, B_WIDTH), lambda i: (i, 0))],
        out_shape=[bf16w] * 3 + [f32w] * 2 + [bf16w] * 4 + [jax.ShapeDtypeStruct((R, A_GD_PAD), BF16),
                                                            jax.ShapeDtypeStruct((R, B_WIDTH), BF16)],
        scratch_shapes=[pltpu.VMEM((tm + 2 * POOL_HALO, B_WIDTH), F32)],
        compiler_params=_params("parallel"),
        name="rwkv_prep_pool",
    )(p, p, p, mu, w0, wup2, a0, aup2, k_k, k_a, p, p, p, pool_w, pool_scale)


def _scan_kernel(*refs, directions, npairs):
    C = SCAN_CHUNK
    N = A_HEAD_DIM
    nd = len(directions)
    in_refs = [refs[6 * d:6 * d + 6] for d in range(nd)]
    y_refs = refs[6 * nd:7 * nd]
    s_refs = refs[7 * nd:8 * nd]

    @pl.when(pl.program_id(1) == 0)
    def _():
        for s_ref in s_refs:
            s_ref[...] = jnp.zeros_like(s_ref)

    row = lax.broadcasted_iota(jnp.int32, (C, C), 0)
    col = lax.broadcasted_iota(jnp.int32, (C, C), 1)
    diag = col == row
    eye = jnp.where(diag, 1.0, 0.0).astype(F32)
    earlier_d = [(col > row) if reverse else (col < row) for reverse in directions]
    incl_d = [jnp.logical_or(e, diag) for e in earlier_d]

    def same_block(n):
        return (row // n) == (col // n)

    hs = (slice(0, N), slice(N, 2 * N))
    x, k_t, b_t, k_e, b_e, e_tot, v, item_dir = ([] for _ in range(8))
    for d, reverse in enumerate(directions):
        lw_ref, k_ref, b_ref, kap_ref, v_ref, r_ref = in_refs[d]
        tri = jnp.where(incl_d[d], 1.0, 0.0).astype(BF16)
        tri3 = jnp.concatenate([tri, tri, tri], axis=1)
        last = 0 if reverse else C - 1
        for p in range(npairs):
            sl = slice(p * LANES, (p + 1) * LANES)
            lw = lw_ref[:, sl]
            c = _dot(tri3, jnp.concatenate(_split3(lw), axis=0))
            ctot = c[last:last + 1, :]
            e_nc = jnp.exp(-c)
            e_tc = jnp.exp(ctot - c)
            kap_p = kap_ref[:, sl].astype(F32) * jnp.exp(c - lw)
            r_p = r_ref[:, sl].astype(F32) * jnp.exp(c)
            k_p = k_ref[:, sl].astype(F32)
            b_p = b_ref[:, sl].astype(F32)
            v_p = v_ref[:, sl]
            e_p = jnp.exp(ctot)
            for h in hs:
                x.append(jnp.concatenate([kap_p[:, h], r_p[:, h]], axis=0).astype(BF16))
                k_t.append((k_p * e_nc)[:, h].astype(BF16))
                b_t.append((b_p * e_nc)[:, h].astype(BF16))
                k_e.append((k_p * e_tc)[:, h].astype(BF16))
                b_e.append((b_p * e_tc)[:, h].astype(BF16))
                e_tot.append(e_p[:, h])
                v.append(v_p[:, h].astype(BF16))
                item_dir.append(d)
    items = range(len(x))
    earlier = [earlier_d[item_dir[i]] for i in items]
    incl = [incl_d[item_dir[i]] for i in items]

    def bd(a, b):
        return _dot(a.astype(BF16), b.astype(BF16))

    g1 = [_dot_nt(x[i], k_t[i]) for i in items]
    g2 = [_dot_nt(x[i], b_t[i]) for i in items]
    a_kk = [jnp.where(earlier[i], g1[i][:C], 0.0).astype(BF16) for i in items]
    a_rk = [jnp.where(incl[i], g1[i][C:], 0.0).astype(BF16) for i in items]
    a_kb = [jnp.where(earlier[i], g2[i][:C], 0.0) for i in items]
    a_rb = [jnp.where(incl[i], g2[i][C:], 0.0).astype(BF16) for i in items]
    blk8 = same_block(8)
    a0 = [jnp.where(blk8, a_kb[i], 0.0) for i in items]
    a2 = [bd(a0[i], a0[i]) for i in items]
    a4 = [bd(a2[i], a2[i]) for i in items]
    t = [bd(eye - a0[i], eye + a2[i]) for i in items]
    t = [bd(t[i], eye + a4[i]) for i in items]
    for n in (16, 32, 64):
        m = jnp.logical_and(same_block(n), jnp.logical_not(same_block(n // 2)))
        off = [jnp.where(m, a_kb[i], 0.0) for i in items]
        ot = [bd(off[i], t[i]) for i in items]
        t = [t[i] - bd(t[i], ot[i]) for i in items]
    per_dir = 2 * npairs
    s0 = [s_refs[item_dir[i]][i % per_dir] for i in items]
    xs = [_dot_nt(x[i], s0[i].astype(BF16)) for i in items]
    akv = [_dot(a_kk[i], v[i]) for i in items]
    u = [bd(t[i], xs[i][:C] + akv[i]).astype(BF16) for i in items]
    ys = [xs[i][C:] + _dot(a_rk[i], v[i]) - _dot(a_rb[i], u[i]) for i in items]
    for i in items:
        s_refs[item_dir[i]][i % per_dir] = s0[i] * e_tot[i] + _dot_tn(v[i], k_e[i]) - _dot_tn(u[i], b_e[i])
    for d in range(nd):
        for p in range(npairs):
            i = d * per_dir + 2 * p
            y_refs[d][:, p * LANES:(p + 1) * LANES] = jnp.concatenate([ys[i], ys[i + 1]], axis=1)


def _delta_scan(streams, *, n_ctx, pairs_per_block=None):
    R, W = streams[0][1][0].shape
    C = SCAN_CHUNK
    nchunks = R // C
    ctx_chunks = n_ctx // C
    npairs = W // LANES
    pb = npairs if pairs_per_block is None else pairs_per_block
    assert R % C == 0 and n_ctx % C == 0 and npairs % pb == 0

    def spec(reverse):
        if reverse:
            return pl.BlockSpec((C, pb * LANES), lambda g, s: (
                jnp.where(s < ctx_chunks, ctx_chunks - 1 - s, nchunks - 1 - (s - ctx_chunks)), g))
        return pl.BlockSpec((C, pb * LANES), lambda g, s: (s, g))

    directions = tuple(rev for rev, _ in streams)
    return pl.pallas_call(
        functools.partial(_scan_kernel, directions=directions, npairs=pb),
        grid=(npairs // pb, nchunks),
        in_specs=[spec(rev) for rev in directions for _ in range(6)],
        out_specs=[spec(rev) for rev in directions],
        out_shape=[jax.ShapeDtypeStruct((R, W), F32)] * len(streams),
        scratch_shapes=[pltpu.VMEM((2 * pb, A_HEAD_DIM, A_HEAD_DIM), F32)] * len(streams),
        compiler_params=_params("parallel", "arbitrary"),
        name="delta_scan",
    )(*[a for _, arrays in streams for a in arrays])


def _rwkv_readout_kernel(yf_ref, yb_ref, r_ref, v_ref, k0_ref, k1_ref, gs_ref, gup_ref, rk_ref, lng_ref, lnb_ref,
                         o_ref):
    ones_bd = _head_block_ones()
    inv_n = 1.0 / A_HEAD_DIM
    ro = yf_ref[...] + yb_ref[...]
    mu = _head_sums(ro, ones_bd) * inv_n
    cen = ro - mu
    var = _head_sums(cen * cen, ones_bd) * inv_n
    yn = cen * lax.rsqrt(var + A_GN_EPS) * lng_ref[...] + lnb_ref[...]
    rk = r_ref[...].astype(F32) * (k0_ref[...].astype(F32) + k1_ref[...].astype(F32)) * rk_ref[...]
    bonus = _head_sums(rk, ones_bd) * v_ref[...].astype(F32)
    g = _dot(gs_ref[...], gup_ref[...])
    o_ref[...] = ((yn + bonus) * g).astype(o_ref.dtype)


def _rwkv_readout(yf, yb, r, v, k0, k1, gs, gup, r_k, ln_g, ln_b):
    R, W = yf.shape
    tm = _tile(R, 256, 16)
    wide = pl.BlockSpec((tm, W), lambda i: (i, 0))
    vec = pl.BlockSpec((1, W), lambda i: (0, 0))
    return pl.pallas_call(
        _rwkv_readout_kernel,
        grid=(R // tm,),
        in_specs=[wide] * 6 + [pl.BlockSpec((tm, A_GD_PAD), lambda i: (i, 0)),
                               pl.BlockSpec((A_GD_PAD, W), lambda i: (0, 0)), vec, vec, vec],
        out_specs=wide,
        out_shape=jax.ShapeDtypeStruct((R, W), BF16),
        compiler_params=_params("parallel"),
        name="rwkv_readout",
    )(yf, yb, r, v, k0, k1, gs, gup, r_k, ln_g, ln_b)


def _pool_tile(i, first, last, u_ref, prev_ref, next_ref, w_ref, scale_ref, o_ref, ext_ref, *, ctx_tiles, n_ctx, n_lat):
    tm = u_ref.shape[0]
    H = POOL_HALO
    ext_ref[0:H, :] = jnp.where(first, 0.0, prev_ref[...])
    ext_ref[H:H + tm, :] = u_ref[...]
    ext_ref[H + tm:H + tm + H, :] = jnp.where(last, 0.0, next_ref[...])
    is_ctx = i < ctx_tiles
    seg_len = jnp.where(is_ctx, n_ctx, n_lat)
    t = lax.broadcasted_iota(jnp.int32, (tm, 1), 0) + i * tm - jnp.where(is_ctx, 0, n_ctx)
    for gi, win in enumerate(POOL_WINDOWS):
        cols = slice(gi * POOL_GROUP_W, (gi + 1) * POOL_GROUP_W)
        acc = None
        for o in range(-(win // 2), win // 2):
            term = ext_ref[H + o:H + o + tm, cols]
            acc = term if acc is None else acc + term
        lo = jnp.maximum(t - win // 2, 0)
        hi = jnp.minimum(t + win // 2 - 1, seg_len - 1)
        cnt = (hi - lo + 1).astype(F32)
        pooled = acc / cnt - u_ref[:, cols]
        y = _dot(pooled.astype(BF16), w_ref[gi]) * scale_ref[:, cols]
        o_ref[:, cols] = y.astype(o_ref.dtype)


def _rope(x, cos, sin):
    return x * cos + pltpu.roll(x, C_HEAD_DIM // 2, axis=1) * sin


def _rope_tables(n_ctx, n_lat):
    half = C_HEAD_DIM // 2
    t = jnp.arange(n_lat)
    row = (t // GRID_W).astype(F32)
    col = (t % GRID_W).astype(F32)
    inv = ROPE_BASE ** (-jnp.arange(0, half, 2, dtype=F32) / half)
    ar = row[:, None] * inv[None]
    ac = col[:, None] * inv[None]
    cos = jnp.concatenate([jnp.cos(ar), jnp.cos(ac), jnp.cos(ar), jnp.cos(ac)], axis=1)
    sin = jnp.concatenate([-jnp.sin(ar), -jnp.sin(ac), jnp.sin(ar), jnp.sin(ac)], axis=1)
    cos = jnp.concatenate([jnp.ones((n_ctx, C_HEAD_DIM), F32), cos], axis=0)
    sin = jnp.concatenate([jnp.zeros((n_ctx, C_HEAD_DIM), F32), sin], axis=0)
    return cos, sin


def _attn_kernel(q0_ref, q1_ref, q2_ref, q3_ref, kp_ref, kc_ref, kn_ref, kx_ref, vp_ref, vc_ref, vn_ref, vx_ref,
                 cp_ref, sp_ref, co_ref, so_ref, cn_ref, sn_ref, sink_ref, o_ref, *, ctx_qblocks, n_qblocks):
    i = pl.program_id(0)
    B = ATTN_BLOCK
    q_refs = (q0_ref, q1_ref, q2_ref, q3_ref)
    tq = q0_ref.shape[0]
    cos_o = co_ref[...]
    sin_o = so_ref[...]
    nloc = tq + 2 * B
    qrow = lax.broadcasted_iota(jnp.int32, (tq, nloc), 0)
    kcol = lax.broadcasted_iota(jnp.int32, (tq, nloc), 1)
    rel = kcol - B - qrow
    lo = jnp.where(i == ctx_qblocks, B, 0)
    hi = jnp.where(i < ctx_qblocks, 0, jnp.where(i == n_qblocks - 1, B + tq, nloc))
    bias = jnp.where(jnp.abs(rel) <= ATTN_BLOCK, 0.0, NEG_INF)
    bias = jnp.where(kcol >= lo, bias, NEG_INF)
    bias = jnp.where(kcol < hi, bias, NEG_INF)
    scale = C_HEAD_DIM ** -0.5
    for h in range(C_KV_HEADS):
        kc = slice(h * C_HEAD_DIM, (h + 1) * C_HEAD_DIM)
        k_loc = jnp.concatenate([_rope(kp_ref[:, kc], cp_ref[...], sp_ref[...]).astype(BF16),
                                 _rope(kc_ref[:, kc], cos_o, sin_o).astype(BF16),
                                 _rope(kn_ref[:, kc], cn_ref[...], sn_ref[...]).astype(BF16)], axis=0)
        v_loc = jnp.concatenate([vp_ref[:, kc].astype(BF16), vc_ref[:, kc].astype(BF16),
                                 vn_ref[:, kc].astype(BF16)], axis=0)
        k_ctx = kx_ref[:, kc].astype(BF16)
        v_ctx = vx_ref[:, kc].astype(BF16)
        for g in range(C_GROUP):
            j = h * C_GROUP + g
            cols = slice(j * C_HEAD_DIM, (j + 1) * C_HEAD_DIM)
            q = _rope(q_refs[h][:, g * C_HEAD_DIM:(g + 1) * C_HEAD_DIM], cos_o, sin_o).astype(BF16)
            s_loc = _dot_nt(q, k_loc) * scale + bias
            s_ctx = _dot_nt(q, k_ctx) * scale
            sk = sink_ref[j:j + 1, 0:1]
            m = jnp.maximum(jnp.maximum(jnp.max(s_loc, axis=-1, keepdims=True),
                                        jnp.max(s_ctx, axis=-1, keepdims=True)), sk)
            e_loc = jnp.exp(s_loc - m)
            e_ctx = jnp.exp(s_ctx - m)
            denom = (jnp.sum(e_loc, axis=-1, keepdims=True) + jnp.sum(e_ctx, axis=-1, keepdims=True)
                     + jnp.exp(sk - m))
            o = _dot(e_loc.astype(BF16), v_loc) + _dot(e_ctx.astype(BF16), v_ctx)
            o_ref[:, cols] = (o / denom).astype(o_ref.dtype)


def _attention(p, cos_tab, sin_tab, sink16, *, n_ctx):
    R = p.shape[0]
    B = ATTN_BLOCK
    tq = _tile(math.gcd(n_ctx, R - n_ctx), 2 * B, B)
    per = tq // B
    n_qblocks = R // tq
    n_blocks = R // B
    ctx_blocks = n_ctx // B
    W = C_KV_WIDTH
    assert n_ctx % B == 0 and R % B == 0 and P_Q % W == 0 and P_K % W == 0 and P_V % W == 0
    assert C_GROUP * C_HEAD_DIM == W
    kcol, vcol = P_K // W, P_V // W

    def prev_rows(i):
        return jnp.clip(i * per - 1, ctx_blocks, n_blocks - 1)

    def next_rows(i):
        return jnp.clip((i + 1) * per, ctx_blocks, n_blocks - 1)

    def kv_specs(c):
        return [pl.BlockSpec((B, W), lambda i: (prev_rows(i), c)), pl.BlockSpec((tq, W), lambda i: (i, c)),
                pl.BlockSpec((B, W), lambda i: (next_rows(i), c)), pl.BlockSpec((n_ctx, W), lambda i: (0, c))]

    tab_specs = []
    for rows, fn in ((B, prev_rows), (tq, lambda i: i), (B, next_rows)):
        tab_specs += [pl.BlockSpec((rows, C_HEAD_DIM), lambda i, fn=fn: (fn(i), 0))] * 2
    q_specs = [pl.BlockSpec((tq, W), lambda i, h=h: (i, P_Q // W + h)) for h in range(C_KV_HEADS)]

    return pl.pallas_call(
        functools.partial(_attn_kernel, ctx_qblocks=n_ctx // tq, n_qblocks=n_qblocks),
        grid=(n_qblocks,),
        in_specs=q_specs + kv_specs(kcol) + kv_specs(vcol) + tab_specs
        + [pl.BlockSpec((C_Q_HEADS, LANES), lambda i: (0, 0))],
        out_specs=pl.BlockSpec((tq, C_WIDTH), lambda i: (i, 0)),
        out_shape=jax.ShapeDtypeStruct((R, C_WIDTH), BF16),
        compiler_params=_params("parallel"),
        name="window_attention",
    )(*([p] * 12), cos_tab, sin_tab, cos_tab, sin_tab, cos_tab, sin_tab, sink16)


def _pad_to(a, axis, size):
    pad = [(0, 0)] * a.ndim
    pad[axis] = (0, size - a.shape[axis])
    return jnp.pad(a, pad)


def _relayout_w_in_kernel(w_ref, o_ref):
    quarter = C_HEAD_DIM // 4

    def copy_rows(dst, src, n):
        o_ref[dst:dst + n, :] = w_ref[src:src + n, :].astype(o_ref.dtype)

    def copy_heads(dst, src, width):
        for h in range(0, width, C_HEAD_DIM):
            for new, old in enumerate((0, 2, 1, 3)):
                copy_rows(dst + h + new * quarter, src + h + old * quarter, quarter)

    copy_rows(0, 0, OFF_B)
    o_ref[OFF_B:A_PAD, :] = jnp.zeros((A_PAD - OFF_B, o_ref.shape[1]), o_ref.dtype)
    copy_heads(P_K, OFF_K, C_KV_WIDTH)
    copy_rows(P_V, OFF_V, C_KV_WIDTH)
    copy_rows(P_B, OFF_B, B_WIDTH)
    copy_heads(P_Q, OFF_Q, C_WIDTH)
    copy_rows(P_G, OFF_G, GATE_RANK)


def _relayout_w_in(w_in):
    depth, D, n = w_in.shape
    assert n == IN_COLS
    w_t = jnp.swapaxes(w_in, 1, 2)
    tc = _tile(D, 256, LANES)
    return pl.pallas_call(
        _relayout_w_in_kernel,
        grid=(depth, D // tc),
        in_specs=[pl.BlockSpec((None, n, tc), lambda l, i: (l, 0, i))],
        out_specs=pl.BlockSpec((None, P_COLS, tc), lambda l, i: (l, 0, i)),
        out_shape=jax.ShapeDtypeStruct((depth, P_COLS, D), BF16),
        compiler_params=_params("parallel", "parallel"),
        name="w_in_relayout",
    )(w_t)


def _low_rank_pair(up):
    z = jnp.zeros_like(up[:, 0])
    return jnp.stack([jnp.concatenate([up[:, 0], z], axis=1), jnp.concatenate([z, up[:, 1]], axis=1)], axis=1).astype(BF16)


def kernel(x, c, ctx, c_ctx, mod_down, mod_up, mod_b, norm_g, w_in, shift_mu, rwkv_w0, rwkv_w_up, rwkv_a0,
           rwkv_a_up, rwkv_g_up, rwkv_k_k, rwkv_k_a, rwkv_r_k, rwkv_ln_g, rwkv_ln_b, pool_w, pool_scale,
           attn_sink, gate_up, w_branch_a, w_branch_b, w_branch_c, w_out, ffn_w1, ffn_w3, ffn_w2):
    assert x.shape[0] == 1 and ctx.shape[0] == 1 and c.shape[0] == 1
    depth = w_in.shape[0]
    T, D = x.shape[1], x.shape[2]
    L = ctx.shape[1]
    d_ff = ffn_w2.shape[1]
    assert 2 * A_DECAY_RANK == LANES and 2 * A_ICLR_RANK == LANES

    w_in_p = _relayout_w_in(w_in)
    mu_p = _pad_to(shift_mu, -1, A_PAD)[:, None, :]
    wup2 = _low_rank_pair(rwkv_w_up)
    aup2 = _low_rank_pair(rwkv_a_up)
    gup_p = _pad_to(rwkv_g_up, 1, A_GD_PAD).astype(BF16)
    pool_w_b = pool_w.astype(BF16)
    gate_up_b = gate_up.astype(BF16)
    wa_b, wb_b, wc_b = (w.astype(BF16) for w in (w_branch_a, w_branch_b, w_branch_c))
    w2_b = ffn_w2.astype(BF16)
    sink16 = jnp.broadcast_to(attn_sink[..., None], (depth, C_Q_HEADS, LANES))
    cos_tab, sin_tab = _rope_tables(L, T)

    c8 = _pad_to(jnp.concatenate([c_ctx[None], c], axis=0), 0, SUBLANES)
    mod = _modulation(c8, mod_down, mod_up, mod_b).reshape(depth, SUBLANES, 6, D)

    def mod6(l, shift_i, scale_i, gate_i):
        m = mod[l]
        return jnp.stack([m[0, gate_i], m[1, gate_i], m[0, scale_i], m[1, scale_i], m[0, shift_i], m[1, shift_i]])

    xs, h = _join_norm(ctx[0], x[0], norm_g[0, 0][None], mod6(0, 0, 1, 2)[2:])
    for l in range(depth):
        p = _matmul(h, w_in_p, l, transposed_w=True, tn_cap=768, name="w_in")
        r, v, kap, lw0, lw1, k0, k1, b0, b1, gs, y_b = _rwkv_prep_pool(
            p, mu_p[l], rwkv_w0[l], wup2[l], rwkv_a0[l], aup2[l], rwkv_k_k[l][None], rwkv_k_a[l][None],
            pool_w_b[l], pool_scale[l][None], n_ctx=L)
        yf, yr = _delta_scan([(False, (lw0, k0, b0, kap, v, r)), (True, (lw1, k1, b1, kap, v, r))], n_ctx=L)
        y_a = _rwkv_readout(yf, yr, r, v, k0, k1, gs, gup_p[l], rwkv_r_k[l][None], rwkv_ln_g[l][None],
                            rwkv_ln_b[l][None])
        y_c = _attention(p, cos_tab, sin_tab, sink16[l], n_ctx=L)
        acc = _merge(p, y_a, y_b, y_c, gate_up_b, wa_b, wb_b, wc_b, l)
        mix = _matmul(acc, w_out, l, out_dtype=BF16, tm_cap=1664, name="w_out")
        xs, h2 = _resid_norm(xs, mix, norm_g[l, 1:3], mod6(l, 3, 4, 2), n_ctx=L, emit_h=True)
        f = _matmul(_ffn_up(h2, ffn_w1, ffn_w3, l), w2_b, l, out_dtype=BF16, tm_cap=640, tk_cap=d_ff, name="ffn_down")
        if l + 1 < depth:
            g2 = jnp.stack([norm_g[l, 3], norm_g[l + 1, 0]])
            m6 = jnp.concatenate([mod6(l, 0, 1, 5)[:2], mod6(l + 1, 0, 1, 2)[2:]], axis=0)
            xs, h = _resid_norm(xs, f, g2, m6, n_ctx=L, emit_h=True)
        else:
            xs, _ = _resid_norm(xs, f, jnp.stack([norm_g[l, 3], norm_g[l, 3]]), mod6(l, 0, 1, 5), n_ctx=L,
                                emit_h=False, latent_only=True)
    return xs[None]
```

```python
import functools
import math

import jax
import jax.numpy as jnp
from jax import lax
from jax.experimental import pallas as pl
from jax.experimental.pallas import tpu as pltpu

F32 = jnp.float32
BF16 = jnp.bfloat16

LANES = 128
SUBLANES = 8
VMEM_LIMIT = 56 * 1024 * 1024
BF16_SUBLANES = 16

ROW_TILE = 256
MM_ROWS = 1280
MM_ROWS_WIDE = 1664
MM_COLS = 512
W_IN_COLS = 768
FFN_UP_COLS = 256
FFN_DOWN_ROWS = 640

NORM_EPS = 1e-6
GRID_W = 64
ROPE_BASE = 10000.0
NEG_INF = -1e30

A_HEADS = 24
A_HEAD_DIM = 64
A_WIDTH = A_HEADS * A_HEAD_DIM
A_DECAY_RANK = 64
A_ICLR_RANK = 64
A_GATE_RANK = 224
A_GN_EPS = 64e-5
A_COLS = 3 * A_WIDTH + 2 * A_DECAY_RANK + 2 * A_ICLR_RANK + A_GATE_RANK
SCAN_CHUNK = 64
POOL_WINDOWS = (2, 4, 8, 16)
POOL_GROUP_W = 384
B_WIDTH = len(POOL_WINDOWS) * POOL_GROUP_W
POOL_HALO = 8
C_Q_HEADS = 16
C_KV_HEADS = 4
C_GROUP = C_Q_HEADS // C_KV_HEADS
C_HEAD_DIM = 128
C_WIDTH = C_Q_HEADS * C_HEAD_DIM
C_KV_WIDTH = C_KV_HEADS * C_HEAD_DIM
ATTN_BLOCK = 128
GATE_RANK = 256
N_BRANCH = 3

OFF_B = A_COLS
OFF_Q = OFF_B + B_WIDTH
OFF_K = OFF_Q + C_WIDTH
OFF_V = OFF_K + C_KV_WIDTH
OFF_G = OFF_V + C_KV_WIDTH
IN_COLS = OFF_G + GATE_RANK
A_PAD = 5120
A_GD_OFF = 3 * A_WIDTH + 2 * A_DECAY_RANK + 2 * A_ICLR_RANK
A_GD_PAD = A_PAD - A_GD_OFF
P_K = A_PAD
P_V = P_K + C_KV_WIDTH
P_B = P_V + C_KV_WIDTH
P_Q = P_B + B_WIDTH
P_G = P_Q + C_WIDTH
P_COLS = P_G + GATE_RANK


def _dot(a, b):
    return jnp.dot(a, b, preferred_element_type=F32)


def _dot_nt(a, b):
    return lax.dot_general(a, b, (((1,), (1,)), ((), ())), preferred_element_type=F32)


def _dot_tn(a, b):
    return lax.dot_general(a, b, (((0,), (0,)), ((), ())), preferred_element_type=F32)


def _tile(n, cap, mult):
    best = None
    for t in range(mult, min(n, cap) + 1, mult):
        if n % t == 0:
            best = t
    assert best is not None, (n, cap, mult)
    return best


def _params(*sem):
    return pltpu.CompilerParams(dimension_semantics=sem, vmem_limit_bytes=VMEM_LIMIT)


def _sigmoid(x):
    return 1.0 / (1.0 + jnp.exp(-x))


def _silu(x):
    return x * _sigmoid(x)


def _split3(x):
    hi = x.astype(BF16)
    rem = x - hi.astype(F32)
    mid = rem.astype(BF16)
    lo = (rem - mid.astype(F32)).astype(BF16)
    return hi, mid, lo


def _head_block_ones():
    r = lax.broadcasted_iota(jnp.int32, (3 * LANES, LANES), 0)
    c = lax.broadcasted_iota(jnp.int32, (3 * LANES, LANES), 1)
    return jnp.where(((r % LANES) // A_HEAD_DIM) == (c // A_HEAD_DIM), 1.0, 0.0).astype(BF16)


def _head_sums(x, ones_bd):
    out = []
    for j in range(0, x.shape[1], LANES):
        out.append(_dot(jnp.concatenate(_split3(x[:, j:j + LANES]), axis=1), ones_bd))
    return jnp.concatenate(out, axis=1)


def _mod_kernel(c_ref, down_ref, up_ref, b_ref, o_ref):
    s = _silu(c_ref[...]).astype(BF16)
    low = _dot(s, down_ref[0].astype(BF16)).astype(BF16)
    o_ref[0] = _dot(low, up_ref[0].astype(BF16)) + b_ref[0]


def _modulation(c8, down, up, bias):
    depth, d, rank = down.shape
    n = up.shape[2]
    tn = _tile(n, d, LANES)
    return pl.pallas_call(
        _mod_kernel,
        grid=(depth, n // tn),
        in_specs=[pl.BlockSpec((SUBLANES, d), lambda l, j: (0, 0)),
                  pl.BlockSpec((1, d, rank), lambda l, j: (l, 0, 0)),
                  pl.BlockSpec((1, rank, tn), lambda l, j: (l, 0, j)),
                  pl.BlockSpec((1, 1, tn), lambda l, j: (l, 0, j))],
        out_specs=pl.BlockSpec((1, SUBLANES, tn), lambda l, j: (l, 0, j)),
        out_shape=jax.ShapeDtypeStruct((depth, SUBLANES, n), F32),
        compiler_params=_params("arbitrary", "arbitrary"),
        name="modulation",
    )(c8, down, up, bias.reshape(depth, 1, n))


def _rms(x, g):
    return x * lax.rsqrt(jnp.mean(x * x, axis=-1, keepdims=True) + NORM_EPS) * g


def _resid_norm_kernel(x_ref, m_ref, g_ref, mod_ref, x1_ref, *h_ref, ctx_tiles):
    is_ctx = pl.program_id(0) < ctx_tiles

    def pick(i):
        return jnp.where(is_ctx, mod_ref[i:i + 1, :], mod_ref[i + 1:i + 2, :])

    x = x_ref[...] + pick(0) * _rms(m_ref[...].astype(F32), g_ref[0:1, :])
    x1_ref[...] = x
    if h_ref:
        h = _rms(x, g_ref[1:2, :]) * (1.0 + pick(2)) + pick(4)
        h_ref[0][...] = h.astype(BF16)


def _join_norm_kernel(ctx_ref, x_ref, g_ref, mod_ref, xs_ref, h_ref, *, ctx_tiles):
    is_ctx = pl.program_id(0) < ctx_tiles
    x = jnp.where(is_ctx, ctx_ref[...], x_ref[...])
    xs_ref[...] = x
    scale = jnp.where(is_ctx, mod_ref[0:1, :], mod_ref[1:2, :])
    shift = jnp.where(is_ctx, mod_ref[2:3, :], mod_ref[3:4, :])
    h_ref[...] = (_rms(x, g_ref[...]) * (1.0 + scale) + shift).astype(BF16)


def _join_norm(ctx, x, g, mod4):
    L, D = ctx.shape
    T = x.shape[0]
    te = _tile(math.gcd(L, T), ROW_TILE, BF16_SUBLANES)
    ctx_tiles = L // te
    row = pl.BlockSpec((te, D), lambda i: (i, 0))
    return pl.pallas_call(
        functools.partial(_join_norm_kernel, ctx_tiles=ctx_tiles),
        grid=((L + T) // te,),
        in_specs=[pl.BlockSpec((te, D), lambda i: (jnp.minimum(i, ctx_tiles - 1), 0)),
                  pl.BlockSpec((te, D), lambda i: (jnp.maximum(i - ctx_tiles, 0), 0)),
                  pl.BlockSpec((1, D), lambda i: (0, 0)), pl.BlockSpec((4, D), lambda i: (0, 0))],
        out_specs=[row, row],
        out_shape=[jax.ShapeDtypeStruct((L + T, D), F32), jax.ShapeDtypeStruct((L + T, D), BF16)],
        compiler_params=_params("arbitrary"),
        name="join_norm",
    )(ctx, x, g, mod4)


def _resid_norm(x, m, g2, mod6, *, n_ctx, emit_h, latent_only=False):
    R, D = x.shape
    te = _tile(math.gcd(n_ctx, R - n_ctx), ROW_TILE, BF16_SUBLANES)
    skip = n_ctx // te if latent_only else 0
    n_out = R - skip * te
    row_in = pl.BlockSpec((te, D), lambda i: (i + skip, 0))
    row = pl.BlockSpec((te, D), lambda i: (i, 0))
    out_shape = [jax.ShapeDtypeStruct((n_out, D), F32)] + ([jax.ShapeDtypeStruct((n_out, D), BF16)] if emit_h else [])
    outs = pl.pallas_call(
        functools.partial(_resid_norm_kernel, ctx_tiles=n_ctx // te - skip),
        grid=(n_out // te,),
        in_specs=[row_in, row_in, pl.BlockSpec((2, D), lambda i: (0, 0)), pl.BlockSpec((6, D), lambda i: (0, 0))],
        out_specs=[row] * len(out_shape), out_shape=out_shape,
        compiler_params=_params("parallel"),
        name="resid_norm",
    )(x, m, g2, mod6)
    return outs[0], (outs[1] if emit_h else None)


def _mm_kernel(x_ref, w_ref, o_ref, *scratch, nk, transposed_w):
    dot = _dot_nt if transposed_w else _dot
    part = dot(x_ref[...].astype(BF16), w_ref[...].astype(BF16))
    if nk == 1:
        o_ref[...] = part.astype(o_ref.dtype)
        return
    acc_ref, = scratch
    k = pl.program_id(2)

    @pl.when(k == 0)
    def _():
        acc_ref[...] = part

    @pl.when(k > 0)
    def _():
        acc_ref[...] += part

    @pl.when(k == nk - 1)
    def _():
        o_ref[...] = acc_ref[...].astype(o_ref.dtype)


def _matmul(x, w, layer, *, transposed_w=False, out_dtype=F32, tm_cap=MM_ROWS, tn_cap=MM_COLS, tk_cap=4096,
            name="matmul"):
    M, K = x.shape
    N = w.shape[1] if transposed_w else w.shape[2]
    tm = _tile(M, tm_cap, BF16_SUBLANES)
    tn = _tile(N, tn_cap, LANES)
    tk = _tile(K, tk_cap, LANES)
    nk = K // tk
    if transposed_w:
        w_spec = pl.BlockSpec((None, tn, tk), lambda i, j, k: (layer, j, k))
    else:
        w_spec = pl.BlockSpec((None, tk, tn), lambda i, j, k: (layer, k, j))
    return pl.pallas_call(
        functools.partial(_mm_kernel, nk=nk, transposed_w=transposed_w),
        grid=(M // tm, N // tn, nk),
        in_specs=[pl.BlockSpec((tm, tk), lambda i, j, k: (i, k)), w_spec],
        out_specs=pl.BlockSpec((tm, tn), lambda i, j, k: (i, j)),
        out_shape=jax.ShapeDtypeStruct((M, N), out_dtype),
        scratch_shapes=[pltpu.VMEM((tm, tn), F32)] if nk > 1 else [],
        compiler_params=_params("parallel", "parallel", "arbitrary"),
        name=name,
    )(x, w)


def _ffn_up_kernel(h_ref, w1_ref, w3_ref, o_ref):
    h = h_ref[...]
    o_ref[...] = (_silu(_dot(h, w1_ref[...].astype(BF16))) * _dot(h, w3_ref[...].astype(BF16))).astype(o_ref.dtype)


def _ffn_up(h, w1, w3, layer):
    M, K = h.shape
    N = w1.shape[2]
    tm = _tile(M, MM_ROWS_WIDE, BF16_SUBLANES)
    tn = _tile(N, FFN_UP_COLS, LANES)
    wspec = pl.BlockSpec((None, K, tn), lambda i, j: (layer, 0, j))
    return pl.pallas_call(
        _ffn_up_kernel,
        grid=(M // tm, N // tn),
        in_specs=[pl.BlockSpec((tm, K), lambda i, j: (i, 0)), wspec, wspec],
        out_specs=pl.BlockSpec((tm, tn), lambda i, j: (i, j)),
        out_shape=jax.ShapeDtypeStruct((M, N), BF16),
        compiler_params=_params("parallel", "parallel"),
        name="ffn_up",
    )(h, w1, w3)


def _merge_kernel(pg_ref, ya_ref, yb_ref, yc_ref, ga_ref, gb_ref, gc_ref, wa_ref, wb_ref, wc_ref, o_ref):
    pg = pg_ref[...].astype(BF16)
    acc = _sigmoid(_dot(pg, ga_ref[...])) * _dot(ya_ref[...], wa_ref[...])
    acc += _sigmoid(_dot(pg, gb_ref[...])) * _dot(yb_ref[...], wb_ref[...])
    acc += _sigmoid(_dot(pg, gc_ref[...])) * _dot(yc_ref[...], wc_ref[...])
    o_ref[...] = acc.astype(o_ref.dtype)


def _merge(p, ya, yb, yc, gate_up, wa, wb, wc, layer):
    R = p.shape[0]
    D = wa.shape[2]
    tm = _tile(R, MM_ROWS, BF16_SUBLANES)
    tn = _tile(D, MM_COLS, LANES)
    nj = D // tn

    def rows(width):
        return pl.BlockSpec((tm, width), lambda i, j: (i, 0))

    def gate(branch):
        return pl.BlockSpec((None, GATE_RANK, tn), lambda i, j: (layer, 0, branch * nj + j))

    def wcol(width):
        return pl.BlockSpec((None, width, tn), lambda i, j: (layer, 0, j))

    return pl.pallas_call(
        _merge_kernel,
        grid=(R // tm, nj),
        in_specs=[pl.BlockSpec((tm, GATE_RANK), lambda i, j: (i, P_G // GATE_RANK)),
                  rows(A_WIDTH), rows(B_WIDTH), rows(C_WIDTH),
                  gate(0), gate(1), gate(2), wcol(A_WIDTH), wcol(B_WIDTH), wcol(C_WIDTH)],
        out_specs=pl.BlockSpec((tm, tn), lambda i, j: (i, j)),
        out_shape=jax.ShapeDtypeStruct((R, D), BF16),
        compiler_params=_params("parallel", "parallel"),
        name="merge",
    )(p, ya, yb, yc, gate_up, gate_up, gate_up, wa, wb, wc)


def _segment_flags(i, ctx_tiles, n_tiles):
    first = jnp.logical_or(i == 0, i == ctx_tiles)
    last = jnp.logical_or(i == ctx_tiles - 1, i == n_tiles - 1)
    return first, last


def _rwkv_prep_kernel(u_ref, prev_ref, next_ref, mu_ref, w0_ref, wup_ref, a0_ref, aup_ref, kk_ref, ka_ref,
                      pu_ref, pprev_ref, pnext_ref, pw_ref, pscale_ref,
                      r_out, v_out, kap_out, lw0_out, lw1_out, k0_out, k1_out, b0_out, b1_out, gs_out, pool_out,
                      ext_ref, *, ctx_tiles, n_tiles, n_ctx, n_lat):
    i = pl.program_id(0)
    first, last = _segment_flags(i, ctx_tiles, n_tiles)
    _pool_tile(i, first, last, pu_ref, pprev_ref, pnext_ref, pw_ref, pscale_ref, pool_out, ext_ref,
               ctx_tiles=ctx_tiles, n_ctx=n_ctx, n_lat=n_lat)
    u = u_ref[...]
    tm = u.shape[0]
    rid = lax.broadcasted_iota(jnp.int32, (tm, 1), 0)
    prev_row = jnp.where(first, 0.0, prev_ref[SUBLANES - 1:SUBLANES, :])
    next_row = jnp.where(last, 0.0, next_ref[0:1, :])
    prev = jnp.where(rid == 0, prev_row, pltpu.roll(u, 1, axis=0))
    nxt = jnp.where(rid == tm - 1, next_row, pltpu.roll(u, tm - 1, axis=0))
    s = u + mu_ref[...] * (0.5 * (prev + nxt) - u)

    W = A_WIDTH
    r = s[:, 0:W]
    k = s[:, W:2 * W]
    v = s[:, 2 * W:3 * W]
    o = 3 * W
    wd = jnp.tanh(s[:, o:o + LANES]).astype(BF16)
    ad = s[:, o + LANES:o + 2 * LANES].astype(BF16)
    gd = s[:, A_GD_OFF:A_PAD]

    ones_bd = _head_block_ones()
    kk = k * kk_ref[...]
    nrm = jnp.sqrt(_head_sums(kk * kk, ones_bd))
    kk = kk / jnp.maximum(nrm, 1e-12)

    r_out[...] = r.astype(BF16)
    v_out[...] = v.astype(BF16)
    kap_out[...] = kk.astype(BF16)
    gs_out[...] = _sigmoid(gd).astype(BF16)
    for d, (lw_out, k_out, b_out) in enumerate(((lw0_out, k0_out, b0_out), (lw1_out, k1_out, b1_out))):
        z = w0_ref[d:d + 1, :] + _dot(wd, wup_ref[d])
        lw_out[...] = -math.exp(-0.5) * _sigmoid(z)
        a = _sigmoid(a0_ref[d:d + 1, :] + _dot(ad, aup_ref[d]))
        k_out[...] = (k * (1.0 + (a - 1.0) * ka_ref[...])).astype(BF16)
        b_out[...] = (kk * a).astype(BF16)


def _rwkv_prep_pool(p, mu, w0, wup2, a0, aup2, k_k, k_a, pool_w, pool_scale, *, n_ctx):
    R = p.shape[0]
    tm = _tile(math.gcd(n_ctx, R - n_ctx), ROW_TILE, BF16_SUBLANES)
    n_tiles = R // tm
    assert SUBLANES == POOL_HALO and P_B % B_WIDTH == 0
    hb = tm // SUBLANES
    n_hblocks = R // SUBLANES
    W = A_WIDTH
    cb = P_B // B_WIDTH

    def const(shape):
        return pl.BlockSpec(shape, lambda i: (0,) * len(shape))

    def halo_specs(width, col):
        return [pl.BlockSpec((tm, width), lambda i: (i, col)),
                pl.BlockSpec((SUBLANES, width), lambda i: (jnp.maximum(i * hb - 1, 0), col)),
                pl.BlockSpec((SUBLANES, width), lambda i: (jnp.minimum((i + 1) * hb, n_hblocks - 1), col))]

    wide = pl.BlockSpec((tm, W), lambda i: (i, 0))
    f32w = jax.ShapeDtypeStruct((R, W), F32)
    bf16w = jax.ShapeDtypeStruct((R, W), BF16)
    return pl.pallas_call(
        functools.partial(_rwkv_prep_kernel, ctx_tiles=n_ctx // tm, n_tiles=n_tiles, n_ctx=n_ctx, n_lat=R - n_ctx),
        grid=(n_tiles,),
        in_specs=halo_specs(A_PAD, 0)
        + [const((1, A_PAD)), const((2, W)), const((2, LANES, W)), const((2, W)), const((2, LANES, W)),
           const((1, W)), const((1, W))]
        + halo_specs(B_WIDTH, cb)
        + [const((len(POOL_WINDOWS), POOL_GROUP_W, POOL_GROUP_W)), const((1, B_WIDTH))],
        out_specs=[wide] * 9 + [pl.BlockSpec((tm, A_GD_PAD), lambda i: (i, 0)),
                                pl.BlockSpec((tm, B_WIDTH), lambda i: (i, 0))],
        out_shape=[bf16w] * 3 + [f32w] * 2 + [bf16w] * 4 + [jax.ShapeDtypeStruct((R, A_GD_PAD), BF16),
                                                            jax.ShapeDtypeStruct((R, B_WIDTH), BF16)],
        scratch_shapes=[pltpu.VMEM((tm + 2 * POOL_HALO, B_WIDTH), F32)],
        compiler_params=_params("parallel"),
        name="rwkv_prep_pool",
    )(p, p, p, mu, w0, wup2, a0, aup2, k_k, k_a, p, p, p, pool_w, pool_scale)


def _scan_kernel(*refs, directions, npairs):
    C = SCAN_CHUNK
    N = A_HEAD_DIM
    nd = len(directions)
    in_refs = [refs[6 * d:6 * d + 6] for d in range(nd)]
    y_refs = refs[6 * nd:7 * nd]
    s_refs = refs[7 * nd:8 * nd]

    @pl.when(pl.program_id(1) == 0)
    def _():
        for s_ref in s_refs:
            s_ref[...] = jnp.zeros_like(s_ref)

    row = lax.broadcasted_iota(jnp.int32, (C, C), 0)
    col = lax.broadcasted_iota(jnp.int32, (C, C), 1)
    diag = col == row
    eye = jnp.where(diag, 1.0, 0.0).astype(F32)
    earlier_d = [(col > row) if reverse else (col < row) for reverse in directions]
    incl_d = [jnp.logical_or(e, diag) for e in earlier_d]

    def same_block(n):
        return (row // n) == (col // n)

    hs = (slice(0, N), slice(N, 2 * N))
    x, k_t, b_t, k_e, b_e, e_tot, v, item_dir = ([] for _ in range(8))
    for d, reverse in enumerate(directions):
        lw_ref, k_ref, b_ref, kap_ref, v_ref, r_ref = in_refs[d]
        tri = jnp.where(incl_d[d], 1.0, 0.0).astype(BF16)
        tri3 = jnp.concatenate([tri, tri, tri], axis=1)
        last = 0 if reverse else C - 1
        for p in range(npairs):
            sl = slice(p * LANES, (p + 1) * LANES)
            lw = lw_ref[:, sl]
            c = _dot(tri3, jnp.concatenate(_split3(lw), axis=0))
            ctot = c[last:last + 1, :]
            e_nc = jnp.exp(-c)
            e_tc = jnp.exp(ctot - c)
            kap_p = kap_ref[:, sl].astype(F32) * jnp.exp(c - lw)
            r_p = r_ref[:, sl].astype(F32) * jnp.exp(c)
            k_p = k_ref[:, sl].astype(F32)
            b_p = b_ref[:, sl].astype(F32)
            v_p = v_ref[:, sl]
            e_p = jnp.exp(ctot)
            for h in hs:
                x.append(jnp.concatenate([kap_p[:, h], r_p[:, h]], axis=0).astype(BF16))
                k_t.append((k_p * e_nc)[:, h].astype(BF16))
                b_t.append((b_p * e_nc)[:, h].astype(BF16))
                k_e.append((k_p * e_tc)[:, h].astype(BF16))
                b_e.append((b_p * e_tc)[:, h].astype(BF16))
                e_tot.append(e_p[:, h])
                v.append(v_p[:, h].astype(BF16))
                item_dir.append(d)
    items = range(len(x))
    earlier = [earlier_d[item_dir[i]] for i in items]
    incl = [incl_d[item_dir[i]] for i in items]

    def bd(a, b):
        return _dot(a.astype(BF16), b.astype(BF16))

    g1 = [_dot_nt(x[i], k_t[i]) for i in items]
    g2 = [_dot_nt(x[i], b_t[i]) for i in items]
    a_kk = [jnp.where(earlier[i], g1[i][:C], 0.0).astype(BF16) for i in items]
    a_rk = [jnp.where(incl[i], g1[i][C:], 0.0).astype(BF16) for i in items]
    a_kb = [jnp.where(earlier[i], g2[i][:C], 0.0) for i in items]
    a_rb = [jnp.where(incl[i], g2[i][C:], 0.0).astype(BF16) for i in items]
    blk8 = same_block(8)
    a0 = [jnp.where(blk8, a_kb[i], 0.0) for i in items]
    a2 = [bd(a0[i], a0[i]) for i in items]
    a4 = [bd(a2[i], a2[i]) for i in items]
    t = [bd(eye - a0[i], eye + a2[i]) for i in items]
    t = [bd(t[i], eye + a4[i]) for i in items]
    for n in (16, 32, 64):
        m = jnp.logical_and(same_block(n), jnp.logical_not(same_block(n // 2)))
        off = [jnp.where(m, a_kb[i], 0.0) for i in items]
        ot = [bd(off[i], t[i]) for i in items]
        t = [t[i] - bd(t[i], ot[i]) for i in items]
    per_dir = 2 * npairs
    s0 = [s_refs[item_dir[i]][i % per_dir] for i in items]
    xs = [_dot_nt(x[i], s0[i].astype(BF16)) for i in items]
    akv = [_dot(a_kk[i], v[i]) for i in items]
    u = [bd(t[i], xs[i][:C] + akv[i]).astype(BF16) for i in items]
    ys = [xs[i][C:] + _dot(a_rk[i], v[i]) - _dot(a_rb[i], u[i]) for i in items]
    for i in items:
        s_refs[item_dir[i]][i % per_dir] = s0[i] * e_tot[i] + _dot_tn(v[i], k_e[i]) - _dot_tn(u[i], b_e[i])
    for d in range(nd):
        for p in range(npairs):
            i = d * per_dir + 2 * p
            y_refs[d][:, p * LANES:(p + 1) * LANES] = jnp.concatenate([ys[i], ys[i + 1]], axis=1)


def _delta_scan(streams, *, n_ctx, pairs_per_block=None):
    R, W = streams[0][1][0].shape
    C = SCAN_CHUNK
    nchunks = R // C
    ctx_chunks = n_ctx // C
    npairs = W // LANES
    pb = npairs if pairs_per_block is None else pairs_per_block
    assert R % C == 0 and n_ctx % C == 0 and npairs % pb == 0

    def spec(reverse):
        if reverse:
            return pl.BlockSpec((C, pb * LANES), lambda g, s: (
                jnp.where(s < ctx_chunks, ctx_chunks - 1 - s, nchunks - 1 - (s - ctx_chunks)), g))
        return pl.BlockSpec((C, pb * LANES), lambda g, s: (s, g))

    directions = tuple(rev for rev, _ in streams)
    return pl.pallas_call(
        functools.partial(_scan_kernel, directions=directions, npairs=pb),
        grid=(npairs // pb, nchunks),
        in_specs=[spec(rev) for rev in directions for _ in range(6)],
        out_specs=[spec(rev) for rev in directions],
        out_shape=[jax.ShapeDtypeStruct((R, W), F32)] * len(streams),
        scratch_shapes=[pltpu.VMEM((2 * pb, A_HEAD_DIM, A_HEAD_DIM), F32)] * len(streams),
        compiler_params=_params("parallel", "arbitrary"),
        name="delta_scan",
    )(*[a for _, arrays in streams for a in arrays])


def _rwkv_readout_kernel(yf_ref, yb_ref, r_ref, v_ref, k0_ref, k1_ref, gs_ref, gup_ref, rk_ref, lng_ref, lnb_ref,
                         o_ref):
    ones_bd = _head_block_ones()
    inv_n = 1.0 / A_HEAD_DIM
    ro = yf_ref[...] + yb_ref[...]
    mu = _head_sums(ro, ones_bd) * inv_n
    cen = ro - mu
    var = _head_sums(cen * cen, ones_bd) * inv_n
    yn = cen * lax.rsqrt(var + A_GN_EPS) * lng_ref[...] + lnb_ref[...]
    rk = r_ref[...].astype(F32) * (k0_ref[...].astype(F32) + k1_ref[...].astype(F32)) * rk_ref[...]
    bonus = _head_sums(rk, ones_bd) * v_ref[...].astype(F32)
    g = _dot(gs_ref[...], gup_ref[...])
    o_ref[...] = ((yn + bonus) * g).astype(o_ref.dtype)


def _rwkv_readout(yf, yb, r, v, k0, k1, gs, gup, r_k, ln_g, ln_b):
    R, W = yf.shape
    tm = _tile(R, ROW_TILE, BF16_SUBLANES)
    wide = pl.BlockSpec((tm, W), lambda i: (i, 0))
    vec = pl.BlockSpec((1, W), lambda i: (0, 0))
    return pl.pallas_call(
        _rwkv_readout_kernel,
        grid=(R // tm,),
        in_specs=[wide] * 6 + [pl.BlockSpec((tm, A_GD_PAD), lambda i: (i, 0)),
                               pl.BlockSpec((A_GD_PAD, W), lambda i: (0, 0)), vec, vec, vec],
        out_specs=wide,
        out_shape=jax.ShapeDtypeStruct((R, W), BF16),
        compiler_params=_params("parallel"),
        name="rwkv_readout",
    )(yf, yb, r, v, k0, k1, gs, gup, r_k, ln_g, ln_b)


def _pool_tile(i, first, last, u_ref, prev_ref, next_ref, w_ref, scale_ref, o_ref, ext_ref, *, ctx_tiles, n_ctx, n_lat):
    tm = u_ref.shape[0]
    H = POOL_HALO
    ext_ref[0:H, :] = jnp.where(first, 0.0, prev_ref[...])
    ext_ref[H:H + tm, :] = u_ref[...]
    ext_ref[H + tm:H + tm + H, :] = jnp.where(last, 0.0, next_ref[...])
    is_ctx = i < ctx_tiles
    seg_len = jnp.where(is_ctx, n_ctx, n_lat)
    t = lax.broadcasted_iota(jnp.int32, (tm, 1), 0) + i * tm - jnp.where(is_ctx, 0, n_ctx)
    for gi, win in enumerate(POOL_WINDOWS):
        cols = slice(gi * POOL_GROUP_W, (gi + 1) * POOL_GROUP_W)
        acc = None
        for o in range(-(win // 2), win // 2):
            term = ext_ref[H + o:H + o + tm, cols]
            acc = term if acc is None else acc + term
        lo = jnp.maximum(t - win // 2, 0)
        hi = jnp.minimum(t + win // 2 - 1, seg_len - 1)
        cnt = (hi - lo + 1).astype(F32)
        pooled = acc / cnt - u_ref[:, cols]
        y = _dot(pooled.astype(BF16), w_ref[gi]) * scale_ref[:, cols]
        o_ref[:, cols] = y.astype(o_ref.dtype)


def _rope(x, cos, sin):
    return x * cos + pltpu.roll(x, C_HEAD_DIM // 2, axis=1) * sin


def _rope_tables(n_ctx, n_lat):
    half = C_HEAD_DIM // 2
    t = jnp.arange(n_lat)
    row = (t // GRID_W).astype(F32)
    col = (t % GRID_W).astype(F32)
    inv = ROPE_BASE ** (-jnp.arange(0, half, 2, dtype=F32) / half)
    ar = row[:, None] * inv[None]
    ac = col[:, None] * inv[None]
    cos = jnp.concatenate([jnp.cos(ar), jnp.cos(ac), jnp.cos(ar), jnp.cos(ac)], axis=1)
    sin = jnp.concatenate([-jnp.sin(ar), -jnp.sin(ac), jnp.sin(ar), jnp.sin(ac)], axis=1)
    cos = jnp.concatenate([jnp.ones((n_ctx, C_HEAD_DIM), F32), cos], axis=0)
    sin = jnp.concatenate([jnp.zeros((n_ctx, C_HEAD_DIM), F32), sin], axis=0)
    return cos, sin


def _attn_kernel(q0_ref, q1_ref, q2_ref, q3_ref, kp_ref, kc_ref, kn_ref, kx_ref, vp_ref, vc_ref, vn_ref, vx_ref,
                 cp_ref, sp_ref, co_ref, so_ref, cn_ref, sn_ref, sink_ref, o_ref, *, ctx_qblocks, n_qblocks):
    i = pl.program_id(0)
    B = ATTN_BLOCK
    q_refs = (q0_ref, q1_ref, q2_ref, q3_ref)
    tq = q0_ref.shape[0]
    cos_o = co_ref[...]
    sin_o = so_ref[...]
    nloc = tq + 2 * B
    qrow = lax.broadcasted_iota(jnp.int32, (tq, nloc), 0)
    kcol = lax.broadcasted_iota(jnp.int32, (tq, nloc), 1)
    rel = kcol - B - qrow
    lo = jnp.where(i == ctx_qblocks, B, 0)
    hi = jnp.where(i < ctx_qblocks, 0, jnp.where(i == n_qblocks - 1, B + tq, nloc))
    bias = jnp.where(jnp.abs(rel) <= ATTN_BLOCK, 0.0, NEG_INF)
    bias = jnp.where(kcol >= lo, bias, NEG_INF)
    bias = jnp.where(kcol < hi, bias, NEG_INF)
    scale = C_HEAD_DIM ** -0.5
    k_loc, v_loc, k_ctx, v_ctx = [], [], [], []
    for h in range(C_KV_HEADS):
        kc = slice(h * C_HEAD_DIM, (h + 1) * C_HEAD_DIM)
        k_loc.append(jnp.concatenate([_rope(kp_ref[:, kc], cp_ref[...], sp_ref[...]).astype(BF16),
                                      _rope(kc_ref[:, kc], cos_o, sin_o).astype(BF16),
                                      _rope(kn_ref[:, kc], cn_ref[...], sn_ref[...]).astype(BF16)], axis=0))
        v_loc.append(jnp.concatenate([vp_ref[:, kc].astype(BF16), vc_ref[:, kc].astype(BF16),
                                      vn_ref[:, kc].astype(BF16)], axis=0))
        k_ctx.append(kx_ref[:, kc].astype(BF16))
        v_ctx.append(vx_ref[:, kc].astype(BF16))
    heads = range(C_Q_HEADS)
    kv = [j // C_GROUP for j in heads]
    q = [_rope(q_refs[kv[j]][:, (j % C_GROUP) * C_HEAD_DIM:(j % C_GROUP + 1) * C_HEAD_DIM], cos_o, sin_o).astype(BF16)
         for j in heads]
    s_loc = [_dot_nt(q[j], k_loc[kv[j]]) * scale + bias for j in heads]
    s_ctx = [_dot_nt(q[j], k_ctx[kv[j]]) * scale for j in heads]
    sk = [sink_ref[j:j + 1, 0:1] for j in heads]
    m = [jnp.maximum(jnp.maximum(jnp.max(s_loc[j], axis=-1, keepdims=True),
                                 jnp.max(s_ctx[j], axis=-1, keepdims=True)), sk[j]) for j in heads]
    e_loc = [jnp.exp(s_loc[j] - m[j]) for j in heads]
    e_ctx = [jnp.exp(s_ctx[j] - m[j]) for j in heads]
    denom = [jnp.sum(e_loc[j], axis=-1, keepdims=True) + jnp.sum(e_ctx[j], axis=-1, keepdims=True)
             + jnp.exp(sk[j] - m[j]) for j in heads]
    o = [_dot(e_loc[j].astype(BF16), v_loc[kv[j]]) + _dot(e_ctx[j].astype(BF16), v_ctx[kv[j]]) for j in heads]
    for j in heads:
        o_ref[:, j * C_HEAD_DIM:(j + 1) * C_HEAD_DIM] = (o[j] / denom[j]).astype(o_ref.dtype)


def _attention(p, cos_tab, sin_tab, sink16, *, n_ctx):
    R = p.shape[0]
    B = ATTN_BLOCK
    tq = _tile(math.gcd(n_ctx, R - n_ctx), 2 * B, B)
    per = tq // B
    n_qblocks = R // tq
    n_blocks = R // B
    ctx_blocks = n_ctx // B
    W = C_KV_WIDTH
    assert n_ctx % B == 0 and R % B == 0 and P_Q % W == 0 and P_K % W == 0 and P_V % W == 0
    assert C_GROUP * C_HEAD_DIM == W
    kcol, vcol = P_K // W, P_V // W

    def prev_rows(i):
        return jnp.clip(i * per - 1, ctx_blocks, n_blocks - 1)

    def next_rows(i):
        return jnp.clip((i + 1) * per, ctx_blocks, n_blocks - 1)

    def kv_specs(c):
        return [pl.BlockSpec((B, W), lambda i: (prev_rows(i), c)), pl.BlockSpec((tq, W), lambda i: (i, c)),
                pl.BlockSpec((B, W), lambda i: (next_rows(i), c)), pl.BlockSpec((n_ctx, W), lambda i: (0, c))]

    tab_specs = []
    for rows, fn in ((B, prev_rows), (tq, lambda i: i), (B, next_rows)):
        tab_specs += [pl.BlockSpec((rows, C_HEAD_DIM), lambda i, fn=fn: (fn(i), 0))] * 2
    q_specs = [pl.BlockSpec((tq, W), lambda i, h=h: (i, P_Q // W + h)) for h in range(C_KV_HEADS)]

    return pl.pallas_call(
        functools.partial(_attn_kernel, ctx_qblocks=n_ctx // tq, n_qblocks=n_qblocks),
        grid=(n_qblocks,),
        in_specs=q_specs + kv_specs(kcol) + kv_specs(vcol) + tab_specs
        + [pl.BlockSpec((C_Q_HEADS, LANES), lambda i: (0, 0))],
        out_specs=pl.BlockSpec((tq, C_WIDTH), lambda i: (i, 0)),
        out_shape=jax.ShapeDtypeStruct((R, C_WIDTH), BF16),
        compiler_params=_params("parallel"),
        name="window_attention",
    )(*([p] * 12), cos_tab, sin_tab, cos_tab, sin_tab, cos_tab, sin_tab, sink16)


def _pad_to(a, axis, size):
    pad = [(0, 0)] * a.ndim
    pad[axis] = (0, size - a.shape[axis])
    return jnp.pad(a, pad)


def _relayout_w_in_kernel(w_ref, o_ref):
    quarter = C_HEAD_DIM // 4

    def copy_rows(dst, src, n):
        o_ref[dst:dst + n, :] = w_ref[src:src + n, :].astype(o_ref.dtype)

    def copy_heads(dst, src, width):
        for h in range(0, width, C_HEAD_DIM):
            for new, old in enumerate((0, 2, 1, 3)):
                copy_rows(dst + h + new * quarter, src + h + old * quarter, quarter)

    copy_rows(0, 0, OFF_B)
    o_ref[OFF_B:A_PAD, :] = jnp.zeros((A_PAD - OFF_B, o_ref.shape[1]), o_ref.dtype)
    copy_heads(P_K, OFF_K, C_KV_WIDTH)
    copy_rows(P_V, OFF_V, C_KV_WIDTH)
    copy_rows(P_B, OFF_B, B_WIDTH)
    copy_heads(P_Q, OFF_Q, C_WIDTH)
    copy_rows(P_G, OFF_G, GATE_RANK)


def _relayout_w_in(w_in):
    depth, D, n = w_in.shape
    assert n == IN_COLS
    w_t = jnp.swapaxes(w_in, 1, 2)
    tc = _tile(D, ROW_TILE, LANES)
    return pl.pallas_call(
        _relayout_w_in_kernel,
        grid=(depth, D // tc),
        in_specs=[pl.BlockSpec((None, n, tc), lambda l, i: (l, 0, i))],
        out_specs=pl.BlockSpec((None, P_COLS, tc), lambda l, i: (l, 0, i)),
        out_shape=jax.ShapeDtypeStruct((depth, P_COLS, D), BF16),
        compiler_params=_params("parallel", "parallel"),
        name="w_in_relayout",
    )(w_t)


def _low_rank_pair(up):
    z = jnp.zeros_like(up[:, 0])
    return jnp.stack([jnp.concatenate([up[:, 0], z], axis=1), jnp.concatenate([z, up[:, 1]], axis=1)], axis=1).astype(BF16)


def kernel(x, c, ctx, c_ctx, mod_down, mod_up, mod_b, norm_g, w_in, shift_mu, rwkv_w0, rwkv_w_up, rwkv_a0,
           rwkv_a_up, rwkv_g_up, rwkv_k_k, rwkv_k_a, rwkv_r_k, rwkv_ln_g, rwkv_ln_b, pool_w, pool_scale,
           attn_sink, gate_up, w_branch_a, w_branch_b, w_branch_c, w_out, ffn_w1, ffn_w3, ffn_w2):
    assert x.shape[0] == 1 and ctx.shape[0] == 1 and c.shape[0] == 1
    depth = w_in.shape[0]
    T, D = x.shape[1], x.shape[2]
    L = ctx.shape[1]
    d_ff = ffn_w2.shape[1]
    assert 2 * A_DECAY_RANK == LANES and 2 * A_ICLR_RANK == LANES

    w_in_p = _relayout_w_in(w_in)
    mu_p = _pad_to(shift_mu, -1, A_PAD)[:, None, :]
    wup2 = _low_rank_pair(rwkv_w_up)
    aup2 = _low_rank_pair(rwkv_a_up)
    gup_p = _pad_to(rwkv_g_up, 1, A_GD_PAD).astype(BF16)
    pool_w_b = pool_w.astype(BF16)
    gate_up_b = gate_up.astype(BF16)
    wa_b, wb_b, wc_b = (w.astype(BF16) for w in (w_branch_a, w_branch_b, w_branch_c))
    w2_b = ffn_w2.astype(BF16)
    sink16 = jnp.broadcast_to(attn_sink[..., None], (depth, C_Q_HEADS, LANES))
    cos_tab, sin_tab = _rope_tables(L, T)

    c8 = _pad_to(jnp.concatenate([c_ctx[None], c], axis=0), 0, SUBLANES)
    mod = _modulation(c8, mod_down, mod_up, mod_b).reshape(depth, SUBLANES, 6, D)

    def mod6(l, shift_i, scale_i, gate_i):
        m = mod[l]
        return jnp.stack([m[0, gate_i], m[1, gate_i], m[0, scale_i], m[1, scale_i], m[0, shift_i], m[1, shift_i]])

    xs, h = _join_norm(ctx[0], x[0], norm_g[0, 0][None], mod6(0, 0, 1, 2)[2:])
    for l in range(depth):
        p = _matmul(h, w_in_p, l, transposed_w=True, tn_cap=W_IN_COLS, name="w_in")
        r, v, kap, lw0, lw1, k0, k1, b0, b1, gs, y_b = _rwkv_prep_pool(
            p, mu_p[l], rwkv_w0[l], wup2[l], rwkv_a0[l], aup2[l], rwkv_k_k[l][None], rwkv_k_a[l][None],
            pool_w_b[l], pool_scale[l][None], n_ctx=L)
        yf, yr = _delta_scan([(False, (lw0, k0, b0, kap, v, r)), (True, (lw1, k1, b1, kap, v, r))], n_ctx=L)
        y_a = _rwkv_readout(yf, yr, r, v, k0, k1, gs, gup_p[l], rwkv_r_k[l][None], rwkv_ln_g[l][None],
                            rwkv_ln_b[l][None])
        y_c = _attention(p, cos_tab, sin_tab, sink16[l], n_ctx=L)
        acc = _merge(p, y_a, y_b, y_c, gate_up_b, wa_b, wb_b, wc_b, l)
        mix = _matmul(acc, w_out, l, out_dtype=BF16, tm_cap=MM_ROWS_WIDE, name="w_out")
        xs, h2 = _resid_norm(xs, mix, norm_g[l, 1:3], mod6(l, 3, 4, 2), n_ctx=L, emit_h=True)
        f = _matmul(_ffn_up(h2, ffn_w1, ffn_w3, l), w2_b, l, out_dtype=BF16, tm_cap=FFN_DOWN_ROWS, tk_cap=d_ff,
                    name="ffn_down")
        if l + 1 < depth:
            g2 = jnp.stack([norm_g[l, 3], norm_g[l + 1, 0]])
            m6 = jnp.concatenate([mod6(l, 0, 1, 5)[:2], mod6(l + 1, 0, 1, 2)[2:]], axis=0)
            xs, h = _resid_norm(xs, f, g2, m6, n_ctx=L, emit_h=True)
        else:
            xs, _ = _resid_norm(xs, f, jnp.stack([norm_g[l, 3], norm_g[l, 3]]), mod6(l, 0, 1, 5), n_ctx=L,
                                emit_h=False, latent_only=True)
    return xs[None]
```
